```python
import jax, jax.numpy as jnp
from jax import lax
import numpy as np

D_MODEL = 1024
BATCH = 4
SEQ = 4096
DEPTH = 2

CHUNK = 64
N_EVEN = (DEPTH + 1) // 2
N_ODD = DEPTH // 2
CONV_DIM = D_MODEL // 2
SHORT_CONV_K = 3
HGRN_WIDTH = D_MODEL // 2
HGRN_HEAD_DIM = 128
HGRN_HEADS = HGRN_WIDTH // HGRN_HEAD_DIM
IN_SPLITS = [CONV_DIM, 2 * CONV_DIM, 3 * CONV_DIM,
             3 * CONV_DIM + HGRN_WIDTH, 3 * CONV_DIM + 2 * HGRN_WIDTH,
             3 * CONV_DIM + 3 * HGRN_WIDTH]
IN_PROJ_DIM = 3 * CONV_DIM + 4 * HGRN_WIDTH
CONF_K = 31
D_FF = 2816
N_EXPERTS = 8
TOP_K = 2
D_FF_EXPERT = 3584
EPS = 1e-6

kernel_name = "hybrid_shortconv_hgrn2_conformer_moe"


def rmsnorm(x, g):
    xf = x.astype(jnp.float32)
    y = xf * lax.rsqrt(jnp.mean(xf * xf, axis=-1, keepdims=True) + EPS) * g.astype(jnp.float32)
    return y.astype(x.dtype)


def layernorm(x, g, b):
    xf = x.astype(jnp.float32)
    mu = jnp.mean(xf, axis=-1, keepdims=True)
    xc = xf - mu
    var = jnp.mean(xc * xc, axis=-1, keepdims=True)
    y = xc * lax.rsqrt(var + EPS) * g.astype(jnp.float32) + b.astype(jnp.float32)
    return y.astype(x.dtype)


def causal_depthwise_conv(x, w):
    K, C = w.shape
    return lax.conv_general_dilated(
        x, w[:, None, :].astype(x.dtype), window_strides=(1,), padding=[(K - 1, 0)],
        dimension_numbers=('NWC', 'WIO', 'NWC'), feature_group_count=C)


def hgrn2_chunkwise(q, k, v, log_f):
    Bsz, T, H, dk = q.shape
    dv = v.shape[-1]
    nc = T // CHUNK

    def to_chunks(a):
        return a.reshape(Bsz, nc, CHUNK, H, a.shape[-1]).transpose(1, 0, 3, 2, 4)

    qc, kc, vc, gc = to_chunks(q), to_chunks(k), to_chunks(v), to_chunks(log_f)
    causal = jnp.tril(jnp.ones((CHUNK, CHUNK), dtype=bool))

    def step(S, inp):
        q_, k_, v_, g_ = inp
        b = jnp.cumsum(g_, axis=2)
        diff = b[:, :, :, None, :] - b[:, :, None, :, :]
        decay = jnp.exp(jnp.where(causal[None, None, :, :, None], diff, -jnp.inf))
        scores = jnp.einsum('bhtd,bhsd,bhtsd->bhts', q_, k_, decay)
        o = (jnp.einsum('bhts,bhse->bhte', scores, v_)
             + jnp.einsum('bhtd,bhde->bhte', q_ * jnp.exp(b), S))
        b_last = b[:, :, -1:, :]
        S = (jnp.exp(b_last[:, :, 0, :])[..., None] * S
             + jnp.einsum('bhsd,bhse->bhde', k_ * jnp.exp(b_last - b), v_))
        return S, o

    S0 = jnp.zeros((Bsz, H, dk, dv), jnp.float32)
    _, o = lax.scan(step, S0, (qc, kc, vc, gc))
    return o.transpose(1, 0, 3, 2, 4).reshape(Bsz, T, H, dv)


def even_mixer(h, w_in, conv_w, lb, gnorm_w, w_out):
    Bsz, T, _ = h.shape
    f32 = jnp.float32
    proj = h @ w_in
    a_x, a_b, a_c, q, f, i, g = jnp.split(proj, IN_SPLITS, axis=-1)
    y_a = a_b * causal_depthwise_conv(a_c * a_x, conv_w)
    fg = lb + (1.0 - lb) * jax.nn.sigmoid(f.astype(f32))
    heads = lambda a: a.reshape(Bsz, T, HGRN_HEADS, HGRN_HEAD_DIM)
    o = hgrn2_chunkwise(heads(q.astype(f32)), heads(1.0 - fg), heads(i.astype(f32)), heads(jnp.log(fg)))
    o = rmsnorm(o, gnorm_w) * jax.nn.silu(heads(g.astype(f32)))
    y_b = o.reshape(Bsz, T, HGRN_WIDTH).astype(h.dtype)
    return jnp.concatenate([y_a, y_b], axis=-1) @ w_out


def conformer_conv(h, w_pw1, b_pw1, w_dw, b_dw, ln_g, ln_b, w_pw2, b_pw2):
    u = jax.nn.glu(h @ w_pw1 + b_pw1, axis=-1)
    u = causal_depthwise_conv(u, w_dw) + b_dw
    u = jax.nn.silu(layernorm(u, ln_g, ln_b))
    return u @ w_pw2 + b_pw2


def swiglu(h, w1, w3, w2):
    return (jax.nn.silu(h @ w1) * (h @ w3)) @ w2


def moe_swiglu(h, w_router, w1, w3, w2):
    Bsz, T, D = h.shape
    hf = h.reshape(Bsz * T, D)
    logits = (hf @ w_router).astype(jnp.float32)
    top_val, top_idx = lax.top_k(logits, TOP_K)
    gates = jax.nn.softmax(top_val, axis=-1)
    dense_gate = jnp.sum(jax.nn.one_hot(top_idx, N_EXPERTS, dtype=jnp.float32) * gates[..., None], axis=1)
    dense_gate = dense_gate.astype(h.dtype)
    out = jnp.zeros_like(hf)
    for e in range(N_EXPERTS):
        out = out + dense_gate[:, e:e + 1] * swiglu(hf, w1[e], w3[e], w2[e])
    return out.reshape(Bsz, T, D)


def setup_inputs(seed: int = 0) -> dict:
    key = jax.random.key(seed)
    ks = jax.random.split(key, 32)
    D = D_MODEL
    nrm = lambda k, shape, fan_in: jax.random.normal(k, shape, jnp.float32) * (fan_in ** -0.5)
    gain = lambda k, shape: 1.0 + 0.02 * jax.random.normal(k, shape, jnp.float32)
    small = lambda k, shape: 0.01 * jax.random.normal(k, shape, jnp.float32)
    return {
        'x': jax.random.normal(ks[0], (BATCH, SEQ, D), jnp.float32),
        'ev_norm_mix': gain(ks[1], (N_EVEN, D)),
        'ev_w_in': nrm(ks[2], (N_EVEN, D, IN_PROJ_DIM), D),
        'ev_conv_w': nrm(ks[3], (N_EVEN, SHORT_CONV_K, CONV_DIM), SHORT_CONV_K),
        'hgrn_lower_bounds': 0.1 * jax.random.normal(ks[4], (DEPTH + 1, HGRN_WIDTH), jnp.float32),
        'ev_gnorm': gain(ks[5], (N_EVEN, HGRN_HEAD_DIM)),
        'ev_w_out': nrm(ks[6], (N_EVEN, D, D), D),
        'ev_norm_ffn': gain(ks[7], (N_EVEN, D)),
        'ev_ffn_w1': nrm(ks[8], (N_EVEN, D, D_FF), D),
        'ev_ffn_w3': nrm(ks[9], (N_EVEN, D, D_FF), D),
        'ev_ffn_w2': nrm(ks[10], (N_EVEN, D_FF, D), D_FF),
        'od_norm_mix': gain(ks[11], (N_ODD, D)),
        'od_w_pw1': nrm(ks[12], (N_ODD, D, 2 * D), D),
        'od_b_pw1': small(ks[13], (N_ODD, 2 * D)),
        'od_w_dw': nrm(ks[14], (N_ODD, CONF_K, D), CONF_K),
        'od_b_dw': small(ks[15], (N_ODD, D)),
        'od_ln_g': gain(ks[16], (N_ODD, D)),
        'od_ln_b': small(ks[17], (N_ODD, D)),
        'od_w_pw2': nrm(ks[18], (N_ODD, D, D), D),
        'od_b_pw2': small(ks[19], (N_ODD, D)),
        'od_norm_ffn': gain(ks[20], (N_ODD, D)),
        'od_router': nrm(ks[21], (N_ODD, D, N_EXPERTS), D),
        'od_moe_w1': nrm(ks[22], (N_ODD, N_EXPERTS, D, D_FF_EXPERT), D),
        'od_moe_w3': nrm(ks[23], (N_ODD, N_EXPERTS, D, D_FF_EXPERT), D),
        'od_moe_w2': nrm(ks[24], (N_ODD, N_EXPERTS, D_FF_EXPERT, D), D_FF_EXPERT),
        'final_norm': gain(ks[25], (D,)),
    }


def reference(x, ev_norm_mix, ev_w_in, ev_conv_w, hgrn_lower_bounds, ev_gnorm, ev_w_out,
              ev_norm_ffn, ev_ffn_w1, ev_ffn_w3, ev_ffn_w2,
              od_norm_mix, od_w_pw1, od_b_pw1, od_w_dw, od_b_dw, od_ln_g, od_ln_b,
              od_w_pw2, od_b_pw2, od_norm_ffn, od_router, od_moe_w1, od_moe_w3, od_moe_w2,
              final_norm):
    lb_sched = jnp.cumsum(jax.nn.softmax(hgrn_lower_bounds.astype(jnp.float32), axis=0), axis=0)
    h = x
    for l in range(DEPTH):
        j = l // 2
        if l % 2 == 0:
            h = h + even_mixer(rmsnorm(h, ev_norm_mix[j]), ev_w_in[j], ev_conv_w[j],
                               lb_sched[l], ev_gnorm[j], ev_w_out[j])
            h = h + swiglu(rmsnorm(h, ev_norm_ffn[j]), ev_ffn_w1[j], ev_ffn_w3[j], ev_ffn_w2[j])
        else:
            h = h + conformer_conv(rmsnorm(h, od_norm_mix[j]), od_w_pw1[j], od_b_pw1[j],
                                   od_w_dw[j], od_b_dw[j], od_ln_g[j], od_ln_b[j],
                                   od_w_pw2[j], od_b_pw2[j])
            h = h + moe_swiglu(rmsnorm(h, od_norm_ffn[j]), od_router[j],
                               od_moe_w1[j], od_moe_w3[j], od_moe_w2[j])
    return rmsnorm(h, final_norm)
```

```python
import functools

import jax
import jax.numpy as jnp
from jax import lax
from jax.experimental import pallas as pl
from jax.experimental.pallas import tpu as pltpu

F32 = jnp.float32
BF16 = jnp.bfloat16
I32 = jnp.int32

EPS = 1e-6
CHUNK = 64
SUB = 8
HEAD_DIM = 128
TOP_K = 2

V7X_LANES = 128
V7X_MXU_DIM = 256
V7X_VMEM_BYTES = 64 * 1024 * 1024
VMEM_LIMIT = V7X_VMEM_BYTES - 8 * 1024 * 1024


def _cparams(n_axes):
    return pltpu.CompilerParams(
        dimension_semantics=("arbitrary",) * n_axes, vmem_limit_bytes=VMEM_LIMIT)


def _const_spec(shape):
    nd = len(shape)
    return pl.BlockSpec(shape, lambda *_: (0,) * nd, pipeline_mode=pl.Buffered(1))


def _rms(x, g):
    return x * lax.rsqrt(jnp.mean(x * x, axis=-1, keepdims=True) + EPS) * g


def _silu(x):
    return x * jax.nn.sigmoid(x)


def _dot(a, b):
    return jnp.dot(a, b, preferred_element_type=F32)


def _dot_nt(a, b):
    return lax.dot_general(a, b, (((1,), (1,)), ((), ())), preferred_element_type=F32)


def _mixer0_kernel(x_ref, gmix_ref, win_ref, convw_ref, lbp_ref, gn_ref, wout_ref, o_ref,
                   proj_ref, zbuf_ref, st_ref, y_ref, w_ref, *, layer, conv_dim, width):
    tt = x_ref.shape[1]
    heads = width // HEAD_DIM
    c3 = 3 * conv_dim

    @pl.when(pl.program_id(1) == 0)
    def _():
        st_ref[...] = jnp.zeros_like(st_ref)
        zbuf_ref[0:SUB, :] = jnp.zeros((SUB, conv_dim), F32)

    x = x_ref[0]
    hn = _rms(x, gmix_ref[...])
    proj_ref[...] = _dot(hn.astype(BF16), win_ref[...])

    z = proj_ref[:, 2 * conv_dim:c3] * proj_ref[:, 0:conv_dim]
    zbuf_ref[SUB:SUB + tt, :] = z
    cw = convw_ref[...]
    conv = (cw[2:3, :] * z + cw[1:2, :] * zbuf_ref[SUB - 1:SUB - 1 + tt, :]
            + cw[0:1, :] * zbuf_ref[SUB - 2:SUB - 2 + tt, :])
    y_ref[:, 0:conv_dim] = (proj_ref[:, conv_dim:2 * conv_dim] * conv).astype(BF16)
    zbuf_ref[0:SUB, :] = zbuf_ref[tt:tt + SUB, :]

    lbp = lbp_ref[...]
    lbe = jnp.exp(lbp - jnp.max(lbp, axis=0, keepdims=True))
    lb = jnp.sum(lbe[0:layer + 1, :], axis=0, keepdims=True) / jnp.sum(lbe, axis=0, keepdims=True)

    row = lax.broadcasted_iota(I32, (CHUNK, width), 0)
    row8 = lax.broadcasted_iota(I32, (SUB, width), 0)
    ti = lax.broadcasted_iota(I32, (CHUNK, CHUNK), 0)
    si = lax.broadcasted_iota(I32, (CHUNK, CHUNK), 1)
    tri = (si <= ti).astype(F32)
    bdr = lax.broadcasted_iota(I32, (V7X_MXU_DIM, V7X_MXU_DIM), 0) // HEAD_DIM
    bdc = lax.broadcasted_iota(I32, (V7X_MXU_DIM, V7X_MXU_DIM), 1) // HEAD_DIM
    head_ones = (bdr == bdc).astype(BF16)
    gn = gn_ref[...]

    def chunk_body(c, carry):
        r0 = pl.multiple_of(c * CHUNK, CHUNK)
        rows = pl.ds(r0, CHUNK)
        q = proj_ref[rows, c3:c3 + width]
        f = proj_ref[rows, c3 + width:c3 + 2 * width]
        v = proj_ref[rows, c3 + 2 * width:c3 + 3 * width]
        og = proj_ref[rows, c3 + 3 * width:c3 + 4 * width]
        fg = lb + (1.0 - lb) * jax.nn.sigmoid(f)
        k = 1.0 - fg
        b = jnp.dot(tri, jnp.log(fg), preferred_element_type=F32, precision=lax.Precision.HIGHEST)
        b_last = b[CHUNK - 1:CHUNK, :]
        v16 = v.astype(BF16)

        q_s = (q * jnp.exp(b)).astype(BF16)
        k_s = (k * jnp.exp(b_last - b)).astype(BF16)
        e_last = jnp.exp(b_last)

        p_acc = [jnp.zeros((CHUNK, CHUNK), F32) for _ in range(heads)]
        half = CHUNK // 2
        while half >= SUB:
            blk = 2 * half
            anc = jnp.concatenate(
                [jnp.broadcast_to(b[i * blk + half:i * blk + half + 1, :], (blk, width))
                 for i in range(CHUNK // blk)], axis=0)
            upper = (row % blk) >= half
            q_l = (q * jnp.where(upper, jnp.exp(b - anc), 0.0)).astype(BF16)
            k_l = (k * jnp.where(upper, 0.0, jnp.exp(anc - b))).astype(BF16)
            same = (ti // blk) == (si // blk)
            for h in range(heads):
                hs = slice(h * HEAD_DIM, (h + 1) * HEAD_DIM)
                p_acc[h] = p_acc[h] + jnp.where(same, _dot_nt(q_l[:, hs], k_l[:, hs]), 0.0)
            half //= 2

        o_diag = []
        for rb in range(CHUNK // SUB):
            rs = slice(rb * SUB, (rb + 1) * SUB)
            b8, q8, k8 = b[rs, :], q[rs, :], k[rs, :]
            for s in range(0, SUB, 2):
                pair = []
                for s1 in (s, s + 1):
                    dec = jnp.where(row8 >= s1, jnp.exp(b8 - b8[s1:s1 + 1, :]), 0.0)
                    pair.append(q8 * dec * k8[s1:s1 + 1, :])
                w_ref[(rb * SUB + s) * SUB:(rb * SUB + s + 2) * SUB, :] = (
                    jnp.concatenate(pair, axis=0).astype(BF16))
        for rb in range(CHUNK // SUB):
            wrows = slice(rb * SUB * SUB, (rb + 1) * SUB * SUB)
            sc = jnp.concatenate(
                [_dot(w_ref[wrows, j * V7X_MXU_DIM:(j + 1) * V7X_MXU_DIM], head_ones)
                 for j in range(width // V7X_MXU_DIM)], axis=1)
            v8 = v[rb * SUB:(rb + 1) * SUB, :]
            acc = sc[0:SUB, :] * v8[0:1, :]
            for s in range(1, SUB):
                acc = acc + sc[s * SUB:(s + 1) * SUB, :] * v8[s:s + 1, :]
            o_diag.append(acc)
        o_diag = jnp.concatenate(o_diag, axis=0)

        outs = []
        for h in range(heads):
            hs = slice(h * HEAD_DIM, (h + 1) * HEAD_DIM)
            st = st_ref[h]
            o_h = (_dot_nt(q_s[:, hs], st.astype(BF16)) + _dot(p_acc[h].astype(BF16), v16[:, hs])
                   + o_diag[:, hs])
            st_ref[h] = e_last[:, hs] * st + _dot(v16[:, hs].T, k_s[:, hs])
            o_n = o_h * lax.rsqrt(jnp.mean(o_h * o_h, axis=-1, keepdims=True) + EPS)
            outs.append(o_n)
        o = jnp.concatenate(outs, axis=1) * gn * _silu(og)
        y_ref[rows, conv_dim:conv_dim + width] = o.astype(BF16)
        return carry

    lax.fori_loop(0, tt // CHUNK, chunk_body, 0)
    o_ref[0] = x + _dot(y_ref[...], wout_ref[...])


def _mixer0(x, gmix, w_in, conv_w, lbp, gnorm, w_out, layer, tt=256):
    bsz, seq, d = x.shape
    conv_dim = conv_w.shape[1]
    width = lbp.shape[1]
    n_in = w_in.shape[1]
    tt = min(tt, seq)
    heads = width // HEAD_DIM
    gn = jnp.tile(gnorm.reshape(1, HEAD_DIM), (1, heads))
    kern = functools.partial(_mixer0_kernel, layer=layer, conv_dim=conv_dim, width=width)
    return pl.pallas_call(
        kern,
        grid=(bsz, seq // tt),
        in_specs=[
            pl.BlockSpec((1, tt, d), lambda b, t: (b, t, 0)),
            _const_spec((1, d)),
            _const_spec((d, n_in)),
            _const_spec(conv_w.shape),
            _const_spec(lbp.shape),
            _const_spec((1, width)),
            _const_spec(w_out.shape),
        ],
        out_specs=pl.BlockSpec((1, tt, d), lambda b, t: (b, t, 0)),
        out_shape=jax.ShapeDtypeStruct(x.shape, F32),
        scratch_shapes=[
            pltpu.VMEM((tt, n_in), F32),
            pltpu.VMEM((tt + SUB, conv_dim), F32),
            pltpu.VMEM((heads, HEAD_DIM, HEAD_DIM), F32),
            pltpu.VMEM((tt, conv_dim + width), BF16),
            pltpu.VMEM((CHUNK * SUB, width), BF16),
        ],
        compiler_params=_cparams(2),
        name="mixer0",
    )(x, gmix.reshape(1, d), w_in.astype(BF16), conv_w, lbp, gn, w_out.astype(BF16))


def _ffn_kernel(x_ref, g_ref, w1_ref, w3_ref, w2_ref, o_ref, *, tf):
    x = x_ref[...]
    hn = _rms(x, g_ref[...]).astype(BF16)
    acc = x
    for j in range(w1_ref.shape[1] // tf):
        cols = slice(j * tf, (j + 1) * tf)
        h = _silu(_dot(hn, w1_ref[:, cols])) * _dot(hn, w3_ref[:, cols])
        acc = acc + _dot(h.astype(BF16), w2_ref[cols, :])
    o_ref[...] = acc


def _ffn(x2, g, w1, w3, w2, tm=512, tf=256):
    n, d = x2.shape
    tm = min(tm, n)
    return pl.pallas_call(
        functools.partial(_ffn_kernel, tf=tf),
        grid=(n // tm,),
        in_specs=[
            pl.BlockSpec((tm, d), lambda i: (i, 0)),
            _const_spec((1, d)),
            _const_spec(w1.shape),
            _const_spec(w3.shape),
            _const_spec(w2.shape),
        ],
        out_specs=pl.BlockSpec((tm, d), lambda i: (i, 0)),
        out_shape=jax.ShapeDtypeStruct((n, d), F32),
        compiler_params=_cparams(1),
        name="ffn0",
    )(x2, g.reshape(1, d), w1.astype(BF16), w3.astype(BF16), w2.astype(BF16))


def _conformer_kernel(x_ref, g_ref, w1_ref, b1_ref, wdw_ref, bdw_ref, lng_ref, lnb_ref,
                      w2_ref, b2_ref, o_ref, ubuf_ref, *, halo):
    tt = x_ref.shape[1]
    d = x_ref.shape[2]
    kw = wdw_ref.shape[0]

    @pl.when(pl.program_id(1) == 0)
    def _():
        ubuf_ref[0:halo, :] = jnp.zeros((halo, d), F32)

    x = x_ref[0]
    hn = _rms(x, g_ref[...]).astype(BF16)
    p = _dot(hn, w1_ref[...]) + b1_ref[...]
    ubuf_ref[halo:halo + tt, :] = p[:, 0:d] * jax.nn.sigmoid(p[:, d:2 * d])
    off = halo - (kw - 1)
    acc = wdw_ref[0:1, :] * ubuf_ref[off:off + tt, :]
    for k in range(1, kw):
        acc = acc + wdw_ref[k:k + 1, :] * ubuf_ref[off + k:off + k + tt, :]
    acc = acc + bdw_ref[...]
    ubuf_ref[0:halo, :] = ubuf_ref[tt:tt + halo, :]
    mu = jnp.mean(acc, axis=-1, keepdims=True)
    xc = acc - mu
    var = jnp.mean(xc * xc, axis=-1, keepdims=True)
    u = _silu(xc * lax.rsqrt(var + EPS) * lng_ref[...] + lnb_ref[...])
    o_ref[0] = x + _dot(u.astype(BF16), w2_ref[...]) + b2_ref[...]


def _conformer(x, g, w_pw1, b_pw1, w_dw, b_dw, ln_g, ln_b, w_pw2, b_pw2, tt=256):
    bsz, seq, d = x.shape
    tt = min(tt, seq)
    kw = w_dw.shape[0]
    halo = -(-(kw - 1) // SUB) * SUB
    r = lambda a: a.reshape(1, -1)
    return pl.pallas_call(
        functools.partial(_conformer_kernel, halo=halo),
        grid=(bsz, seq // tt),
        in_specs=[
            pl.BlockSpec((1, tt, d), lambda b, t: (b, t, 0)),
            _const_spec((1, d)),
            _const_spec(w_pw1.shape),
            _const_spec((1, 2 * d)),
            _const_spec(w_dw.shape),
            _const_spec((1, d)),
            _const_spec((1, d)),
            _const_spec((1, d)),
            _const_spec(w_pw2.shape),
            _const_spec((1, d)),
        ],
        out_specs=pl.BlockSpec((1, tt, d), lambda b, t: (b, t, 0)),
        out_shape=jax.ShapeDtypeStruct(x.shape, F32),
        scratch_shapes=[pltpu.VMEM((tt + halo, d), F32)],
        compiler_params=_cparams(2),
        name="conformer",
    )(x, r(g), w_pw1.astype(BF16), r(b_pw1), w_dw, r(b_dw), r(ln_g), r(ln_b),
      w_pw2.astype(BF16), r(b_pw2))


def _router_kernel(x_ref, g_ref, wr_ref, hn_ref, exp_ref, rank_ref, gate_ref, cnt_ref, run_ref,
                   *, n_experts):
    tm = x_ref.shape[0]

    @pl.when(pl.program_id(0) == 0)
    def _():
        run_ref[...] = jnp.zeros_like(run_ref)

    hn = _rms(x_ref[...], g_ref[...])
    hn_ref[...] = hn
    logits = jnp.dot(hn, wr_ref[...], preferred_element_type=F32, precision=lax.Precision.HIGHEST)
    lane = lax.broadcasted_iota(I32, logits.shape, 1)
    neg = jnp.float32(-jnp.inf)
    big = jnp.int32(V7X_LANES)
    logits = jnp.where(lane < n_experts, logits, neg)
    m1 = jnp.max(logits, axis=-1, keepdims=True)
    i1 = jnp.min(jnp.where(logits == m1, lane, big), axis=-1, keepdims=True)
    rest = jnp.where(lane == i1, neg, logits)
    m2 = jnp.max(rest, axis=-1, keepdims=True)
    i2 = jnp.min(jnp.where(rest == m2, lane, big), axis=-1, keepdims=True)
    e2 = jnp.exp(m2 - m1)
    g1 = 1.0 / (1.0 + e2)
    g2 = e2 / (1.0 + e2)

    sel1 = lane == i1
    sel2 = lane == i2
    onehot = jnp.where(sel1 | sel2, 1.0, 0.0)
    ti = lax.broadcasted_iota(I32, (tm, tm), 0)
    si = lax.broadcasted_iota(I32, (tm, tm), 1)
    before = _dot((si < ti).astype(BF16), onehot.astype(BF16)) + run_ref[...]
    r1 = jnp.sum(jnp.where(sel1, before, 0.0), axis=-1, keepdims=True).astype(I32)
    r2 = jnp.sum(jnp.where(sel2, before, 0.0), axis=-1, keepdims=True).astype(I32)
    run_ref[...] = run_ref[...] + jnp.sum(onehot, axis=0, keepdims=True)

    col = lax.broadcasted_iota(I32, (tm, TOP_K), 1)
    exp_ref[...] = jnp.where(col == 0, i1, i2)
    rank_ref[...] = jnp.where(col == 0, r1, r2)
    gate_ref[...] = jnp.where(col == 0, g1, g2)
    cnt_ref[...] = run_ref[...].astype(I32)


def _router(x2, g, w_router, tm=512):
    n, d = x2.shape
    tm = min(tm, n)
    n_experts = w_router.shape[1]
    wr = jnp.zeros((d, V7X_LANES), F32).at[:, :n_experts].set(w_router)
    per_token = lambda dt: jax.ShapeDtypeStruct((n, TOP_K), dt)
    return pl.pallas_call(
        functools.partial(_router_kernel, n_experts=n_experts),
        grid=(n // tm,),
        in_specs=[
            pl.BlockSpec((tm, d), lambda i: (i, 0)),
            _const_spec((1, d)),
            _const_spec((d, V7X_LANES)),
        ],
        out_specs=[
            pl.BlockSpec((tm, d), lambda i: (i, 0)),
            pl.BlockSpec((tm, TOP_K), lambda i: (i, 0)),
            pl.BlockSpec((tm, TOP_K), lambda i: (i, 0)),
            pl.BlockSpec((tm, TOP_K), lambda i: (i, 0)),
            pl.BlockSpec((1, V7X_LANES), lambda i: (0, 0)),
        ],
        out_shape=[
            jax.ShapeDtypeStruct((n, d), F32),
            per_token(I32), per_token(I32), per_token(F32),
            jax.ShapeDtypeStruct((1, V7X_LANES), I32),
        ],
        scratch_shapes=[pltpu.VMEM((1, V7X_LANES), F32)],
        compiler_params=_cparams(1),
        name="router",
    )(x2, g.reshape(1, d), wr)


def _dispatch_kernel(slot_ref, pstart_ref, plen_ref, used_ref, hn_ref, xs_ref, zero_ref, sem, zsem,
                     *, n_experts, tile):
    tm = hn_ref.shape[0]
    n_tiles = xs_ref.shape[0] // tile

    def row_copy(r, k):
        dst = slot_ref[TOP_K * r + k]
        return pltpu.make_async_copy(hn_ref.at[pl.ds(r, 1)], xs_ref.at[pl.ds(dst, 1)], sem)

    def issue(r, carry):
        for k in range(TOP_K):
            row_copy(r, k).start()
        return carry

    lax.fori_loop(0, tm, issue, 0)

    def drain(r, carry):
        for k in range(TOP_K):
            row_copy(r, k).wait()
        return carry

    lax.fori_loop(0, tm, drain, 0)

    def for_each_pad_copy(act):
        for e in range(n_experts):
            start = pstart_ref[e]
            length = plen_ref[e]
            lead = jnp.minimum((SUB - (start & (SUB - 1))) & (SUB - 1), length)
            for r in range(SUB - 1):
                @pl.when(r < lead)
                def _():
                    act(pltpu.make_async_copy(
                        zero_ref.at[pl.ds(r, 1)], xs_ref.at[pl.ds(start + r, 1)], zsem))
            cur = start + lead
            rem = length - lead
            p = SUB
            while p < tile:
                take = (rem & p) != 0
                at = pl.multiple_of(cur, SUB)

                @pl.when(take)
                def _():
                    act(pltpu.make_async_copy(
                        zero_ref.at[pl.ds(0, p)], xs_ref.at[pl.ds(at, p)], zsem))
                cur = cur + jnp.where(take, p, 0)
                p *= 2
        for i in range(n_experts):
            t = used_ref[0] + i

            @pl.when(t < n_tiles)
            def _():
                act(pltpu.make_async_copy(
                    zero_ref, xs_ref.at[pl.ds(pl.multiple_of(t * tile, tile), tile)], zsem))

    @pl.when(pl.program_id(0) == pl.num_programs(0) - 1)
    def _():
        zero_ref[...] = jnp.zeros_like(zero_ref)
        for_each_pad_copy(lambda cp: cp.start())
        for_each_pad_copy(lambda cp: cp.wait())


def _dispatch(hn, slots_flat, pad_start, pad_len, used_tiles, n_tiles, tile, tm=512):
    n, d = hn.shape
    tm = min(tm, n)
    n_experts = pad_start.shape[0]
    smem = pl.BlockSpec(memory_space=pltpu.SMEM)
    return pl.pallas_call(
        functools.partial(_dispatch_kernel, n_experts=n_experts, tile=tile),
        grid=(n // tm,),
        in_specs=[
            pl.BlockSpec((TOP_K * tm,), lambda i: (i,), memory_space=pltpu.SMEM),
            smem, smem, smem,
            pl.BlockSpec((tm, d), lambda i: (i, 0)),
        ],
        out_specs=pl.BlockSpec(memory_space=pl.ANY),
        out_shape=jax.ShapeDtypeStruct((n_tiles * tile, d), F32),
        scratch_shapes=[
            pltpu.VMEM((tile, d), F32),
            pltpu.SemaphoreType.DMA(()),
            pltpu.SemaphoreType.DMA(()),
        ],
        compiler_params=_cparams(1),
        name="dispatch",
    )(slots_flat, pad_start, pad_len, used_tiles, hn)


def _grouped_kernel(exp_ref, used_ref, x_ref, w1_ref, w3_ref, w2_ref, o_ref, xb_ref, acc_ref):
    j = pl.program_id(1)
    last = pl.num_programs(1) - 1
    valid = pl.program_id(0) < used_ref[0]

    @pl.when(valid & (j == 0))
    def _():
        xb_ref[...] = x_ref[...].astype(BF16)
        acc_ref[...] = jnp.zeros_like(acc_ref)

    @pl.when(valid)
    def _():
        xb = xb_ref[...]
        h = _silu(_dot(xb, w1_ref[...])) * _dot(xb, w3_ref[...])
        acc_ref[...] += _dot(h.astype(BF16), w2_ref[...])

    @pl.when(valid & (j == last))
    def _():
        o_ref[...] = acc_ref[...]

    @pl.when(jnp.logical_not(valid) & (j == last))
    def _():
        o_ref[...] = jnp.zeros_like(o_ref)


def _grouped(xs, w1, w3, w2, tile_exp, used_tiles, tile, tf=512):
    rows, d = xs.shape
    nj = w1.shape[2] // tf

    def jsel(i, j, used):
        return jnp.where(i < used[0], j, nj - 1)

    grid_spec = pltpu.PrefetchScalarGridSpec(
        num_scalar_prefetch=2,
        grid=(rows // tile, nj),
        in_specs=[
            pl.BlockSpec((tile, d), lambda i, j, ex, us: (i, 0)),
            pl.BlockSpec((None, d, tf), lambda i, j, ex, us: (ex[i], 0, jsel(i, j, us))),
            pl.BlockSpec((None, d, tf), lambda i, j, ex, us: (ex[i], 0, jsel(i, j, us))),
            pl.BlockSpec((None, tf, d), lambda i, j, ex, us: (ex[i], jsel(i, j, us), 0)),
        ],
        out_specs=pl.BlockSpec((tile, d), lambda i, j, ex, us: (i, 0)),
        scratch_shapes=[pltpu.VMEM((tile, d), BF16), pltpu.VMEM((tile, d), F32)],
    )
    return pl.pallas_call(
        _grouped_kernel,
        grid_spec=grid_spec,
        out_shape=jax.ShapeDtypeStruct((rows, d), F32),
        compiler_params=_cparams(2),
        name="grouped_swiglu",
    )(tile_exp, used_tiles, xs, w1.astype(BF16), w3.astype(BF16), w2.astype(BF16))


def _combine_kernel(slot_ref, h_ref, gate_ref, g_ref, ys_ref, o_ref, buf_ref, sem):
    tm = h_ref.shape[0]

    def row_copy(r, k):
        src = slot_ref[TOP_K * r + k]
        return pltpu.make_async_copy(ys_ref.at[pl.ds(src, 1)], buf_ref.at[k, pl.ds(r, 1)], sem)

    def issue(r, carry):
        for k in range(TOP_K):
            row_copy(r, k).start()
        return carry

    lax.fori_loop(0, tm, issue, 0)

    def drain(r, carry):
        for k in range(TOP_K):
            row_copy(r, k).wait()
        return carry

    lax.fori_loop(0, tm, drain, 0)

    gate = gate_ref[...]
    out = h_ref[...] + gate[:, 0:1] * buf_ref[0] + gate[:, 1:2] * buf_ref[1]
    o_ref[...] = _rms(out, g_ref[...])


def _combine(h2, gates, slots_flat, ys, g, tm=512):
    n, d = h2.shape
    tm = min(tm, n)
    return pl.pallas_call(
        _combine_kernel,
        grid=(n // tm,),
        in_specs=[
            pl.BlockSpec((TOP_K * tm,), lambda i: (i,), memory_space=pltpu.SMEM),
            pl.BlockSpec((tm, d), lambda i: (i, 0)),
            pl.BlockSpec((tm, TOP_K), lambda i: (i, 0)),
            _const_spec((1, d)),
            pl.BlockSpec(memory_space=pl.ANY),
        ],
        out_specs=pl.BlockSpec((tm, d), lambda i: (i, 0)),
        out_shape=jax.ShapeDtypeStruct((n, d), F32),
        scratch_shapes=[pltpu.VMEM((TOP_K, tm, d), F32), pltpu.SemaphoreType.DMA(())],
        compiler_params=_cparams(1),
        name="combine",
    )(slots_flat, h2, gates, g.reshape(1, d), ys)


def _moe(h2, g_ffn, w_router, w1, w3, w2, g_final, tile=512):
    n, d = h2.shape
    n_experts = w_router.shape[1]
    tile = min(tile, n)
    n_tiles = TOP_K * n // tile + n_experts
    hn, expert, rank, gates, counts = _router(h2, g_ffn, w_router)
    counts = counts[0, :n_experts]

    tiles_per = (counts + tile - 1) // tile
    ends = jnp.cumsum(tiles_per)
    starts = ends - tiles_per
    used = ends[-1:]
    onehot = expert[:, :, None] == jnp.arange(n_experts, dtype=I32)
    slots = rank + jnp.sum(jnp.where(onehot, starts * tile, 0), axis=-1)
    slots_flat = slots.reshape(-1).astype(I32)
    pad_start = (starts * tile + counts).astype(I32)
    pad_len = (tiles_per * tile - counts).astype(I32)
    tidx = jnp.minimum(jnp.arange(n_tiles, dtype=I32), used - 1)
    tile_exp = jnp.sum((tidx[:, None] >= ends[None, :]).astype(I32), axis=1)

    xs = _dispatch(hn, slots_flat, pad_start, pad_len, used.astype(I32), n_tiles, tile)
    ys = _grouped(xs, w1, w3, w2, tile_exp.astype(I32), used.astype(I32), tile)
    return _combine(h2, gates, slots_flat, ys, g_final)


def kernel(x, ev_norm_mix, ev_w_in, ev_conv_w, hgrn_lower_bounds, ev_gnorm, ev_w_out,
           ev_norm_ffn, ev_ffn_w1, ev_ffn_w3, ev_ffn_w2,
           od_norm_mix, od_w_pw1, od_b_pw1, od_w_dw, od_b_dw, od_ln_g, od_ln_b,
           od_w_pw2, od_b_pw2, od_norm_ffn, od_router, od_moe_w1, od_moe_w3, od_moe_w2,
           final_norm):
    bsz, seq, d = x.shape
    assert ev_w_in.shape[0] == 1 and od_w_pw1.shape[0] == 1
    h = _mixer0(x, ev_norm_mix[0], ev_w_in[0], ev_conv_w[0], hgrn_lower_bounds, ev_gnorm[0],
                ev_w_out[0], layer=0)
    h = _ffn(h.reshape(bsz * seq, d), ev_norm_ffn[0], ev_ffn_w1[0], ev_ffn_w3[0], ev_ffn_w2[0])
    h = _conformer(h.reshape(bsz, seq, d), od_norm_mix[0], od_w_pw1[0], od_b_pw1[0], od_w_dw[0],
                   od_b_dw[0], od_ln_g[0], od_ln_b[0], od_w_pw2[0], od_b_pw2[0])
    out = _moe(h.reshape(bsz * seq, d), od_norm_ffn[0], od_router[0], od_moe_w1[0], od_moe_w3[0],
               od_moe_w2[0], final_norm)
    return out.reshape(bsz, seq, d)
```

```python
import functools

import jax
import jax.numpy as jnp
from jax import lax
from jax.experimental import pallas as pl
from jax.experimental.pallas import tpu as pltpu

F32 = jnp.float32
BF16 = jnp.bfloat16
I32 = jnp.int32

EPS = 1e-6
CHUNK = 64
SUB = 8
HEAD_DIM = 128
TOP_K = 2

V7X_LANES = 128
V7X_MXU_DIM = 256
V7X_VMEM_BYTES = 64 * 1024 * 1024
VMEM_LIMIT = V7X_VMEM_BYTES - 8 * 1024 * 1024


def _cparams(n_axes):
    return pltpu.CompilerParams(
        dimension_semantics=("arbitrary",) * n_axes, vmem_limit_bytes=VMEM_LIMIT)


def _const_spec(shape):
    nd = len(shape)
    return pl.BlockSpec(shape, lambda *_: (0,) * nd, pipeline_mode=pl.Buffered(1))


def _rms(x, g):
    return x * lax.rsqrt(jnp.mean(x * x, axis=-1, keepdims=True) + EPS) * g


def _silu(x):
    return x * jax.nn.sigmoid(x)


def _dot(a, b):
    return jnp.dot(a, b, preferred_element_type=F32)


def _dot_nt(a, b):
    return lax.dot_general(a, b, (((1,), (1,)), ((), ())), preferred_element_type=F32)


def _store_row_tiles(ref, val):
    m = val.shape[0]
    for c in range(SUB):
        ref[pl.ds(c, m, stride=SUB), :] = val[:, c * V7X_LANES:(c + 1) * V7X_LANES]


def _load_row_tiles(ref, m, c):
    return ref[pl.ds(c, m, stride=SUB), :]


def _row_tile(ref, row):
    return ref.at[pl.ds(pl.multiple_of(row * SUB, SUB), SUB)]


ROW_COPY_UNROLL = 8


def _start_row_copies(n_rows, row_copy):
    def body(i, carry):
        for u in range(ROW_COPY_UNROLL):
            for k in range(TOP_K):
                row_copy(i * ROW_COPY_UNROLL + u, k).start(priority=(u * TOP_K + k) % 2)
        return carry

    lax.fori_loop(0, n_rows // ROW_COPY_UNROLL, body, 0)


def _mixer0_kernel(x_ref, gmix_ref, win_ref, convw_ref, lbp_ref, gn_ref, wout_ref, o_ref,
                   proj_ref, zbuf_ref, st_ref, y_ref, w_ref, *, layer, conv_dim, width):
    tt = x_ref.shape[1]
    heads = width // HEAD_DIM
    c3 = 3 * conv_dim

    @pl.when(pl.program_id(1) == 0)
    def _():
        st_ref[...] = jnp.zeros_like(st_ref)
        zbuf_ref[0:SUB, :] = jnp.zeros((SUB, conv_dim), F32)

    x = x_ref[0]
    hn = _rms(x, gmix_ref[...])
    proj_ref[...] = _dot(hn.astype(BF16), win_ref[...])

    z = proj_ref[:, 2 * conv_dim:c3] * proj_ref[:, 0:conv_dim]
    zbuf_ref[SUB:SUB + tt, :] = z
    cw = convw_ref[...]
    conv = (cw[2:3, :] * z + cw[1:2, :] * zbuf_ref[SUB - 1:SUB - 1 + tt, :]
            + cw[0:1, :] * zbuf_ref[SUB - 2:SUB - 2 + tt, :])
    y_ref[:, 0:conv_dim] = (proj_ref[:, conv_dim:2 * conv_dim] * conv).astype(BF16)
    zbuf_ref[0:SUB, :] = zbuf_ref[tt:tt + SUB, :]

    lbp = lbp_ref[...]
    lbe = jnp.exp(lbp - jnp.max(lbp, axis=0, keepdims=True))
    lb = jnp.sum(lbe[0:layer + 1, :], axis=0, keepdims=True) / jnp.sum(lbe, axis=0, keepdims=True)

    row = lax.broadcasted_iota(I32, (CHUNK, width), 0)
    row8 = lax.broadcasted_iota(I32, (SUB, width), 0)
    ti = lax.broadcasted_iota(I32, (CHUNK, CHUNK), 0)
    si = lax.broadcasted_iota(I32, (CHUNK, CHUNK), 1)
    tri = (si <= ti).astype(F32)
    bdr = lax.broadcasted_iota(I32, (V7X_MXU_DIM, V7X_MXU_DIM), 0) // HEAD_DIM
    bdc = lax.broadcasted_iota(I32, (V7X_MXU_DIM, V7X_MXU_DIM), 1) // HEAD_DIM
    head_ones = (bdr == bdc).astype(BF16)
    gn = gn_ref[...]

    def chunk_body(c, carry):
        r0 = pl.multiple_of(c * CHUNK, CHUNK)
        rows = pl.ds(r0, CHUNK)
        q = proj_ref[rows, c3:c3 + width]
        f = proj_ref[rows, c3 + width:c3 + 2 * width]
        v = proj_ref[rows, c3 + 2 * width:c3 + 3 * width]
        og = proj_ref[rows, c3 + 3 * width:c3 + 4 * width]
        fg = lb + (1.0 - lb) * jax.nn.sigmoid(f)
        k = 1.0 - fg
        b = jnp.dot(tri, jnp.log(fg), preferred_element_type=F32, precision=lax.Precision.HIGHEST)
        b_last = b[CHUNK - 1:CHUNK, :]
        v16 = v.astype(BF16)

        q_s = (q * jnp.exp(b)).astype(BF16)
        k_s = (k * jnp.exp(b_last - b)).astype(BF16)
        e_last = jnp.exp(b_last)

        p_acc = [jnp.zeros((CHUNK, CHUNK), F32) for _ in range(heads)]
        half = CHUNK // 2
        while half >= SUB:
            blk = 2 * half
            anc = jnp.concatenate(
                [jnp.broadcast_to(b[i * blk + half:i * blk + half + 1, :], (blk, width))
                 for i in range(CHUNK // blk)], axis=0)
            upper = (row % blk) >= half
            q_l = (q * jnp.where(upper, jnp.exp(b - anc), 0.0)).astype(BF16)
            k_l = (k * jnp.where(upper, 0.0, jnp.exp(anc - b))).astype(BF16)
            same = (ti // blk) == (si // blk)
            for h in range(heads):
                hs = slice(h * HEAD_DIM, (h + 1) * HEAD_DIM)
                p_acc[h] = p_acc[h] + jnp.where(same, _dot_nt(q_l[:, hs], k_l[:, hs]), 0.0)
            half //= 2

        o_diag = []
        for rb in range(CHUNK // SUB):
            rs = slice(rb * SUB, (rb + 1) * SUB)
            b8, q8, k8 = b[rs, :], q[rs, :], k[rs, :]
            for s in range(0, SUB, 2):
                pair = []
                for s1 in (s, s + 1):
                    dec = jnp.where(row8 >= s1, jnp.exp(b8 - b8[s1:s1 + 1, :]), 0.0)
                    pair.append(q8 * dec * k8[s1:s1 + 1, :])
                w_ref[(rb * SUB + s) * SUB:(rb * SUB + s + 2) * SUB, :] = (
                    jnp.concatenate(pair, axis=0).astype(BF16))
        for rb in range(CHUNK // SUB):
            wrows = slice(rb * SUB * SUB, (rb + 1) * SUB * SUB)
            sc = jnp.concatenate(
                [_dot(w_ref[wrows, j * V7X_MXU_DIM:(j + 1) * V7X_MXU_DIM], head_ones)
                 for j in range(width // V7X_MXU_DIM)], axis=1)
            v8 = v[rb * SUB:(rb + 1) * SUB, :]
            acc = sc[0:SUB, :] * v8[0:1, :]
            for s in range(1, SUB):
                acc = acc + sc[s * SUB:(s + 1) * SUB, :] * v8[s:s + 1, :]
            o_diag.append(acc)
        o_diag = jnp.concatenate(o_diag, axis=0)

        outs = []
        for h in range(heads):
            hs = slice(h * HEAD_DIM, (h + 1) * HEAD_DIM)
            st = st_ref[h]
            o_h = (_dot_nt(q_s[:, hs], st.astype(BF16)) + _dot(p_acc[h].astype(BF16), v16[:, hs])
                   + o_diag[:, hs])
            st_ref[h] = e_last[:, hs] * st + _dot(v16[:, hs].T, k_s[:, hs])
            o_n = o_h * lax.rsqrt(jnp.mean(o_h * o_h, axis=-1, keepdims=True) + EPS)
            outs.append(o_n)
        o = jnp.concatenate(outs, axis=1) * gn * _silu(og)
        y_ref[rows, conv_dim:conv_dim + width] = o.astype(BF16)
        return carry

    lax.fori_loop(0, tt // CHUNK, chunk_body, 0)
    o_ref[0] = x + _dot(y_ref[...], wout_ref[...])


def _mixer0(x, gmix, w_in, conv_w, lbp, gnorm, w_out, layer, tt=256):
    bsz, seq, d = x.shape
    conv_dim = conv_w.shape[1]
    width = lbp.shape[1]
    n_in = w_in.shape[1]
    tt = min(tt, seq)
    heads = width // HEAD_DIM
    gn = jnp.tile(gnorm.reshape(1, HEAD_DIM), (1, heads))
    kern = functools.partial(_mixer0_kernel, layer=layer, conv_dim=conv_dim, width=width)
    return pl.pallas_call(
        kern,
        grid=(bsz, seq // tt),
        in_specs=[
            pl.BlockSpec((1, tt, d), lambda b, t: (b, t, 0)),
            _const_spec((1, d)),
            _const_spec((d, n_in)),
            _const_spec(conv_w.shape),
            _const_spec(lbp.shape),
            _const_spec((1, width)),
            _const_spec(w_out.shape),
        ],
        out_specs=pl.BlockSpec((1, tt, d), lambda b, t: (b, t, 0)),
        out_shape=jax.ShapeDtypeStruct(x.shape, F32),
        scratch_shapes=[
            pltpu.VMEM((tt, n_in), F32),
            pltpu.VMEM((tt + SUB, conv_dim), F32),
            pltpu.VMEM((heads, HEAD_DIM, HEAD_DIM), F32),
            pltpu.VMEM((tt, conv_dim + width), BF16),
            pltpu.VMEM((CHUNK * SUB, width), BF16),
        ],
        compiler_params=_cparams(2),
        name="mixer0",
    )(x, gmix.reshape(1, d), w_in.astype(BF16), conv_w, lbp, gn, w_out.astype(BF16))


def _ffn_kernel(x_ref, g_ref, w1_ref, w3_ref, w2_ref, o_ref, *, tf):
    x = x_ref[...]
    hn = _rms(x, g_ref[...]).astype(BF16)
    acc = x
    for j in range(w1_ref.shape[1] // tf):
        cols = slice(j * tf, (j + 1) * tf)
        h = _silu(_dot(hn, w1_ref[:, cols])) * _dot(hn, w3_ref[:, cols])
        acc = acc + _dot(h.astype(BF16), w2_ref[cols, :])
    o_ref[...] = acc


def _ffn(x2, g, w1, w3, w2, tm=512, tf=256):
    n, d = x2.shape
    tm = min(tm, n)
    return pl.pallas_call(
        functools.partial(_ffn_kernel, tf=tf),
        grid=(n // tm,),
        in_specs=[
            pl.BlockSpec((tm, d), lambda i: (i, 0)),
            _const_spec((1, d)),
            _const_spec(w1.shape),
            _const_spec(w3.shape),
            _const_spec(w2.shape),
        ],
        out_specs=pl.BlockSpec((tm, d), lambda i: (i, 0)),
        out_shape=jax.ShapeDtypeStruct((n, d), F32),
        compiler_params=_cparams(1),
        name="ffn0",
    )(x2, g.reshape(1, d), w1.astype(BF16), w3.astype(BF16), w2.astype(BF16))


def _conformer_kernel(x_ref, g_ref, w1_ref, b1_ref, wdw_ref, bdw_ref, lng_ref, lnb_ref,
                      w2_ref, b2_ref, o_ref, ubuf_ref, *, halo):
    tt = x_ref.shape[1]
    d = x_ref.shape[2]
    kw = wdw_ref.shape[0]

    @pl.when(pl.program_id(1) == 0)
    def _():
        ubuf_ref[0:halo, :] = jnp.zeros((halo, d), F32)

    x = x_ref[0]
    hn = _rms(x, g_ref[...]).astype(BF16)
    p = _dot(hn, w1_ref[...]) + b1_ref[...]
    ubuf_ref[halo:halo + tt, :] = p[:, 0:d] * jax.nn.sigmoid(p[:, d:2 * d])
    off = halo - (kw - 1)
    acc = wdw_ref[0:1, :] * ubuf_ref[off:off + tt, :]
    for k in range(1, kw):
        acc = acc + wdw_ref[k:k + 1, :] * ubuf_ref[off + k:off + k + tt, :]
    acc = acc + bdw_ref[...]
    ubuf_ref[0:halo, :] = ubuf_ref[tt:tt + halo, :]
    mu = jnp.mean(acc, axis=-1, keepdims=True)
    xc = acc - mu
    var = jnp.mean(xc * xc, axis=-1, keepdims=True)
    u = _silu(xc * lax.rsqrt(var + EPS) * lng_ref[...] + lnb_ref[...])
    o_ref[0] = x + _dot(u.astype(BF16), w2_ref[...]) + b2_ref[...]


def _conformer(x, g, w_pw1, b_pw1, w_dw, b_dw, ln_g, ln_b, w_pw2, b_pw2, tt=256):
    bsz, seq, d = x.shape
    tt = min(tt, seq)
    kw = w_dw.shape[0]
    halo = -(-(kw - 1) // SUB) * SUB
    r = lambda a: a.reshape(1, -1)
    return pl.pallas_call(
        functools.partial(_conformer_kernel, halo=halo),
        grid=(bsz, seq // tt),
        in_specs=[
            pl.BlockSpec((1, tt, d), lambda b, t: (b, t, 0)),
            _const_spec((1, d)),
            _const_spec(w_pw1.shape),
            _const_spec((1, 2 * d)),
            _const_spec(w_dw.shape),
            _const_spec((1, d)),
            _const_spec((1, d)),
            _const_spec((1, d)),
            _const_spec(w_pw2.shape),
            _const_spec((1, d)),
        ],
        out_specs=pl.BlockSpec((1, tt, d), lambda b, t: (b, t, 0)),
        out_shape=jax.ShapeDtypeStruct(x.shape, F32),
        scratch_shapes=[pltpu.VMEM((tt + halo, d), F32)],
        compiler_params=_cparams(2),
        name="conformer",
    )(x, r(g), w_pw1.astype(BF16), r(b_pw1), w_dw, r(b_dw), r(ln_g), r(ln_b),
      w_pw2.astype(BF16), r(b_pw2))


def _router_kernel(x_ref, g_ref, wr_ref, hn_ref, exp_ref, rank_ref, gate_ref, cnt_ref, run_ref,
                   *, n_experts):
    tm = x_ref.shape[0]

    @pl.when(pl.program_id(0) == 0)
    def _():
        run_ref[...] = jnp.zeros_like(run_ref)

    hn = _rms(x_ref[...], g_ref[...])
    _store_row_tiles(hn_ref, hn)
    logits = jnp.dot(hn, wr_ref[...], preferred_element_type=F32, precision=lax.Precision.HIGHEST)
    lane = lax.broadcasted_iota(I32, logits.shape, 1)
    neg = jnp.float32(-jnp.inf)
    big = jnp.int32(V7X_LANES)
    logits = jnp.where(lane < n_experts, logits, neg)
    m1 = jnp.max(logits, axis=-1, keepdims=True)
    i1 = jnp.min(jnp.where(logits == m1, lane, big), axis=-1, keepdims=True)
    rest = jnp.where(lane == i1, neg, logits)
    m2 = jnp.max(rest, axis=-1, keepdims=True)
    i2 = jnp.min(jnp.where(rest == m2, lane, big), axis=-1, keepdims=True)
    e2 = jnp.exp(m2 - m1)
    g1 = 1.0 / (1.0 + e2)
    g2 = e2 / (1.0 + e2)

    sel1 = lane == i1
    sel2 = lane == i2
    onehot = jnp.where(sel1 | sel2, 1.0, 0.0)
    ti = lax.broadcasted_iota(I32, (tm, tm), 0)
    si = lax.broadcasted_iota(I32, (tm, tm), 1)
    before = _dot((si < ti).astype(BF16), onehot.astype(BF16)) + run_ref[...]
    r1 = jnp.sum(jnp.where(sel1, before, 0.0), axis=-1, keepdims=True).astype(I32)
    r2 = jnp.sum(jnp.where(sel2, before, 0.0), axis=-1, keepdims=True).astype(I32)
    run_ref[...] = run_ref[...] + jnp.sum(onehot, axis=0, keepdims=True)

    col = lax.broadcasted_iota(I32, (tm, TOP_K), 1)
    exp_ref[...] = jnp.where(col == 0, i1, i2)
    rank_ref[...] = jnp.where(col == 0, r1, r2)
    gate_ref[...] = jnp.where(col == 0, g1, g2)
    cnt_ref[...] = run_ref[...].astype(I32)


def _router(x2, g, w_router, tm=512):
    n, d = x2.shape
    assert d == SUB * V7X_LANES
    tm = min(tm, n)
    n_experts = w_router.shape[1]
    wr = jnp.zeros((d, V7X_LANES), F32).at[:, :n_experts].set(w_router)
    per_token = lambda dt: jax.ShapeDtypeStruct((n, TOP_K), dt)
    return pl.pallas_call(
        functools.partial(_router_kernel, n_experts=n_experts),
        grid=(n // tm,),
        in_specs=[
            pl.BlockSpec((tm, d), lambda i: (i, 0)),
            _const_spec((1, d)),
            _const_spec((d, V7X_LANES)),
        ],
        out_specs=[
            pl.BlockSpec((tm * SUB, V7X_LANES), lambda i: (i, 0)),
            pl.BlockSpec((tm, TOP_K), lambda i: (i, 0)),
            pl.BlockSpec((tm, TOP_K), lambda i: (i, 0)),
            pl.BlockSpec((tm, TOP_K), lambda i: (i, 0)),
            pl.BlockSpec((1, V7X_LANES), lambda i: (0, 0)),
        ],
        out_shape=[
            jax.ShapeDtypeStruct((n * SUB, V7X_LANES), F32),
            per_token(I32), per_token(I32), per_token(F32),
            jax.ShapeDtypeStruct((1, V7X_LANES), I32),
        ],
        scratch_shapes=[pltpu.VMEM((1, V7X_LANES), F32)],
        compiler_params=_cparams(1),
        name="router",
    )(x2, g.reshape(1, d), wr)


def _dispatch_kernel(slot_ref, pstart_ref, plen_ref, used_ref, hn_ref, xs_ref, zero_ref, sem, zsem,
                     *, n_experts, tile):
    tm = hn_ref.shape[0] // SUB
    n_tiles = xs_ref.shape[0] // (tile * SUB)

    def row_copy(r, k):
        return pltpu.make_async_copy(
            _row_tile(hn_ref, r), _row_tile(xs_ref, slot_ref[TOP_K * r + k]), sem)

    _start_row_copies(tm, row_copy)
    for k in range(TOP_K):
        pltpu.make_async_copy(hn_ref, xs_ref.at[pl.ds(0, tm * SUB)], sem).wait()

    def rows(ref, start, count):
        return ref.at[pl.ds(pl.multiple_of(start * SUB, SUB), count * SUB)]

    def for_each_pad_copy(act):
        for e in range(n_experts):
            cur = pstart_ref[e]
            length = plen_ref[e]
            p = 1
            while p < tile:
                take = (length & p) != 0

                @pl.when(take)
                def _():
                    act(pltpu.make_async_copy(rows(zero_ref, 0, p), rows(xs_ref, cur, p), zsem))
                cur = cur + jnp.where(take, p, 0)
                p *= 2
        for i in range(n_experts):
            t = used_ref[0] + i

            @pl.when(t < n_tiles)
            def _():
                act(pltpu.make_async_copy(zero_ref, rows(xs_ref, t * tile, tile), zsem))

    @pl.when(pl.program_id(0) == pl.num_programs(0) - 1)
    def _():
        zero_ref[...] = jnp.zeros_like(zero_ref)
        for_each_pad_copy(lambda cp: cp.start())
        for_each_pad_copy(lambda cp: cp.wait())


def _dispatch(hn, slots_flat, pad_start, pad_len, used_tiles, n_tiles, tile, tm=512):
    n = hn.shape[0] // SUB
    tm = min(tm, n)
    n_experts = pad_start.shape[0]
    smem = pl.BlockSpec(memory_space=pltpu.SMEM)
    return pl.pallas_call(
        functools.partial(_dispatch_kernel, n_experts=n_experts, tile=tile),
        grid=(n // tm,),
        in_specs=[
            pl.BlockSpec((TOP_K * tm,), lambda i: (i,), memory_space=pltpu.SMEM),
            smem, smem, smem,
            pl.BlockSpec((tm * SUB, V7X_LANES), lambda i: (i, 0)),
        ],
        out_specs=pl.BlockSpec(memory_space=pl.ANY),
        out_shape=jax.ShapeDtypeStruct((n_tiles * tile * SUB, V7X_LANES), F32),
        scratch_shapes=[
            pltpu.VMEM((tile * SUB, V7X_LANES), F32),
            pltpu.SemaphoreType.DMA(()),
            pltpu.SemaphoreType.DMA(()),
        ],
        compiler_params=_cparams(1),
        name="dispatch",
    )(slots_flat, pad_start, pad_len, used_tiles, hn)


def _grouped_kernel(exp_ref, used_ref, x_ref, w1_ref, w3_ref, w2_ref, o_ref, xb_ref, acc_ref):
    j = pl.program_id(1)
    last = pl.num_programs(1) - 1
    valid = pl.program_id(0) < used_ref[0]

    tile = acc_ref.shape[0]

    @pl.when(valid & (j == 0))
    def _():
        for c in range(SUB):
            xb_ref[:, c * V7X_LANES:(c + 1) * V7X_LANES] = _load_row_tiles(x_ref, tile, c).astype(BF16)
        acc_ref[...] = jnp.zeros_like(acc_ref)

    @pl.when(valid)
    def _():
        xb = xb_ref[...]
        h = _silu(_dot(xb, w1_ref[...])) * _dot(xb, w3_ref[...])
        acc_ref[...] += _dot(h.astype(BF16), w2_ref[...])

    @pl.when(valid & (j == last))
    def _():
        _store_row_tiles(o_ref, acc_ref[...])

    @pl.when(jnp.logical_not(valid) & (j == last))
    def _():
        o_ref[...] = jnp.zeros_like(o_ref)


def _grouped(xs, w1, w3, w2, tile_exp, used_tiles, tile, tf=512):
    d = w1.shape[1]
    n_tiles = xs.shape[0] // (tile * SUB)
    nj = w1.shape[2] // tf

    def jsel(i, j, used):
        return jnp.where(i < used[0], j, nj - 1)

    grid_spec = pltpu.PrefetchScalarGridSpec(
        num_scalar_prefetch=2,
        grid=(n_tiles, nj),
        in_specs=[
            pl.BlockSpec((tile * SUB, V7X_LANES), lambda i, j, ex, us: (i, 0)),
            pl.BlockSpec((None, d, tf), lambda i, j, ex, us: (ex[i], 0, jsel(i, j, us))),
            pl.BlockSpec((None, d, tf), lambda i, j, ex, us: (ex[i], 0, jsel(i, j, us))),
            pl.BlockSpec((None, tf, d), lambda i, j, ex, us: (ex[i], jsel(i, j, us), 0)),
        ],
        out_specs=pl.BlockSpec((tile * SUB, V7X_LANES), lambda i, j, ex, us: (i, 0)),
        scratch_shapes=[pltpu.VMEM((tile, d), BF16), pltpu.VMEM((tile, d), F32)],
    )
    return pl.pallas_call(
        _grouped_kernel,
        grid_spec=grid_spec,
        out_shape=jax.ShapeDtypeStruct(xs.shape, F32),
        compiler_params=_cparams(2),
        name="grouped_swiglu",
    )(tile_exp, used_tiles, xs, w1.astype(BF16), w3.astype(BF16), w2.astype(BF16))


def _combine_kernel(slot_ref, h_ref, gate_ref, g_ref, ys_ref, o_ref, buf_ref, sem):
    tm, d = h_ref.shape

    def row_copy(r, k):
        return pltpu.make_async_copy(
            _row_tile(ys_ref, slot_ref[TOP_K * r + k]), _row_tile(buf_ref.at[k], r), sem)

    _start_row_copies(tm, row_copy)
    for k in range(TOP_K):
        pltpu.make_async_copy(ys_ref.at[pl.ds(0, tm * SUB)], buf_ref.at[k], sem).wait()

    gate = gate_ref[...]
    ss = jnp.zeros((tm, 1), F32)
    for c in range(SUB):
        lanes = slice(c * V7X_LANES, (c + 1) * V7X_LANES)
        piece = (h_ref[:, lanes] + gate[:, 0:1] * _load_row_tiles(buf_ref.at[0], tm, c)
                 + gate[:, 1:2] * _load_row_tiles(buf_ref.at[1], tm, c))
        o_ref[:, lanes] = piece
        ss = ss + jnp.sum(piece * piece, axis=-1, keepdims=True)
    o_ref[...] = o_ref[...] * lax.rsqrt(ss / d + EPS) * g_ref[...]


def _combine(h2, gates, slots_flat, ys, g, tm=512):
    n, d = h2.shape
    tm = min(tm, n)
    return pl.pallas_call(
        _combine_kernel,
        grid=(n // tm,),
        in_specs=[
            pl.BlockSpec((TOP_K * tm,), lambda i: (i,), memory_space=pltpu.SMEM),
            pl.BlockSpec((tm, d), lambda i: (i, 0)),
            pl.BlockSpec((tm, TOP_K), lambda i: (i, 0)),
            _const_spec((1, d)),
            pl.BlockSpec(memory_space=pl.ANY),
        ],
        out_specs=pl.BlockSpec((tm, d), lambda i: (i, 0)),
        out_shape=jax.ShapeDtypeStruct((n, d), F32),
        scratch_shapes=[pltpu.VMEM((TOP_K, tm * SUB, V7X_LANES), F32),
                        pltpu.SemaphoreType.DMA(())],
        compiler_params=_cparams(1),
        name="combine",
    )(slots_flat, h2, gates, g.reshape(1, d), ys)


def _moe(h2, g_ffn, w_router, w1, w3, w2, g_final, tile=512):
    n, d = h2.shape
    n_experts = w_router.shape[1]
    tile = min(tile, n)
    n_tiles = TOP_K * n // tile + n_experts
    hn, expert, rank, gates, counts = _router(h2, g_ffn, w_router)
    counts = counts[0, :n_experts]

    tiles_per = (counts + tile - 1) // tile
    ends = jnp.cumsum(tiles_per)
    starts = ends - tiles_per
    used = ends[-1:]
    onehot = expert[:, :, None] == jnp.arange(n_experts, dtype=I32)
    slots = rank + jnp.sum(jnp.where(onehot, starts * tile, 0), axis=-1)
    slots_flat = slots.reshape(-1).astype(I32)
    pad_start = (starts * tile + counts).astype(I32)
    pad_len = (tiles_per * tile - counts).astype(I32)
    tidx = jnp.minimum(jnp.arange(n_tiles, dtype=I32), used - 1)
    tile_exp = jnp.sum((tidx[:, None] >= ends[None, :]).astype(I32), axis=1)

    xs = _dispatch(hn, slots_flat, pad_start, pad_len, used.astype(I32), n_tiles, tile)
    ys = _grouped(xs, w1, w3, w2, tile_exp.astype(I32), used.astype(I32), tile)
    return _combine(h2, gates, slots_flat, ys, g_final)


def kernel(x, ev_norm_mix, ev_w_in, ev_conv_w, hgrn_lower_bounds, ev_gnorm, ev_w_out,
           ev_norm_ffn, ev_ffn_w1, ev_ffn_w3, ev_ffn_w2,
           od_norm_mix, od_w_pw1, od_b_pw1, od_w_dw, od_b_dw, od_ln_g, od_ln_b,
           od_w_pw2, od_b_pw2, od_norm_ffn, od_router, od_moe_w1, od_moe_w3, od_moe_w2,
           final_norm):
    bsz, seq, d = x.shape
    assert ev_w_in.shape[0] == 1 and od_w_pw1.shape[0] == 1
    h = _mixer0(x, ev_norm_mix[0], ev_w_in[0], ev_conv_w[0], hgrn_lower_bounds, ev_gnorm[0],
                ev_w_out[0], layer=0)
    h = _ffn(h.reshape(bsz * seq, d), ev_norm_ffn[0], ev_ffn_w1[0], ev_ffn_w3[0], ev_ffn_w2[0])
    h = _conformer(h.reshape(bsz, seq, d), od_norm_mix[0], od_w_pw1[0], od_b_pw1[0], od_w_dw[0],
                   od_b_dw[0], od_ln_g[0], od_ln_b[0], od_w_pw2[0], od_b_pw2[0])
    out = _moe(h.reshape(bsz * seq, d), od_norm_ffn[0], od_router[0], od_moe_w1[0], od_moe_w3[0],
               od_moe_w2[0], final_norm)
    return out.reshape(bsz, seq, d)
```

```python
import functools

import jax
import jax.numpy as jnp
from jax import lax
from jax.experimental import pallas as pl
from jax.experimental.pallas import tpu as pltpu

F32 = jnp.float32
BF16 = jnp.bfloat16
I32 = jnp.int32

EPS = 1e-6
CHUNK = 64
SUB = 8
HEAD_DIM = 128
TOP_K = 2

V7X_LANES = 128
V7X_MXU_DIM = 256
V7X_VMEM_BYTES = 64 * 1024 * 1024
VMEM_LIMIT = V7X_VMEM_BYTES - 8 * 1024 * 1024


def _cparams(n_axes):
    return pltpu.CompilerParams(
        dimension_semantics=("arbitrary",) * n_axes, vmem_limit_bytes=VMEM_LIMIT)


def _const_spec(shape):
    nd = len(shape)
    return pl.BlockSpec(shape, lambda *_: (0,) * nd, pipeline_mode=pl.Buffered(1))


def _rms(x, g):
    return x * lax.rsqrt(jnp.mean(x * x, axis=-1, keepdims=True) + EPS) * g


def _silu(x):
    return x * jax.nn.sigmoid(x)


def _dot(a, b):
    return jnp.dot(a, b, preferred_element_type=F32)


def _dot_nt(a, b):
    return lax.dot_general(a, b, (((1,), (1,)), ((), ())), preferred_element_type=F32)


def _store_row_tiles(ref, val):
    m = val.shape[0]
    for c in range(SUB):
        ref[pl.ds(c, m, stride=SUB), :] = val[:, c * V7X_LANES:(c + 1) * V7X_LANES]


def _load_row_tiles(ref, m, c):
    return ref[pl.ds(c, m, stride=SUB), :]


def _row_tile(ref, row):
    return ref.at[pl.ds(pl.multiple_of(row * SUB, SUB), SUB)]


ROW_COPY_UNROLL = 8


def _start_row_copies(n_rows, row_copy):
    def body(i, carry):
        for u in range(ROW_COPY_UNROLL):
            for k in range(TOP_K):
                row_copy(i * ROW_COPY_UNROLL + u, k).start(priority=(u * TOP_K + k) % 2)
        return carry

    lax.fori_loop(0, n_rows // ROW_COPY_UNROLL, body, 0)


def _mixer0_kernel(x_ref, gmix_ref, win_ref, convw_ref, lbp_ref, gn_ref, wout_ref, o_ref,
                   proj_ref, zbuf_ref, st_ref, y_ref, w_ref, *, layer, conv_dim, width):
    tt = x_ref.shape[1]
    heads = width // HEAD_DIM
    c3 = 3 * conv_dim

    @pl.when(pl.program_id(1) == 0)
    def _():
        st_ref[...] = jnp.zeros_like(st_ref)
        zbuf_ref[0:SUB, :] = jnp.zeros((SUB, conv_dim), F32)

    x = x_ref[0]
    hn = _rms(x, gmix_ref[...])
    proj_ref[...] = _dot(hn.astype(BF16), win_ref[...])

    z = proj_ref[:, 2 * conv_dim:c3] * proj_ref[:, 0:conv_dim]
    zbuf_ref[SUB:SUB + tt, :] = z
    cw = convw_ref[...]
    conv = (cw[2:3, :] * z + cw[1:2, :] * zbuf_ref[SUB - 1:SUB - 1 + tt, :]
            + cw[0:1, :] * zbuf_ref[SUB - 2:SUB - 2 + tt, :])
    y_ref[:, 0:conv_dim] = (proj_ref[:, conv_dim:2 * conv_dim] * conv).astype(BF16)
    zbuf_ref[0:SUB, :] = zbuf_ref[tt:tt + SUB, :]

    lbp = lbp_ref[...]
    lbe = jnp.exp(lbp - jnp.max(lbp, axis=0, keepdims=True))
    lb = jnp.sum(lbe[0:layer + 1, :], axis=0, keepdims=True) / jnp.sum(lbe, axis=0, keepdims=True)

    row = lax.broadcasted_iota(I32, (CHUNK, width), 0)
    row8 = lax.broadcasted_iota(I32, (SUB, width), 0)
    ti = lax.broadcasted_iota(I32, (CHUNK, CHUNK), 0)
    si = lax.broadcasted_iota(I32, (CHUNK, CHUNK), 1)
    tri = (si <= ti).astype(F32)
    bdr = lax.broadcasted_iota(I32, (V7X_MXU_DIM, V7X_MXU_DIM), 0) // HEAD_DIM
    bdc = lax.broadcasted_iota(I32, (V7X_MXU_DIM, V7X_MXU_DIM), 1) // HEAD_DIM
    head_ones = (bdr == bdc).astype(BF16)
    gn = gn_ref[...]

    def chunk_body(c, carry):
        r0 = pl.multiple_of(c * CHUNK, CHUNK)
        rows = pl.ds(r0, CHUNK)
        q = proj_ref[rows, c3:c3 + width]
        f = proj_ref[rows, c3 + width:c3 + 2 * width]
        v = proj_ref[rows, c3 + 2 * width:c3 + 3 * width]
        og = proj_ref[rows, c3 + 3 * width:c3 + 4 * width]
        fg = lb + (1.0 - lb) * jax.nn.sigmoid(f)
        k = 1.0 - fg
        b = jnp.dot(tri, jnp.log(fg), preferred_element_type=F32, precision=lax.Precision.HIGHEST)
        b_last = b[CHUNK - 1:CHUNK, :]
        v16 = v.astype(BF16)

        q_s = (q * jnp.exp(b)).astype(BF16)
        k_s = (k * jnp.exp(b_last - b)).astype(BF16)
        e_last = jnp.exp(b_last)

        p_acc = [jnp.zeros((CHUNK, CHUNK), F32) for _ in range(heads)]
        half = CHUNK // 2
        while half >= SUB:
            blk = 2 * half
            anc = jnp.concatenate(
                [jnp.broadcast_to(b[i * blk + half:i * blk + half + 1, :], (blk, width))
                 for i in range(CHUNK // blk)], axis=0)
            upper = (row % blk) >= half
            q_l = (q * jnp.where(upper, jnp.exp(b - anc), 0.0)).astype(BF16)
            k_l = (k * jnp.where(upper, 0.0, jnp.exp(anc - b))).astype(BF16)
            same = (ti // blk) == (si // blk)
            for h in range(heads):
                hs = slice(h * HEAD_DIM, (h + 1) * HEAD_DIM)
                p_acc[h] = p_acc[h] + jnp.where(same, _dot_nt(q_l[:, hs], k_l[:, hs]), 0.0)
            half //= 2

        o_diag = []
        for rb in range(CHUNK // SUB):
            rs = slice(rb * SUB, (rb + 1) * SUB)
            b8, q8, k8 = b[rs, :], q[rs, :], k[rs, :]
            for s in range(0, SUB, 2):
                pair = []
                for s1 in (s, s + 1):
                    dec = jnp.where(row8 >= s1, jnp.exp(b8 - b8[s1:s1 + 1, :]), 0.0)
                    pair.append(q8 * dec * k8[s1:s1 + 1, :])
                w_ref[(rb * SUB + s) * SUB:(rb * SUB + s + 2) * SUB, :] = (
                    jnp.concatenate(pair, axis=0).astype(BF16))
        for rb in range(CHUNK // SUB):
            wrows = slice(rb * SUB * SUB, (rb + 1) * SUB * SUB)
            sc = jnp.concatenate(
                [_dot(w_ref[wrows, j * V7X_MXU_DIM:(j + 1) * V7X_MXU_DIM], head_ones)
                 for j in range(width // V7X_MXU_DIM)], axis=1)
            v8 = v[rb * SUB:(rb + 1) * SUB, :]
            acc = sc[0:SUB, :] * v8[0:1, :]
            for s in range(1, SUB):
                acc = acc + sc[s * SUB:(s + 1) * SUB, :] * v8[s:s + 1, :]
            o_diag.append(acc)
        o_diag = jnp.concatenate(o_diag, axis=0)

        outs = []
        for h in range(heads):
            hs = slice(h * HEAD_DIM, (h + 1) * HEAD_DIM)
            st = st_ref[h]
            o_h = (_dot_nt(q_s[:, hs], st.astype(BF16)) + _dot(p_acc[h].astype(BF16), v16[:, hs])
                   + o_diag[:, hs])
            st_ref[h] = e_last[:, hs] * st + _dot(v16[:, hs].T, k_s[:, hs])
            o_n = o_h * lax.rsqrt(jnp.mean(o_h * o_h, axis=-1, keepdims=True) + EPS)
            outs.append(o_n)
        o = jnp.concatenate(outs, axis=1) * gn * _silu(og)
        y_ref[rows, conv_dim:conv_dim + width] = o.astype(BF16)
        return carry

    lax.fori_loop(0, tt // CHUNK, chunk_body, 0, unroll=2)
    o_ref[0] = x + _dot(y_ref[...], wout_ref[...])


def _mixer0(x, gmix, w_in, conv_w, lbp, gnorm, w_out, layer, tt=256):
    bsz, seq, d = x.shape
    conv_dim = conv_w.shape[1]
    width = lbp.shape[1]
    n_in = w_in.shape[1]
    tt = min(tt, seq)
    heads = width // HEAD_DIM
    gn = jnp.tile(gnorm.reshape(1, HEAD_DIM), (1, heads))
    kern = functools.partial(_mixer0_kernel, layer=layer, conv_dim=conv_dim, width=width)
    return pl.pallas_call(
        kern,
        grid=(bsz, seq // tt),
        in_specs=[
            pl.BlockSpec((1, tt, d), lambda b, t: (b, t, 0)),
            _const_spec((1, d)),
            _const_spec((d, n_in)),
            _const_spec(conv_w.shape),
            _const_spec(lbp.shape),
            _const_spec((1, width)),
            _const_spec(w_out.shape),
        ],
        out_specs=pl.BlockSpec((1, tt, d), lambda b, t: (b, t, 0)),
        out_shape=jax.ShapeDtypeStruct(x.shape, F32),
        scratch_shapes=[
            pltpu.VMEM((tt, n_in), F32),
            pltpu.VMEM((tt + SUB, conv_dim), F32),
            pltpu.VMEM((heads, HEAD_DIM, HEAD_DIM), F32),
            pltpu.VMEM((tt, conv_dim + width), BF16),
            pltpu.VMEM((CHUNK * SUB, width), BF16),
        ],
        compiler_params=_cparams(2),
        name="mixer0",
    )(x, gmix.reshape(1, d), w_in.astype(BF16), conv_w, lbp, gn, w_out.astype(BF16))


def _ffn_kernel(x_ref, g_ref, w1_ref, w3_ref, w2_ref, o_ref, *, tf):
    x = x_ref[...]
    hn = _rms(x, g_ref[...]).astype(BF16)
    acc = x
    for j in range(w1_ref.shape[1] // tf):
        cols = slice(j * tf, (j + 1) * tf)
        h = _silu(_dot(hn, w1_ref[:, cols])) * _dot(hn, w3_ref[:, cols])
        acc = acc + _dot(h.astype(BF16), w2_ref[cols, :])
    o_ref[...] = acc


def _ffn(x2, g, w1, w3, w2, tm=512, tf=256):
    n, d = x2.shape
    tm = min(tm, n)
    return pl.pallas_call(
        functools.partial(_ffn_kernel, tf=tf),
        grid=(n // tm,),
        in_specs=[
            pl.BlockSpec((tm, d), lambda i: (i, 0)),
            _const_spec((1, d)),
            _const_spec(w1.shape),
            _const_spec(w3.shape),
            _const_spec(w2.shape),
        ],
        out_specs=pl.BlockSpec((tm, d), lambda i: (i, 0)),
        out_shape=jax.ShapeDtypeStruct((n, d), F32),
        compiler_params=_cparams(1),
        name="ffn0",
    )(x2, g.reshape(1, d), w1.astype(BF16), w3.astype(BF16), w2.astype(BF16))


def _conformer_kernel(x_ref, g_ref, w1_ref, b1_ref, wdw_ref, bdw_ref, lng_ref, lnb_ref,
                      w2_ref, b2_ref, o_ref, ubuf_ref, conv_ref, *, halo):
    tt = x_ref.shape[1]
    d = x_ref.shape[2]
    kw = wdw_ref.shape[0]

    @pl.when(pl.program_id(1) == 0)
    def _():
        ubuf_ref[0:halo, :] = jnp.zeros((halo, d), F32)
        ubuf_ref[halo + tt:halo + tt + SUB, :] = jnp.zeros((SUB, d), F32)

    x = x_ref[0]
    hn = _rms(x, g_ref[...]).astype(BF16)
    p = _dot(hn, w1_ref[...]) + b1_ref[...]
    ubuf_ref[halo:halo + tt, :] = p[:, 0:d] * jax.nn.sigmoid(p[:, d:2 * d])
    off0 = halo - (kw - 1)
    seg = min(tt, 128)
    for c in range(d // V7X_LANES):
        lanes = slice(c * V7X_LANES, (c + 1) * V7X_LANES)
        wcol = wdw_ref[:, lanes]
        for base in range(0, tt, seg):
            acc = None
            for r in range(SUB):
                part = None
                for a in range((off0 + kw - 1) // SUB + 1):
                    k = SUB * a + r - off0
                    if 0 <= k < kw:
                        win = ubuf_ref[base + SUB * a:base + SUB * a + seg + SUB, lanes]
                        term = wcol[k:k + 1, :] * win
                        part = term if part is None else part + term
                if part is not None:
                    shifted = part[r:r + seg, :]
                    acc = shifted if acc is None else acc + shifted
            conv_ref[base:base + seg, lanes] = acc + bdw_ref[:, lanes]
    acc = conv_ref[...]
    ubuf_ref[0:halo, :] = ubuf_ref[tt:tt + halo, :]
    mu = jnp.mean(acc, axis=-1, keepdims=True)
    xc = acc - mu
    var = jnp.mean(xc * xc, axis=-1, keepdims=True)
    u = _silu(xc * lax.rsqrt(var + EPS) * lng_ref[...] + lnb_ref[...])
    o_ref[0] = x + _dot(u.astype(BF16), w2_ref[...]) + b2_ref[...]


def _conformer(x, g, w_pw1, b_pw1, w_dw, b_dw, ln_g, ln_b, w_pw2, b_pw2, tt=256):
    bsz, seq, d = x.shape
    tt = min(tt, seq)
    kw = w_dw.shape[0]
    halo = -(-(kw - 1) // SUB) * SUB
    r = lambda a: a.reshape(1, -1)
    return pl.pallas_call(
        functools.partial(_conformer_kernel, halo=halo),
        grid=(bsz, seq // tt),
        in_specs=[
            pl.BlockSpec((1, tt, d), lambda b, t: (b, t, 0)),
            _const_spec((1, d)),
            _const_spec(w_pw1.shape),
            _const_spec((1, 2 * d)),
            _const_spec(w_dw.shape),
            _const_spec((1, d)),
            _const_spec((1, d)),
            _const_spec((1, d)),
            _const_spec(w_pw2.shape),
            _const_spec((1, d)),
        ],
        out_specs=pl.BlockSpec((1, tt, d), lambda b, t: (b, t, 0)),
        out_shape=jax.ShapeDtypeStruct(x.shape, F32),
        scratch_shapes=[pltpu.VMEM((halo + tt + SUB, d), F32), pltpu.VMEM((tt, d), F32)],
        compiler_params=_cparams(2),
        name="conformer",
    )(x, r(g), w_pw1.astype(BF16), r(b_pw1), w_dw, r(b_dw), r(ln_g), r(ln_b),
      w_pw2.astype(BF16), r(b_pw2))


def _router_kernel(x_ref, g_ref, wr_ref, hn_ref, exp_ref, rank_ref, gate_ref, cnt_ref, run_ref,
                   *, n_experts):
    tm = x_ref.shape[0]

    @pl.when(pl.program_id(0) == 0)
    def _():
        run_ref[...] = jnp.zeros_like(run_ref)

    hn = _rms(x_ref[...], g_ref[...])
    _store_row_tiles(hn_ref, hn)
    logits = jnp.dot(hn, wr_ref[...], preferred_element_type=F32, precision=lax.Precision.HIGHEST)
    lane = lax.broadcasted_iota(I32, logits.shape, 1)
    neg = jnp.float32(-jnp.inf)
    big = jnp.int32(V7X_LANES)
    logits = jnp.where(lane < n_experts, logits, neg)
    m1 = jnp.max(logits, axis=-1, keepdims=True)
    i1 = jnp.min(jnp.where(logits == m1, lane, big), axis=-1, keepdims=True)
    rest = jnp.where(lane == i1, neg, logits)
    m2 = jnp.max(rest, axis=-1, keepdims=True)
    i2 = jnp.min(jnp.where(rest == m2, lane, big), axis=-1, keepdims=True)
    e2 = jnp.exp(m2 - m1)
    g1 = 1.0 / (1.0 + e2)
    g2 = e2 / (1.0 + e2)

    sel1 = lane == i1
    sel2 = lane == i2
    onehot = jnp.where(sel1 | sel2, 1.0, 0.0)
    ti = lax.broadcasted_iota(I32, (tm, tm), 0)
    si = lax.broadcasted_iota(I32, (tm, tm), 1)
    before = _dot((si < ti).astype(BF16), onehot.astype(BF16)) + run_ref[...]
    r1 = jnp.sum(jnp.where(sel1, before, 0.0), axis=-1, keepdims=True).astype(I32)
    r2 = jnp.sum(jnp.where(sel2, before, 0.0), axis=-1, keepdims=True).astype(I32)
    run_ref[...] = run_ref[...] + jnp.sum(onehot, axis=0, keepdims=True)

    col = lax.broadcasted_iota(I32, (tm, TOP_K), 1)
    exp_ref[...] = jnp.where(col == 0, i1, i2)
    rank_ref[...] = jnp.where(col == 0, r1, r2)
    gate_ref[...] = jnp.where(col == 0, g1, g2)
    cnt_ref[...] = run_ref[...].astype(I32)


def _router(x2, g, w_router, tm=512):
    n, d = x2.shape
    assert d == SUB * V7X_LANES
    tm = min(tm, n)
    n_experts = w_router.shape[1]
    wr = jnp.zeros((d, V7X_LANES), F32).at[:, :n_experts].set(w_router)
    per_token = lambda dt: jax.ShapeDtypeStruct((n, TOP_K), dt)
    return pl.pallas_call(
        functools.partial(_router_kernel, n_experts=n_experts),
        grid=(n // tm,),
        in_specs=[
            pl.BlockSpec((tm, d), lambda i: (i, 0)),
            _const_spec((1, d)),
            _const_spec((d, V7X_LANES)),
        ],
        out_specs=[
            pl.BlockSpec((tm * SUB, V7X_LANES), lambda i: (i, 0)),
            pl.BlockSpec((tm, TOP_K), lambda i: (i, 0)),
            pl.BlockSpec((tm, TOP_K), lambda i: (i, 0)),
            pl.BlockSpec((tm, TOP_K), lambda i: (i, 0)),
            pl.BlockSpec((1, V7X_LANES), lambda i: (0, 0)),
        ],
        out_shape=[
            jax.ShapeDtypeStruct((n * SUB, V7X_LANES), F32),
            per_token(I32), per_token(I32), per_token(F32),
            jax.ShapeDtypeStruct((1, V7X_LANES), I32),
        ],
        scratch_shapes=[pltpu.VMEM((1, V7X_LANES), F32)],
        compiler_params=_cparams(1),
        name="router",
    )(x2, g.reshape(1, d), wr)


def _dispatch_kernel(slot_ref, pstart_ref, plen_ref, used_ref, hn_ref, xs_ref, zero_ref, sem, zsem,
                     *, n_experts, tile):
    tm = hn_ref.shape[0] // SUB
    n_tiles = xs_ref.shape[0] // (tile * SUB)

    def row_copy(r, k):
        return pltpu.make_async_copy(
            _row_tile(hn_ref, r), _row_tile(xs_ref, slot_ref[TOP_K * r + k]), sem)

    _start_row_copies(tm, row_copy)
    for k in range(TOP_K):
        pltpu.make_async_copy(hn_ref, xs_ref.at[pl.ds(0, tm * SUB)], sem).wait()

    def rows(ref, start, count):
        return ref.at[pl.ds(pl.multiple_of(start * SUB, SUB), count * SUB)]

    def for_each_pad_copy(act):
        for e in range(n_experts):
            cur = pstart_ref[e]
            length = plen_ref[e]
            p = 1
            while p < tile:
                take = (length & p) != 0

                @pl.when(take)
                def _():
                    act(pltpu.make_async_copy(rows(zero_ref, 0, p), rows(xs_ref, cur, p), zsem))
                cur = cur + jnp.where(take, p, 0)
                p *= 2
        for i in range(n_experts):
            t = used_ref[0] + i

            @pl.when(t < n_tiles)
            def _():
                act(pltpu.make_async_copy(zero_ref, rows(xs_ref, t * tile, tile), zsem))

    @pl.when(pl.program_id(0) == pl.num_programs(0) - 1)
    def _():
        zero_ref[...] = jnp.zeros_like(zero_ref)
        for_each_pad_copy(lambda cp: cp.start())
        for_each_pad_copy(lambda cp: cp.wait())


def _dispatch(hn, slots_flat, pad_start, pad_len, used_tiles, n_tiles, tile, tm=512):
    n = hn.shape[0] // SUB
    tm = min(tm, n)
    n_experts = pad_start.shape[0]
    smem = pl.BlockSpec(memory_space=pltpu.SMEM)
    return pl.pallas_call(
        functools.partial(_dispatch_kernel, n_experts=n_experts, tile=tile),
        grid=(n // tm,),
        in_specs=[
            pl.BlockSpec((TOP_K * tm,), lambda i: (i,), memory_space=pltpu.SMEM),
            smem, smem, smem,
            pl.BlockSpec((tm * SUB, V7X_LANES), lambda i: (i, 0)),
        ],
        out_specs=pl.BlockSpec(memory_space=pl.ANY),
        out_shape=jax.ShapeDtypeStruct((n_tiles * tile * SUB, V7X_LANES), F32),
        scratch_shapes=[
            pltpu.VMEM((tile * SUB, V7X_LANES), F32),
            pltpu.SemaphoreType.DMA(()),
            pltpu.SemaphoreType.DMA(()),
        ],
        compiler_params=_cparams(1),
        name="dispatch",
    )(slots_flat, pad_start, pad_len, used_tiles, hn)


def _grouped_kernel(exp_ref, used_ref, nsub_ref, x_ref, w1_ref, w3_ref, w2_ref, o_ref,
                    xb_ref, acc_ref, wb1_ref, wb3_ref, wb2_ref, *, sub):
    i = pl.program_id(0)
    j = pl.program_id(1)
    last = pl.num_programs(1) - 1
    valid = i < used_ref[0]
    tile = acc_ref.shape[0]

    @pl.when(valid & (j == 0))
    def _():
        for c in range(SUB):
            xb_ref[:, c * V7X_LANES:(c + 1) * V7X_LANES] = _load_row_tiles(x_ref, tile, c).astype(BF16)
        acc_ref[...] = jnp.zeros_like(acc_ref)

    @pl.when(valid)
    def _():
        wb1_ref[...] = w1_ref[...].astype(BF16)
        wb3_ref[...] = w3_ref[...].astype(BF16)
        wb2_ref[...] = w2_ref[...].astype(BF16)
        for s in range(tile // sub):
            @pl.when(s < nsub_ref[i])
            def _():
                rows = slice(s * sub, (s + 1) * sub)
                xb = xb_ref[rows, :]
                h = _silu(_dot(xb, wb1_ref[...])) * _dot(xb, wb3_ref[...])
                acc_ref[rows, :] += _dot(h.astype(BF16), wb2_ref[...])

    @pl.when(valid & (j == last))
    def _():
        _store_row_tiles(o_ref, acc_ref[...])

    @pl.when(jnp.logical_not(valid) & (j == last))
    def _():
        o_ref[...] = jnp.zeros_like(o_ref)


def _grouped(xs, w1, w3, w2, tile_exp, used_tiles, tile_nsub, tile, sub, tf=512):
    d = w1.shape[1]
    n_tiles = xs.shape[0] // (tile * SUB)
    nj = w1.shape[2] // tf

    def jsel(i, j, used):
        return jnp.where(i < used[0], j, nj - 1)

    grid_spec = pltpu.PrefetchScalarGridSpec(
        num_scalar_prefetch=3,
        grid=(n_tiles, nj),
        in_specs=[
            pl.BlockSpec((tile * SUB, V7X_LANES), lambda i, j, ex, us, ns: (i, 0)),
            pl.BlockSpec((None, d, tf), lambda i, j, ex, us, ns: (ex[i], 0, jsel(i, j, us))),
            pl.BlockSpec((None, d, tf), lambda i, j, ex, us, ns: (ex[i], 0, jsel(i, j, us))),
            pl.BlockSpec((None, tf, d), lambda i, j, ex, us, ns: (ex[i], jsel(i, j, us), 0)),
        ],
        out_specs=pl.BlockSpec((tile * SUB, V7X_LANES), lambda i, j, ex, us, ns: (i, 0)),
        scratch_shapes=[
            pltpu.VMEM((tile, d), BF16), pltpu.VMEM((tile, d), F32),
            pltpu.VMEM((d, tf), BF16), pltpu.VMEM((d, tf), BF16), pltpu.VMEM((tf, d), BF16),
        ],
    )
    return pl.pallas_call(
        functools.partial(_grouped_kernel, sub=sub),
        grid_spec=grid_spec,
        out_shape=jax.ShapeDtypeStruct(xs.shape, F32),
        compiler_params=_cparams(2),
        name="grouped_swiglu",
    )(tile_exp, used_tiles, tile_nsub, xs, w1, w3, w2)


def _combine_kernel(slot_ref, h_ref, gate_ref, g_ref, ys_ref, o_ref, buf_ref, sem):
    tm, d = h_ref.shape

    def row_copy(r, k):
        return pltpu.make_async_copy(
            _row_tile(ys_ref, slot_ref[TOP_K * r + k]), _row_tile(buf_ref.at[k], r), sem)

    _start_row_copies(tm, row_copy)
    for k in range(TOP_K):
        pltpu.make_async_copy(ys_ref.at[pl.ds(0, tm * SUB)], buf_ref.at[k], sem).wait()

    gate = gate_ref[...]
    ss = jnp.zeros((tm, 1), F32)
    for c in range(SUB):
        lanes = slice(c * V7X_LANES, (c + 1) * V7X_LANES)
        piece = (h_ref[:, lanes] + gate[:, 0:1] * _load_row_tiles(buf_ref.at[0], tm, c)
                 + gate[:, 1:2] * _load_row_tiles(buf_ref.at[1], tm, c))
        o_ref[:, lanes] = piece
        ss = ss + jnp.sum(piece * piece, axis=-1, keepdims=True)
    o_ref[...] = o_ref[...] * lax.rsqrt(ss / d + EPS) * g_ref[...]


def _combine(h2, gates, slots_flat, ys, g, tm=512):
    n, d = h2.shape
    tm = min(tm, n)
    return pl.pallas_call(
        _combine_kernel,
        grid=(n // tm,),
        in_specs=[
            pl.BlockSpec((TOP_K * tm,), lambda i: (i,), memory_space=pltpu.SMEM),
            pl.BlockSpec((tm, d), lambda i: (i, 0)),
            pl.BlockSpec((tm, TOP_K), lambda i: (i, 0)),
            _const_spec((1, d)),
            pl.BlockSpec(memory_space=pl.ANY),
        ],
        out_specs=pl.BlockSpec((tm, d), lambda i: (i, 0)),
        out_shape=jax.ShapeDtypeStruct((n, d), F32),
        scratch_shapes=[pltpu.VMEM((TOP_K, tm * SUB, V7X_LANES), F32),
                        pltpu.SemaphoreType.DMA(())],
        compiler_params=_cparams(1),
        name="combine",
    )(slots_flat, h2, gates, g.reshape(1, d), ys)


def _moe(h2, g_ffn, w_router, w1, w3, w2, g_final, tile=1024, sub=512):
    n, d = h2.shape
    n_experts = w_router.shape[1]
    tile = min(tile, n)
    n_tiles = TOP_K * n // tile + n_experts
    hn, expert, rank, gates, counts = _router(h2, g_ffn, w_router)
    counts = counts[0, :n_experts]

    tiles_per = (counts + tile - 1) // tile
    ends = jnp.cumsum(tiles_per)
    starts = ends - tiles_per
    used = ends[-1:]
    onehot = expert[:, :, None] == jnp.arange(n_experts, dtype=I32)
    slots = rank + jnp.sum(jnp.where(onehot, starts * tile, 0), axis=-1)
    slots_flat = slots.reshape(-1).astype(I32)
    pad_start = (starts * tile + counts).astype(I32)
    pad_len = (tiles_per * tile - counts).astype(I32)
    tidx = jnp.minimum(jnp.arange(n_tiles, dtype=I32), used - 1)
    tile_exp = jnp.sum((tidx[:, None] >= ends[None, :]).astype(I32), axis=1)
    sub = min(sub, tile)
    tile_rows = jnp.clip(counts[tile_exp] - (tidx - starts[tile_exp]) * tile, 0, tile)
    tile_nsub = (tile_rows + sub - 1) // sub

    xs = _dispatch(hn, slots_flat, pad_start, pad_len, used.astype(I32), n_tiles, tile)
    ys = _grouped(xs, w1, w3, w2, tile_exp.astype(I32), used.astype(I32), tile_nsub.astype(I32),
                  tile, sub)
    return _combine(h2, gates, slots_flat, ys, g_final)


def kernel(x, ev_norm_mix, ev_w_in, ev_conv_w, hgrn_lower_bounds, ev_gnorm, ev_w_out,
           ev_norm_ffn, ev_ffn_w1, ev_ffn_w3, ev_ffn_w2,
           od_norm_mix, od_w_pw1, od_b_pw1, od_w_dw, od_b_dw, od_ln_g, od_ln_b,
           od_w_pw2, od_b_pw2, od_norm_ffn, od_router, od_moe_w1, od_moe_w3, od_moe_w2,
           final_norm):
    bsz, seq, d = x.shape
    assert ev_w_in.shape[0] == 1 and od_w_pw1.shape[0] == 1
    h = _mixer0(x, ev_norm_mix[0], ev_w_in[0], ev_conv_w[0], hgrn_lower_bounds, ev_gnorm[0],
                ev_w_out[0], layer=0)
    h = _ffn(h.reshape(bsz * seq, d), ev_norm_ffn[0], ev_ffn_w1[0], ev_ffn_w3[0], ev_ffn_w2[0])
    h = _conformer(h.reshape(bsz, seq, d), od_norm_mix[0], od_w_pw1[0], od_b_pw1[0], od_w_dw[0],
                   od_b_dw[0], od_ln_g[0], od_ln_b[0], od_w_pw2[0], od_b_pw2[0])
    out = _moe(h.reshape(bsz * seq, d), od_norm_ffn[0], od_router[0], od_moe_w1[0], od_moe_w3[0],
               od_moe_w2[0], final_norm)
    return out.reshape(bsz, seq, d)
```

```python
import functools

import jax
import jax.numpy as jnp
from jax import lax
from jax.experimental import pallas as pl
from jax.experimental.pallas import tpu as pltpu

F32 = jnp.float32
BF16 = jnp.bfloat16
I32 = jnp.int32

EPS = 1e-6
CHUNK = 64
CHUNK_SLOTS = 2
SUB = 8
HEAD_DIM = 128
TOP_K = 2

V7X_LANES = 128
V7X_MXU_DIM = 256
V7X_VMEM_BYTES = 64 * 1024 * 1024
VMEM_LIMIT = V7X_VMEM_BYTES - 8 * 1024 * 1024


def _cparams(n_axes):
    return pltpu.CompilerParams(
        dimension_semantics=("arbitrary",) * n_axes, vmem_limit_bytes=VMEM_LIMIT)


def _const_spec(shape):
    nd = len(shape)
    return pl.BlockSpec(shape, lambda *_: (0,) * nd, pipeline_mode=pl.Buffered(1))


def _rms(x, g):
    return x * lax.rsqrt(jnp.mean(x * x, axis=-1, keepdims=True) + EPS) * g


def _silu(x):
    return x * jax.nn.sigmoid(x)


def _dot(a, b):
    return jnp.dot(a, b, preferred_element_type=F32)


def _dot_nt(a, b):
    return lax.dot_general(a, b, (((1,), (1,)), ((), ())), preferred_element_type=F32)


def _store_row_tiles(ref, val):
    m = val.shape[0]
    for c in range(SUB):
        ref[pl.ds(c, m, stride=SUB), :] = val[:, c * V7X_LANES:(c + 1) * V7X_LANES]


def _load_row_tiles(ref, m, c):
    return ref[pl.ds(c, m, stride=SUB), :]


def _row_tile(ref, row):
    return ref.at[pl.ds(pl.multiple_of(row * SUB, SUB), SUB)]


ROW_COPY_UNROLL = 8


def _start_row_copies(n_rows, row_copy):
    def body(i, carry):
        for u in range(ROW_COPY_UNROLL):
            for k in range(TOP_K):
                row_copy(i * ROW_COPY_UNROLL + u, k).start(priority=(u * TOP_K + k) % 2)
        return carry

    lax.fori_loop(0, n_rows // ROW_COPY_UNROLL, body, 0)


def _mixer0_kernel(x_ref, gmix_ref, win_ref, convw_ref, lbp_ref, gn_ref, wout_ref, cast_ref,
                   o_ref, cast_o_ref, proj_ref, zbuf_ref, st_ref, y_ref, w_ref, *, layer, conv_dim, width):
    tt = x_ref.shape[1]
    heads = width // HEAD_DIM
    c3 = 3 * conv_dim

    @pl.when(pl.program_id(1) == 0)
    def _():
        st_ref[...] = jnp.zeros_like(st_ref)
        zbuf_ref[0:SUB, :] = jnp.zeros((SUB, conv_dim), F32)

    x = x_ref[0]
    hn = _rms(x, gmix_ref[...])
    proj_ref[...] = _dot(hn.astype(BF16), win_ref[...])

    z = proj_ref[:, 2 * conv_dim:c3] * proj_ref[:, 0:conv_dim]
    zbuf_ref[SUB:SUB + tt, :] = z
    cw = convw_ref[...]
    conv = (cw[2:3, :] * z + cw[1:2, :] * zbuf_ref[SUB - 1:SUB - 1 + tt, :]
            + cw[0:1, :] * zbuf_ref[SUB - 2:SUB - 2 + tt, :])
    y_ref[:, 0:conv_dim] = (proj_ref[:, conv_dim:2 * conv_dim] * conv).astype(BF16)
    zbuf_ref[0:SUB, :] = zbuf_ref[tt:tt + SUB, :]

    lbp = lbp_ref[...]
    lbe = jnp.exp(lbp - jnp.max(lbp, axis=0, keepdims=True))
    lb = jnp.sum(lbe[0:layer + 1, :], axis=0, keepdims=True) / jnp.sum(lbe, axis=0, keepdims=True)

    row = lax.broadcasted_iota(I32, (CHUNK, width), 0)
    row8 = lax.broadcasted_iota(I32, (SUB, width), 0)
    ti = lax.broadcasted_iota(I32, (CHUNK, CHUNK), 0)
    si = lax.broadcasted_iota(I32, (CHUNK, CHUNK), 1)
    tri = (si <= ti).astype(F32)
    bdr = lax.broadcasted_iota(I32, (V7X_MXU_DIM, V7X_MXU_DIM), 0) // HEAD_DIM
    bdc = lax.broadcasted_iota(I32, (V7X_MXU_DIM, V7X_MXU_DIM), 1) // HEAD_DIM
    head_ones = (bdr == bdc).astype(BF16)
    gn = gn_ref[...]

    def bcast_row(val, r, n):
        return jnp.broadcast_to(val[r:r + 1, :], (n, val.shape[1]))

    def chunk(r0, slot):
        rows = pl.ds(r0, CHUNK)
        ws_ref = w_ref.at[slot]
        q = proj_ref[rows, c3:c3 + width]
        f = proj_ref[rows, c3 + width:c3 + 2 * width]
        v = proj_ref[rows, c3 + 2 * width:c3 + 3 * width]
        og = proj_ref[rows, c3 + 3 * width:c3 + 4 * width]
        fg = lb + (1.0 - lb) * jax.nn.sigmoid(f)
        k = 1.0 - fg
        b = jnp.dot(tri, jnp.log(fg), preferred_element_type=F32, precision=lax.Precision.HIGHEST)
        b_last = b[CHUNK - 1:CHUNK, :]
        v16 = v.astype(BF16)

        q_s = (q * jnp.exp(b)).astype(BF16)
        k_s = (k * jnp.exp(b_last - b)).astype(BF16)
        e_last = jnp.exp(b_last)

        p_acc = [jnp.zeros((CHUNK, CHUNK), F32) for _ in range(heads)]
        half = CHUNK // 2
        while half >= SUB:
            blk = 2 * half
            anc = jnp.concatenate(
                [bcast_row(b, i * blk + half, blk) for i in range(CHUNK // blk)], axis=0)
            upper = (row % blk) >= half
            q_l = (q * jnp.where(upper, jnp.exp(b - anc), 0.0)).astype(BF16)
            k_l = (k * jnp.where(upper, 0.0, jnp.exp(anc - b))).astype(BF16)
            same = (ti // blk) == (si // blk)
            for h in range(heads):
                hs = slice(h * HEAD_DIM, (h + 1) * HEAD_DIM)
                p_acc[h] = p_acc[h] + jnp.where(same, _dot_nt(q_l[:, hs], k_l[:, hs]), 0.0)
            half //= 2

        for rb in range(CHUNK // SUB):
            rs = slice(rb * SUB, (rb + 1) * SUB)
            b8, q8, k8 = b[rs, :], q[rs, :], k[rs, :]
            for s in range(0, SUB, 2):
                pair = []
                for s1 in (s, s + 1):
                    dec = jnp.where(row8 >= s1, jnp.exp(b8 - bcast_row(b8, s1, SUB)), 0.0)
                    pair.append(q8 * dec * bcast_row(k8, s1, SUB))
                ws_ref[(rb * SUB + s) * SUB:(rb * SUB + s + 2) * SUB, :] = (
                    jnp.concatenate(pair, axis=0).astype(BF16))
        o_diag = []
        for rb in range(CHUNK // SUB):
            wrows = slice(rb * SUB * SUB, (rb + 1) * SUB * SUB)
            sc = jnp.concatenate(
                [_dot(ws_ref[wrows, j * V7X_MXU_DIM:(j + 1) * V7X_MXU_DIM], head_ones)
                 for j in range(width // V7X_MXU_DIM)], axis=1)
            v8 = v[rb * SUB:(rb + 1) * SUB, :]
            acc = None
            for s in range(SUB):
                term = sc[s * SUB:(s + 1) * SUB, :] * bcast_row(v8, s, SUB)
                acc = term if acc is None else acc + term
            o_diag.append(acc)
        o_diag = jnp.concatenate(o_diag, axis=0)

        outs = []
        for h in range(heads):
            hs = slice(h * HEAD_DIM, (h + 1) * HEAD_DIM)
            st = st_ref[h]
            o_h = (_dot_nt(q_s[:, hs], st.astype(BF16)) + _dot(p_acc[h].astype(BF16), v16[:, hs])
                   + o_diag[:, hs])
            st_ref[h] = e_last[:, hs] * st + _dot(v16[:, hs].T, k_s[:, hs])
            o_n = o_h * lax.rsqrt(jnp.mean(o_h * o_h, axis=-1, keepdims=True) + EPS)
            outs.append(o_n)
        o = jnp.concatenate(outs, axis=1) * gn * _silu(og)
        y_ref[rows, conv_dim:conv_dim + width] = o.astype(BF16)

    def slots_body(i, carry):
        for slot in range(CHUNK_SLOTS):
            chunk(pl.multiple_of((i * CHUNK_SLOTS + slot) * CHUNK, CHUNK), slot)
        return carry

    lax.fori_loop(0, tt // (CHUNK_SLOTS * CHUNK), slots_body, 0)
    o_ref[0] = x + _dot(y_ref[...], wout_ref[...])
    cast_o_ref[...] = cast_ref[...].astype(BF16)


def _cast_job(w2d, n_steps, step_of):
    rows, cols = w2d.shape
    assert rows % n_steps == 0
    spec = pl.BlockSpec((rows // n_steps, cols), lambda *ids: (step_of(*ids), 0))
    return spec, jax.ShapeDtypeStruct(w2d.shape, BF16)


def _mixer0(x, gmix, w_in, conv_w, lbp, gnorm, w_out, layer, cast_w, tt=256):
    bsz, seq, d = x.shape
    conv_dim = conv_w.shape[1]
    width = lbp.shape[1]
    n_in = w_in.shape[1]
    tt = min(tt, seq)
    heads = width // HEAD_DIM
    gn = jnp.tile(gnorm.reshape(1, HEAD_DIM), (1, heads))
    n_t = seq // tt
    cast_spec, cast_shape = _cast_job(cast_w, bsz * n_t, lambda b, t: b * n_t + t)
    kern = functools.partial(_mixer0_kernel, layer=layer, conv_dim=conv_dim, width=width)
    return pl.pallas_call(
        kern,
        grid=(bsz, seq // tt),
        in_specs=[
            pl.BlockSpec((1, tt, d), lambda b, t: (b, t, 0)),
            _const_spec((1, d)),
            _const_spec((d, n_in)),
            _const_spec(conv_w.shape),
            _const_spec(lbp.shape),
            _const_spec((1, width)),
            _const_spec(w_out.shape),
            cast_spec,
        ],
        out_specs=[pl.BlockSpec((1, tt, d), lambda b, t: (b, t, 0)), cast_spec],
        out_shape=[jax.ShapeDtypeStruct(x.shape, F32), cast_shape],
        scratch_shapes=[
            pltpu.VMEM((tt, n_in), F32),
            pltpu.VMEM((tt + SUB, conv_dim), F32),
            pltpu.VMEM((heads, HEAD_DIM, HEAD_DIM), F32),
            pltpu.VMEM((tt, conv_dim + width), BF16),
            pltpu.VMEM((CHUNK_SLOTS, CHUNK * SUB, width), BF16),
        ],
        compiler_params=_cparams(2),
        name="mixer0",
    )(x, gmix.reshape(1, d), w_in.astype(BF16), conv_w, lbp, gn, w_out.astype(BF16), cast_w)


def _ffn_kernel(x_ref, g_ref, w1_ref, w3_ref, w2_ref, cast_ref, o_ref, cast_o_ref, *, tf):
    x = x_ref[...]
    hn = _rms(x, g_ref[...]).astype(BF16)
    acc = x
    for j in range(w1_ref.shape[1] // tf):
        cols = slice(j * tf, (j + 1) * tf)
        h = _silu(_dot(hn, w1_ref[:, cols])) * _dot(hn, w3_ref[:, cols])
        acc = acc + _dot(h.astype(BF16), w2_ref[cols, :])
    o_ref[...] = acc
    cast_o_ref[...] = cast_ref[...].astype(BF16)


def _ffn(x2, g, w1, w3, w2, cast_w, tm=512, tf=256):
    n, d = x2.shape
    tm = min(tm, n)
    cast_spec, cast_shape = _cast_job(cast_w, n // tm, lambda i: i)
    return pl.pallas_call(
        functools.partial(_ffn_kernel, tf=tf),
        grid=(n // tm,),
        in_specs=[
            pl.BlockSpec((tm, d), lambda i: (i, 0)),
            _const_spec((1, d)),
            _const_spec(w1.shape),
            _const_spec(w3.shape),
            _const_spec(w2.shape),
            cast_spec,
        ],
        out_specs=[pl.BlockSpec((tm, d), lambda i: (i, 0)), cast_spec],
        out_shape=[jax.ShapeDtypeStruct((n, d), F32), cast_shape],
        compiler_params=_cparams(1),
        name="ffn0",
    )(x2, g.reshape(1, d), w1.astype(BF16), w3.astype(BF16), w2.astype(BF16), cast_w)


def _conformer_kernel(x_ref, g_ref, w1_ref, b1_ref, wdw_ref, bdw_ref, lng_ref, lnb_ref,
                      w2_ref, b2_ref, cast_ref, o_ref, cast_o_ref, ubuf_ref, conv_ref, *, halo):
    tt = x_ref.shape[1]
    d = x_ref.shape[2]
    kw = wdw_ref.shape[0]

    @pl.when(pl.program_id(1) == 0)
    def _():
        ubuf_ref[0:halo, :] = jnp.zeros((halo, d), F32)
        ubuf_ref[halo + tt:halo + tt + SUB, :] = jnp.zeros((SUB, d), F32)

    x = x_ref[0]
    hn = _rms(x, g_ref[...]).astype(BF16)
    p = _dot(hn, w1_ref[...]) + b1_ref[...]
    ubuf_ref[halo:halo + tt, :] = p[:, 0:d] * jax.nn.sigmoid(p[:, d:2 * d])
    off0 = halo - (kw - 1)
    seg = min(tt, 128)
    for c in range(d // V7X_LANES):
        lanes = slice(c * V7X_LANES, (c + 1) * V7X_LANES)
        wcol = wdw_ref[:, lanes]
        for base in range(0, tt, seg):
            acc = None
            for r in range(SUB):
                part = None
                for a in range((off0 + kw - 1) // SUB + 1):
                    k = SUB * a + r - off0
                    if 0 <= k < kw:
                        win = ubuf_ref[base + SUB * a:base + SUB * a + seg + SUB, lanes]
                        term = wcol[k:k + 1, :] * win
                        part = term if part is None else part + term
                if part is not None:
                    shifted = part[r:r + seg, :]
                    acc = shifted if acc is None else acc + shifted
            conv_ref[base:base + seg, lanes] = acc + bdw_ref[:, lanes]
    acc = conv_ref[...]
    ubuf_ref[0:halo, :] = ubuf_ref[tt:tt + halo, :]
    mu = jnp.mean(acc, axis=-1, keepdims=True)
    xc = acc - mu
    var = jnp.mean(xc * xc, axis=-1, keepdims=True)
    u = _silu(xc * lax.rsqrt(var + EPS) * lng_ref[...] + lnb_ref[...])
    o_ref[0] = x + _dot(u.astype(BF16), w2_ref[...]) + b2_ref[...]
    cast_o_ref[...] = cast_ref[...].astype(BF16)


def _conformer(x, g, w_pw1, b_pw1, w_dw, b_dw, ln_g, ln_b, w_pw2, b_pw2, cast_w, tt=256):
    bsz, seq, d = x.shape
    tt = min(tt, seq)
    kw = w_dw.shape[0]
    halo = -(-(kw - 1) // SUB) * SUB
    r = lambda a: a.reshape(1, -1)
    n_t = seq // tt
    cast_spec, cast_shape = _cast_job(cast_w, bsz * n_t, lambda b, t: b * n_t + t)
    return pl.pallas_call(
        functools.partial(_conformer_kernel, halo=halo),
        grid=(bsz, seq // tt),
        in_specs=[
            pl.BlockSpec((1, tt, d), lambda b, t: (b, t, 0)),
            _const_spec((1, d)),
            _const_spec(w_pw1.shape),
            _const_spec((1, 2 * d)),
            _const_spec(w_dw.shape),
            _const_spec((1, d)),
            _const_spec((1, d)),
            _const_spec((1, d)),
            _const_spec(w_pw2.shape),
            _const_spec((1, d)),
            cast_spec,
        ],
        out_specs=[pl.BlockSpec((1, tt, d), lambda b, t: (b, t, 0)), cast_spec],
        out_shape=[jax.ShapeDtypeStruct(x.shape, F32), cast_shape],
        scratch_shapes=[pltpu.VMEM((halo + tt + SUB, d), F32), pltpu.VMEM((tt, d), F32)],
        compiler_params=_cparams(2),
        name="conformer",
    )(x, r(g), w_pw1.astype(BF16), r(b_pw1), w_dw, r(b_dw), r(ln_g), r(ln_b),
      w_pw2.astype(BF16), r(b_pw2), cast_w)


def _router_kernel(x_ref, g_ref, wr_ref, hn_ref, exp_ref, rank_ref, gate_ref, cnt_ref, run_ref,
                   *, n_experts):
    tm = x_ref.shape[0]

    @pl.when(pl.program_id(0) == 0)
    def _():
        run_ref[...] = jnp.zeros_like(run_ref)

    hn = _rms(x_ref[...], g_ref[...])
    _store_row_tiles(hn_ref, hn)
    logits = jnp.dot(hn, wr_ref[...], preferred_element_type=F32, precision=lax.Precision.HIGHEST)
    lane = lax.broadcasted_iota(I32, logits.shape, 1)
    neg = jnp.float32(-jnp.inf)
    big = jnp.int32(V7X_LANES)
    logits = jnp.where(lane < n_experts, logits, neg)
    m1 = jnp.max(logits, axis=-1, keepdims=True)
    i1 = jnp.min(jnp.where(logits == m1, lane, big), axis=-1, keepdims=True)
    rest = jnp.where(lane == i1, neg, logits)
    m2 = jnp.max(rest, axis=-1, keepdims=True)
    i2 = jnp.min(jnp.where(rest == m2, lane, big), axis=-1, keepdims=True)
    e2 = jnp.exp(m2 - m1)
    g1 = 1.0 / (1.0 + e2)
    g2 = e2 / (1.0 + e2)

    sel1 = lane == i1
    sel2 = lane == i2
    onehot = jnp.where(sel1 | sel2, 1.0, 0.0)
    ti = lax.broadcasted_iota(I32, (tm, tm), 0)
    si = lax.broadcasted_iota(I32, (tm, tm), 1)
    before = _dot((si < ti).astype(BF16), onehot.astype(BF16)) + run_ref[...]
    r1 = jnp.sum(jnp.where(sel1, before, 0.0), axis=-1, keepdims=True).astype(I32)
    r2 = jnp.sum(jnp.where(sel2, before, 0.0), axis=-1, keepdims=True).astype(I32)
    run_ref[...] = run_ref[...] + jnp.sum(onehot, axis=0, keepdims=True)

    col = lax.broadcasted_iota(I32, (tm, TOP_K), 1)
    exp_ref[...] = jnp.where(col == 0, i1, i2)
    rank_ref[...] = jnp.where(col == 0, r1, r2)
    gate_ref[...] = jnp.where(col == 0, g1, g2)
    cnt_ref[...] = run_ref[...].astype(I32)


def _router(x2, g, w_router, tm=512):
    n, d = x2.shape
    assert d == SUB * V7X_LANES
    tm = min(tm, n)
    n_experts = w_router.shape[1]
    wr = jnp.zeros((d, V7X_LANES), F32).at[:, :n_experts].set(w_router)
    per_token = lambda dt: jax.ShapeDtypeStruct((n, TOP_K), dt)
    return pl.pallas_call(
        functools.partial(_router_kernel, n_experts=n_experts),
        grid=(n // tm,),
        in_specs=[
            pl.BlockSpec((tm, d), lambda i: (i, 0)),
            _const_spec((1, d)),
            _const_spec((d, V7X_LANES)),
        ],
        out_specs=[
            pl.BlockSpec((tm * SUB, V7X_LANES), lambda i: (i, 0)),
            pl.BlockSpec((tm, TOP_K), lambda i: (i, 0)),
            pl.BlockSpec((tm, TOP_K), lambda i: (i, 0)),
            pl.BlockSpec((tm, TOP_K), lambda i: (i, 0)),
            pl.BlockSpec((1, V7X_LANES), lambda i: (0, 0)),
        ],
        out_shape=[
            jax.ShapeDtypeStruct((n * SUB, V7X_LANES), F32),
            per_token(I32), per_token(I32), per_token(F32),
            jax.ShapeDtypeStruct((1, V7X_LANES), I32),
        ],
        scratch_shapes=[pltpu.VMEM((1, V7X_LANES), F32)],
        compiler_params=_cparams(1),
        name="router",
    )(x2, g.reshape(1, d), wr)


def _dispatch_kernel(slot_ref, pstart_ref, plen_ref, used_ref, hn_ref, xs_ref, zero_ref, sem, zsem,
                     *, n_experts, tile):
    tm = hn_ref.shape[0] // SUB
    n_tiles = xs_ref.shape[0] // (tile * SUB)

    def row_copy(r, k):
        return pltpu.make_async_copy(
            _row_tile(hn_ref, r), _row_tile(xs_ref, slot_ref[TOP_K * r + k]), sem)

    _start_row_copies(tm, row_copy)
    for k in range(TOP_K):
        pltpu.make_async_copy(hn_ref, xs_ref.at[pl.ds(0, tm * SUB)], sem).wait()

    def rows(ref, start, count):
        return ref.at[pl.ds(pl.multiple_of(start * SUB, SUB), count * SUB)]

    def for_each_pad_copy(act):
        for e in range(n_experts):
            cur = pstart_ref[e]
            length = plen_ref[e]
            p = 1
            while p < tile:
                take = (length & p) != 0

                @pl.when(take)
                def _():
                    act(pltpu.make_async_copy(rows(zero_ref, 0, p), rows(xs_ref, cur, p), zsem))
                cur = cur + jnp.where(take, p, 0)
                p *= 2
        for i in range(n_experts):
            t = used_ref[0] + i

            @pl.when(t < n_tiles)
            def _():
                act(pltpu.make_async_copy(zero_ref, rows(xs_ref, t * tile, tile), zsem))

    @pl.when(pl.program_id(0) == pl.num_programs(0) - 1)
    def _():
        zero_ref[...] = jnp.zeros_like(zero_ref)
        for_each_pad_copy(lambda cp: cp.start())
        for_each_pad_copy(lambda cp: cp.wait())


def _dispatch(hn, slots_flat, pad_start, pad_len, used_tiles, n_tiles, tile, tm=1024):
    n = hn.shape[0] // SUB
    tm = min(tm, n)
    n_experts = pad_start.shape[0]
    smem = pl.BlockSpec(memory_space=pltpu.SMEM)
    return pl.pallas_call(
        functools.partial(_dispatch_kernel, n_experts=n_experts, tile=tile),
        grid=(n // tm,),
        in_specs=[
            pl.BlockSpec((TOP_K * tm,), lambda i: (i,), memory_space=pltpu.SMEM),
            smem, smem, smem,
            pl.BlockSpec((tm * SUB, V7X_LANES), lambda i: (i, 0)),
        ],
        out_specs=pl.BlockSpec(memory_space=pl.ANY),
        out_shape=jax.ShapeDtypeStruct((n_tiles * tile * SUB, V7X_LANES), F32),
        scratch_shapes=[
            pltpu.VMEM((tile * SUB, V7X_LANES), F32),
            pltpu.SemaphoreType.DMA(()),
            pltpu.SemaphoreType.DMA(()),
        ],
        compiler_params=_cparams(1),
        name="dispatch",
    )(slots_flat, pad_start, pad_len, used_tiles, hn)


def _grouped_kernel(exp_ref, used_ref, nsub_ref, x_ref, w1_ref, w3_ref, w2_ref, o_ref,
                    xb_ref, acc_ref, *, sub):
    i = pl.program_id(0)
    j = pl.program_id(1)
    last = pl.num_programs(1) - 1
    valid = i < used_ref[0]
    tile = acc_ref.shape[0]

    @pl.when(valid & (j == 0))
    def _():
        for c in range(SUB):
            xb_ref[:, c * V7X_LANES:(c + 1) * V7X_LANES] = _load_row_tiles(x_ref, tile, c).astype(BF16)
        acc_ref[...] = jnp.zeros_like(acc_ref)

    @pl.when(valid)
    def _():
        def sub_tile(s, w1b, w3b, w2b):
            rows = slice(s * sub, (s + 1) * sub)
            xb = xb_ref[rows, :]
            h = _silu(_dot(xb, w1b)) * _dot(xb, w3b)
            acc_ref[rows, :] += _dot(h.astype(BF16), w2b)

        sub_tile(0, w1_ref[...], w3_ref[...], w2_ref[...])
        for s in range(1, tile // sub):
            @pl.when(s < nsub_ref[i])
            def _():
                sub_tile(s, w1_ref[...], w3_ref[...], w2_ref[...])

    @pl.when(valid & (j == last))
    def _():
        _store_row_tiles(o_ref, acc_ref[...])

    @pl.when(jnp.logical_not(valid) & (j == last))
    def _():
        o_ref[...] = jnp.zeros_like(o_ref)


def _grouped(xs, w1, w3, w2, tile_exp, used_tiles, tile_nsub, tile, sub, tf=512):
    d = w1.shape[1]
    n_tiles = xs.shape[0] // (tile * SUB)
    nj = w1.shape[2] // tf

    def jsel(i, j, used):
        return jnp.where(i < used[0], j, nj - 1)

    grid_spec = pltpu.PrefetchScalarGridSpec(
        num_scalar_prefetch=3,
        grid=(n_tiles, nj),
        in_specs=[
            pl.BlockSpec((tile * SUB, V7X_LANES), lambda i, j, ex, us, ns: (i, 0)),
            pl.BlockSpec((None, d, tf), lambda i, j, ex, us, ns: (ex[i], 0, jsel(i, j, us))),
            pl.BlockSpec((None, d, tf), lambda i, j, ex, us, ns: (ex[i], 0, jsel(i, j, us))),
            pl.BlockSpec((None, tf, d), lambda i, j, ex, us, ns: (ex[i], jsel(i, j, us), 0)),
        ],
        out_specs=pl.BlockSpec((tile * SUB, V7X_LANES), lambda i, j, ex, us, ns: (i, 0)),
        scratch_shapes=[pltpu.VMEM((tile, d), BF16), pltpu.VMEM((tile, d), F32)],
    )
    return pl.pallas_call(
        functools.partial(_grouped_kernel, sub=sub),
        grid_spec=grid_spec,
        out_shape=jax.ShapeDtypeStruct(xs.shape, F32),
        compiler_params=_cparams(2),
        name="grouped_swiglu",
    )(tile_exp, used_tiles, tile_nsub, xs, w1, w3, w2)


def _combine_kernel(slot_ref, h_ref, gate_ref, g_ref, ys_ref, o_ref, buf_ref, sem):
    tm, d = h_ref.shape

    def row_copy(r, k):
        return pltpu.make_async_copy(
            _row_tile(ys_ref, slot_ref[TOP_K * r + k]), _row_tile(buf_ref.at[k], r), sem)

    _start_row_copies(tm, row_copy)
    for k in range(TOP_K):
        pltpu.make_async_copy(ys_ref.at[pl.ds(0, tm * SUB)], buf_ref.at[k], sem).wait()

    gate = gate_ref[...]
    ss = jnp.zeros((tm, 1), F32)
    for c in range(SUB):
        lanes = slice(c * V7X_LANES, (c + 1) * V7X_LANES)
        piece = (h_ref[:, lanes] + gate[:, 0:1] * _load_row_tiles(buf_ref.at[0], tm, c)
                 + gate[:, 1:2] * _load_row_tiles(buf_ref.at[1], tm, c))
        o_ref[:, lanes] = piece
        ss = ss + jnp.sum(piece * piece, axis=-1, keepdims=True)
    o_ref[...] = o_ref[...] * lax.rsqrt(ss / d + EPS) * g_ref[...]


def _combine(h2, gates, slots_flat, ys, g, tm=1024):
    n, d = h2.shape
    tm = min(tm, n)
    return pl.pallas_call(
        _combine_kernel,
        grid=(n // tm,),
        in_specs=[
            pl.BlockSpec((TOP_K * tm,), lambda i: (i,), memory_space=pltpu.SMEM),
            pl.BlockSpec((tm, d), lambda i: (i, 0)),
            pl.BlockSpec((tm, TOP_K), lambda i: (i, 0)),
            _const_spec((1, d)),
            pl.BlockSpec(memory_space=pl.ANY),
        ],
        out_specs=pl.BlockSpec((tm, d), lambda i: (i, 0)),
        out_shape=jax.ShapeDtypeStruct((n, d), F32),
        scratch_shapes=[pltpu.VMEM((TOP_K, tm * SUB, V7X_LANES), F32),
                        pltpu.SemaphoreType.DMA(())],
        compiler_params=_cparams(1),
        name="combine",
    )(slots_flat, h2, gates, g.reshape(1, d), ys)


def _moe(h2, g_ffn, w_router, w1, w3, w2, g_final, tile=1024, sub=512):
    n, d = h2.shape
    n_experts = w_router.shape[1]
    tile = min(tile, n)
    n_tiles = TOP_K * n // tile + n_experts
    hn, expert, rank, gates, counts = _router(h2, g_ffn, w_router)
    counts = counts[0, :n_experts]

    tiles_per = (counts + tile - 1) // tile
    ends = jnp.cumsum(tiles_per)
    starts = ends - tiles_per
    used = ends[-1:]
    onehot = expert[:, :, None] == jnp.arange(n_experts, dtype=I32)
    slots = rank + jnp.sum(jnp.where(onehot, starts * tile, 0), axis=-1)
    slots_flat = slots.reshape(-1).astype(I32)
    pad_start = (starts * tile + counts).astype(I32)
    pad_len = (tiles_per * tile - counts).astype(I32)
    tidx = jnp.minimum(jnp.arange(n_tiles, dtype=I32), used - 1)
    tile_exp = jnp.sum((tidx[:, None] >= ends[None, :]).astype(I32), axis=1)
    sub = min(sub, tile)
    tile_rows = jnp.clip(counts[tile_exp] - (tidx - starts[tile_exp]) * tile, 0, tile)
    tile_nsub = (tile_rows + sub - 1) // sub

    xs = _dispatch(hn, slots_flat, pad_start, pad_len, used.astype(I32), n_tiles, tile)
    ys = _grouped(xs, w1, w3, w2, tile_exp.astype(I32), used.astype(I32), tile_nsub.astype(I32),
                  tile, sub)
    return _combine(h2, gates, slots_flat, ys, g_final)


def kernel(x, ev_norm_mix, ev_w_in, ev_conv_w, hgrn_lower_bounds, ev_gnorm, ev_w_out,
           ev_norm_ffn, ev_ffn_w1, ev_ffn_w3, ev_ffn_w2,
           od_norm_mix, od_w_pw1, od_b_pw1, od_w_dw, od_b_dw, od_ln_g, od_ln_b,
           od_w_pw2, od_b_pw2, od_norm_ffn, od_router, od_moe_w1, od_moe_w3, od_moe_w2,
           final_norm):
    bsz, seq, d = x.shape
    assert ev_w_in.shape[0] == 1 and od_w_pw1.shape[0] == 1
    n_experts, _, n_ff = od_moe_w1.shape[1:]
    flat = lambda w: w.reshape(-1, w.shape[-1])
    h, w1b = _mixer0(x, ev_norm_mix[0], ev_w_in[0], ev_conv_w[0], hgrn_lower_bounds, ev_gnorm[0],
                     ev_w_out[0], layer=0, cast_w=flat(od_moe_w1[0]))
    h, w3b = _ffn(h.reshape(bsz * seq, d), ev_norm_ffn[0], ev_ffn_w1[0], ev_ffn_w3[0],
                  ev_ffn_w2[0], cast_w=flat(od_moe_w3[0]))
    h, w2b = _conformer(h.reshape(bsz, seq, d), od_norm_mix[0], od_w_pw1[0], od_b_pw1[0],
                        od_w_dw[0], od_b_dw[0], od_ln_g[0], od_ln_b[0], od_w_pw2[0], od_b_pw2[0],
                        cast_w=flat(od_moe_w2[0]))
    out = _moe(h.reshape(bsz * seq, d), od_norm_ffn[0], od_router[0],
               w1b.reshape(n_experts, d, n_ff), w3b.reshape(n_experts, d, n_ff),
               w2b.reshape(n_experts, n_ff, d), final_norm)
    return out.reshape(bsz, seq, d)
```

```python
import functools

import jax
import jax.numpy as jnp
from jax import lax
from jax.experimental import pallas as pl
from jax.experimental.pallas import tpu as pltpu

F32 = jnp.float32
BF16 = jnp.bfloat16
I32 = jnp.int32

EPS = 1e-6
CHUNK = 64
CHUNK_SLOTS = 2
SUB = 8
HEAD_DIM = 128
TOP_K = 2

V7X_LANES = 128
V7X_MXU_DIM = 256
V7X_VMEM_BYTES = 64 * 1024 * 1024
VMEM_LIMIT = V7X_VMEM_BYTES - 8 * 1024 * 1024


def _cparams(n_axes):
    return pltpu.CompilerParams(
        dimension_semantics=("arbitrary",) * n_axes, vmem_limit_bytes=VMEM_LIMIT)


def _const_spec(shape):
    nd = len(shape)
    return pl.BlockSpec(shape, lambda *_: (0,) * nd, pipeline_mode=pl.Buffered(1))


def _rms(x, g):
    return x * lax.rsqrt(jnp.mean(x * x, axis=-1, keepdims=True) + EPS) * g


def _silu(x):
    return x * jax.nn.sigmoid(x)


def _dot(a, b):
    return jnp.dot(a, b, preferred_element_type=F32)


def _dot_nt(a, b):
    return lax.dot_general(a, b, (((1,), (1,)), ((), ())), preferred_element_type=F32)


def _store_row_tiles(ref, val):
    m = val.shape[0]
    for c in range(SUB):
        ref[pl.ds(c, m, stride=SUB), :] = val[:, c * V7X_LANES:(c + 1) * V7X_LANES]


def _load_row_tiles(ref, m, c):
    return ref[pl.ds(c, m, stride=SUB), :]


def _row_tile(ref, row):
    return ref.at[pl.ds(pl.multiple_of(row * SUB, SUB), SUB)]


ROW_COPY_UNROLL = 8


def _start_row_copies(n_rows, row_copy):
    def body(i, carry):
        for u in range(ROW_COPY_UNROLL):
            for k in range(TOP_K):
                row_copy(i * ROW_COPY_UNROLL + u, k).start(priority=(u * TOP_K + k) % 2)
        return carry

    lax.fori_loop(0, n_rows // ROW_COPY_UNROLL, body, 0)


def _mixer0_tile(x_ref, xn_ref, gmix_ref, win_ref, convw_ref, lbp_ref, gn_ref, wout_ref, o_ref,
                 proj_ref, next_ref, zbuf_ref, st_ref, y_ref, w_ref, s_ref, *, layer, conv_dim,
                 width):
    tt = x_ref.shape[0]
    heads = width // HEAD_DIM
    c3 = 3 * conv_dim

    x = x_ref[...]
    next_ref[...] = _dot(_rms(xn_ref[...], gmix_ref[...]).astype(BF16), win_ref[...])

    z = proj_ref[:, 2 * conv_dim:c3] * proj_ref[:, 0:conv_dim]
    zbuf_ref[SUB:SUB + tt, :] = z
    cw = convw_ref[...]
    conv = (cw[2:3, :] * z + cw[1:2, :] * zbuf_ref[SUB - 1:SUB - 1 + tt, :]
            + cw[0:1, :] * zbuf_ref[SUB - 2:SUB - 2 + tt, :])
    y_ref[:, 0:conv_dim] = (proj_ref[:, conv_dim:2 * conv_dim] * conv).astype(BF16)
    zbuf_ref[0:SUB, :] = zbuf_ref[tt:tt + SUB, :]

    lbp = lbp_ref[...]
    lbe = jnp.exp(lbp - jnp.max(lbp, axis=0, keepdims=True))
    lb = jnp.sum(lbe[0:layer + 1, :], axis=0, keepdims=True) / jnp.sum(lbe, axis=0, keepdims=True)

    row = lax.broadcasted_iota(I32, (CHUNK, width), 0)
    row8 = lax.broadcasted_iota(I32, (SUB, width), 0)
    ti = lax.broadcasted_iota(I32, (CHUNK, CHUNK), 0)
    si = lax.broadcasted_iota(I32, (CHUNK, CHUNK), 1)
    tri = (si <= ti).astype(BF16)
    bdr = lax.broadcasted_iota(I32, (V7X_MXU_DIM, V7X_MXU_DIM), 0) // HEAD_DIM
    bdc = lax.broadcasted_iota(I32, (V7X_MXU_DIM, V7X_MXU_DIM), 1) // HEAD_DIM
    head_ones = (bdr == bdc).astype(BF16)
    gn = gn_ref[...]

    def bcast_row(val, r, n):
        return jnp.broadcast_to(val[r:r + 1, :], (n, val.shape[1]))

    def chunk(r0, slot):
        rows = slice(r0, r0 + CHUNK)
        ws_ref = w_ref.at[slot]
        q = proj_ref[rows, c3:c3 + width]
        f = proj_ref[rows, c3 + width:c3 + 2 * width]
        v = proj_ref[rows, c3 + 2 * width:c3 + 3 * width]
        og = proj_ref[rows, c3 + 3 * width:c3 + 4 * width]
        fg = lb + (1.0 - lb) * jax.nn.sigmoid(f)
        k = 1.0 - fg
        g_rest = jnp.log(fg)
        b = jnp.zeros((CHUNK, width), F32)
        for _ in range(3):
            g_part = g_rest.astype(BF16)
            b = b + _dot(tri, g_part)
            g_rest = g_rest - g_part.astype(F32)
        b_last = b[CHUNK - 1:CHUNK, :]
        v16 = v.astype(BF16)

        q_s = (q * jnp.exp(b)).astype(BF16)
        k_s = (k * jnp.exp(b_last - b)).astype(BF16)
        e_last = jnp.exp(b_last)

        p_acc = [jnp.zeros((CHUNK, CHUNK), F32) for _ in range(heads)]
        half = CHUNK // 2
        while half >= SUB:
            blk = 2 * half
            anc = jnp.concatenate(
                [bcast_row(b, i * blk + half, blk) for i in range(CHUNK // blk)], axis=0)
            upper = (row % blk) >= half
            q_l = (q * jnp.where(upper, jnp.exp(b - anc), 0.0)).astype(BF16)
            k_l = (k * jnp.where(upper, 0.0, jnp.exp(anc - b))).astype(BF16)
            same = (ti // blk) == (si // blk)
            for h in range(heads):
                hs = slice(h * HEAD_DIM, (h + 1) * HEAD_DIM)
                p_acc[h] = p_acc[h] + jnp.where(same, _dot_nt(q_l[:, hs], k_l[:, hs]), 0.0)
            half //= 2

        for rb in range(CHUNK // SUB):
            rs = slice(rb * SUB, (rb + 1) * SUB)
            b8, q8, k8 = b[rs, :], q[rs, :], k[rs, :]
            for s in range(0, SUB, 2):
                pair = []
                for s1 in (s, s + 1):
                    dec = jnp.where(row8 >= s1, jnp.exp(b8 - bcast_row(b8, s1, SUB)), 0.0)
                    pair.append(q8 * dec * bcast_row(k8, s1, SUB))
                ws_ref[(rb * SUB + s) * SUB:(rb * SUB + s + 2) * SUB, :] = (
                    jnp.concatenate(pair, axis=0).astype(BF16))
        sc_ref = s_ref.at[slot]
        for j in range(width // V7X_MXU_DIM):
            lanes = slice(j * V7X_MXU_DIM, (j + 1) * V7X_MXU_DIM)
            sc_ref[:, lanes] = _dot(ws_ref[:, lanes], head_ones)
        o_diag = []
        for rb in range(CHUNK // SUB):
            sc = sc_ref[rb * SUB * SUB:(rb + 1) * SUB * SUB, :]
            v8 = v[rb * SUB:(rb + 1) * SUB, :]
            acc = None
            for s in range(SUB):
                term = sc[s * SUB:(s + 1) * SUB, :] * bcast_row(v8, s, SUB)
                acc = term if acc is None else acc + term
            o_diag.append(acc)
        o_diag = jnp.concatenate(o_diag, axis=0)

        outs = []
        for h in range(heads):
            hs = slice(h * HEAD_DIM, (h + 1) * HEAD_DIM)
            st = st_ref[h]
            o_h = (_dot_nt(q_s[:, hs], st.astype(BF16)) + _dot(p_acc[h].astype(BF16), v16[:, hs])
                   + o_diag[:, hs])
            st_ref[h] = e_last[:, hs] * st + _dot(v16[:, hs].T, k_s[:, hs])
            o_n = o_h * lax.rsqrt(jnp.mean(o_h * o_h, axis=-1, keepdims=True) + EPS)
            outs.append(o_n)
        o = jnp.concatenate(outs, axis=1) * gn * _silu(og)
        y_ref[rows, conv_dim:conv_dim + width] = o.astype(BF16)

    for c in range(tt // CHUNK):
        chunk(c * CHUNK, c % CHUNK_SLOTS)
    o_ref[...] = x + _dot(y_ref[...], wout_ref[...])


def _mixer0_kernel(x_ref, xn_ref, gmix_ref, win_ref, convw_ref, lbp_ref, gn_ref, wout_ref,
                   cast_ref, o_ref, cast_o_ref, proj_a_ref, proj_b_ref, zbuf_ref, st_ref, y_ref,
                   w_ref, s_ref, *, tiles_per_seq, **static):
    i = pl.program_id(0)

    @pl.when(i == 0)
    def _():
        proj_a_ref[...] = _dot(_rms(x_ref[...], gmix_ref[...]).astype(BF16), win_ref[...])

    @pl.when(i % tiles_per_seq == 0)
    def _():
        st_ref[...] = jnp.zeros_like(st_ref)
        zbuf_ref[0:SUB, :] = jnp.zeros((SUB, zbuf_ref.shape[1]), F32)

    for parity, (cur, nxt) in enumerate(((proj_a_ref, proj_b_ref), (proj_b_ref, proj_a_ref))):
        @pl.when(i % 2 == parity)
        def _():
            _mixer0_tile(x_ref, xn_ref, gmix_ref, win_ref, convw_ref, lbp_ref, gn_ref, wout_ref,
                         o_ref, cur, nxt, zbuf_ref, st_ref, y_ref, w_ref, s_ref, **static)

    cast_o_ref[...] = cast_ref[...].astype(BF16)


def _cast_job(w2d, n_steps, step_of):
    rows, cols = w2d.shape
    assert rows % n_steps == 0
    spec = pl.BlockSpec((rows // n_steps, cols), lambda *ids: (step_of(*ids), 0))
    return spec, jax.ShapeDtypeStruct(w2d.shape, BF16)


def _mixer0(x, gmix, w_in, conv_w, lbp, gnorm, w_out, layer, cast_w, tt=256):
    bsz, seq, d = x.shape
    conv_dim = conv_w.shape[1]
    width = lbp.shape[1]
    n_in = w_in.shape[1]
    tt = min(tt, seq)
    heads = width // HEAD_DIM
    gn = jnp.tile(gnorm.reshape(1, HEAD_DIM), (1, heads))
    n_t = seq // tt
    n_tiles = bsz * n_t
    x2 = x.reshape(bsz * seq, d)
    cast_spec, cast_shape = _cast_job(cast_w, n_tiles, lambda i: i)
    kern = functools.partial(_mixer0_kernel, tiles_per_seq=n_t, layer=layer, conv_dim=conv_dim,
                             width=width)
    out, cast_out = pl.pallas_call(
        kern,
        grid=(n_tiles,),
        in_specs=[
            pl.BlockSpec((tt, d), lambda i: (i, 0)),
            pl.BlockSpec((tt, d), lambda i: (jnp.minimum(i + 1, n_tiles - 1), 0)),
            _const_spec((1, d)),
            _const_spec((d, n_in)),
            _const_spec(conv_w.shape),
            _const_spec(lbp.shape),
            _const_spec((1, width)),
            _const_spec(w_out.shape),
            cast_spec,
        ],
        out_specs=[pl.BlockSpec((tt, d), lambda i: (i, 0)), cast_spec],
        out_shape=[jax.ShapeDtypeStruct(x2.shape, F32), cast_shape],
        scratch_shapes=[
            pltpu.VMEM((tt, n_in), F32),
            pltpu.VMEM((tt, n_in), F32),
            pltpu.VMEM((tt + SUB, conv_dim), F32),
            pltpu.VMEM((heads, HEAD_DIM, HEAD_DIM), F32),
            pltpu.VMEM((tt, conv_dim + width), BF16),
            pltpu.VMEM((CHUNK_SLOTS, CHUNK * SUB, width), BF16),
            pltpu.VMEM((CHUNK_SLOTS, CHUNK * SUB, width), F32),
        ],
        compiler_params=_cparams(1),
        name="mixer0",
    )(x2, x2, gmix.reshape(1, d), w_in.astype(BF16), conv_w, lbp, gn, w_out.astype(BF16), cast_w)
    return out.reshape(x.shape), cast_out


def _ffn_kernel(x_ref, g_ref, w1_ref, w3_ref, w2_ref, cast_ref, o_ref, cast_o_ref, *, tf):
    x = x_ref[...]
    hn = _rms(x, g_ref[...]).astype(BF16)
    acc = x
    for j in range(w1_ref.shape[1] // tf):
        cols = slice(j * tf, (j + 1) * tf)
        h = _silu(_dot(hn, w1_ref[:, cols])) * _dot(hn, w3_ref[:, cols])
        acc = acc + _dot(h.astype(BF16), w2_ref[cols, :])
    o_ref[...] = acc
    cast_o_ref[...] = cast_ref[...].astype(BF16)


def _ffn(x2, g, w1, w3, w2, cast_w, tm=512, tf=256):
    n, d = x2.shape
    tm = min(tm, n)
    cast_spec, cast_shape = _cast_job(cast_w, n // tm, lambda i: i)
    return pl.pallas_call(
        functools.partial(_ffn_kernel, tf=tf),
        grid=(n // tm,),
        in_specs=[
            pl.BlockSpec((tm, d), lambda i: (i, 0)),
            _const_spec((1, d)),
            _const_spec(w1.shape),
            _const_spec(w3.shape),
            _const_spec(w2.shape),
            cast_spec,
        ],
        out_specs=[pl.BlockSpec((tm, d), lambda i: (i, 0)), cast_spec],
        out_shape=[jax.ShapeDtypeStruct((n, d), F32), cast_shape],
        compiler_params=_cparams(1),
        name="ffn0",
    )(x2, g.reshape(1, d), w1.astype(BF16), w3.astype(BF16), w2.astype(BF16), cast_w)


def _conformer_kernel(x_ref, g_ref, w1_ref, b1_ref, wdw_ref, bdw_ref, lng_ref, lnb_ref,
                      w2_ref, b2_ref, cast_ref, o_ref, cast_o_ref, ubuf_ref, conv_ref, *, halo):
    tt = x_ref.shape[1]
    d = x_ref.shape[2]
    kw = wdw_ref.shape[0]

    @pl.when(pl.program_id(1) == 0)
    def _():
        ubuf_ref[0:halo, :] = jnp.zeros((halo, d), F32)
        ubuf_ref[halo + tt:halo + tt + SUB, :] = jnp.zeros((SUB, d), F32)

    x = x_ref[0]
    hn = _rms(x, g_ref[...]).astype(BF16)
    p = _dot(hn, w1_ref[...]) + b1_ref[...]
    ubuf_ref[halo:halo + tt, :] = p[:, 0:d] * jax.nn.sigmoid(p[:, d:2 * d])
    off0 = halo - (kw - 1)
    seg = min(tt, 128)
    for c in range(d // V7X_LANES):
        lanes = slice(c * V7X_LANES, (c + 1) * V7X_LANES)
        wcol = wdw_ref[:, lanes]
        for base in range(0, tt, seg):
            acc = None
            for r in range(SUB):
                part = None
                for a in range((off0 + kw - 1) // SUB + 1):
                    k = SUB * a + r - off0
                    if 0 <= k < kw:
                        win = ubuf_ref[base + SUB * a:base + SUB * a + seg + SUB, lanes]
                        term = wcol[k:k + 1, :] * win
                        part = term if part is None else part + term
                if part is not None:
                    shifted = part[r:r + seg, :]
                    acc = shifted if acc is None else acc + shifted
            conv_ref[base:base + seg, lanes] = acc + bdw_ref[:, lanes]
    acc = conv_ref[...]
    ubuf_ref[0:halo, :] = ubuf_ref[tt:tt + halo, :]
    mu = jnp.mean(acc, axis=-1, keepdims=True)
    xc = acc - mu
    var = jnp.mean(xc * xc, axis=-1, keepdims=True)
    u = _silu(xc * lax.rsqrt(var + EPS) * lng_ref[...] + lnb_ref[...])
    o_ref[0] = x + _dot(u.astype(BF16), w2_ref[...]) + b2_ref[...]
    cast_o_ref[...] = cast_ref[...].astype(BF16)


def _conformer(x, g, w_pw1, b_pw1, w_dw, b_dw, ln_g, ln_b, w_pw2, b_pw2, cast_w, tt=256):
    bsz, seq, d = x.shape
    tt = min(tt, seq)
    kw = w_dw.shape[0]
    halo = -(-(kw - 1) // SUB) * SUB
    r = lambda a: a.reshape(1, -1)
    n_t = seq // tt
    cast_spec, cast_shape = _cast_job(cast_w, bsz * n_t, lambda b, t: b * n_t + t)
    return pl.pallas_call(
        functools.partial(_conformer_kernel, halo=halo),
        grid=(bsz, seq // tt),
        in_specs=[
            pl.BlockSpec((1, tt, d), lambda b, t: (b, t, 0)),
            _const_spec((1, d)),
            _const_spec(w_pw1.shape),
            _const_spec((1, 2 * d)),
            _const_spec(w_dw.shape),
            _const_spec((1, d)),
            _const_spec((1, d)),
            _const_spec((1, d)),
            _const_spec(w_pw2.shape),
            _const_spec((1, d)),
            cast_spec,
        ],
        out_specs=[pl.BlockSpec((1, tt, d), lambda b, t: (b, t, 0)), cast_spec],
        out_shape=[jax.ShapeDtypeStruct(x.shape, F32), cast_shape],
        scratch_shapes=[pltpu.VMEM((halo + tt + SUB, d), F32), pltpu.VMEM((tt, d), F32)],
        compiler_params=_cparams(2),
        name="conformer",
    )(x, r(g), w_pw1.astype(BF16), r(b_pw1), w_dw, r(b_dw), r(ln_g), r(ln_b),
      w_pw2.astype(BF16), r(b_pw2), cast_w)


def _router_kernel(x_ref, g_ref, wr_ref, hn_ref, exp_ref, rank_ref, gate_ref, cnt_ref, run_ref,
                   *, n_experts):
    tm = x_ref.shape[0]

    @pl.when(pl.program_id(0) == 0)
    def _():
        run_ref[...] = jnp.zeros_like(run_ref)

    hn = _rms(x_ref[...], g_ref[...])
    _store_row_tiles(hn_ref, hn)
    wr = wr_ref[...]
    hn_hi, wr_hi = hn.astype(BF16), wr.astype(BF16)
    hn_lo = (hn - hn_hi.astype(F32)).astype(BF16)
    wr_lo = (wr - wr_hi.astype(F32)).astype(BF16)
    logits = _dot(hn_hi, wr_hi) + (_dot(hn_lo, wr_hi) + _dot(hn_hi, wr_lo))
    lane = lax.broadcasted_iota(I32, logits.shape, 1)
    neg = jnp.float32(-jnp.inf)
    big = jnp.int32(V7X_LANES)
    logits = jnp.where(lane < n_experts, logits, neg)
    m1 = jnp.max(logits, axis=-1, keepdims=True)
    i1 = jnp.min(jnp.where(logits == m1, lane, big), axis=-1, keepdims=True)
    rest = jnp.where(lane == i1, neg, logits)
    m2 = jnp.max(rest, axis=-1, keepdims=True)
    i2 = jnp.min(jnp.where(rest == m2, lane, big), axis=-1, keepdims=True)
    e2 = jnp.exp(m2 - m1)
    g1 = 1.0 / (1.0 + e2)
    g2 = e2 / (1.0 + e2)

    sel1 = lane == i1
    sel2 = lane == i2
    onehot = jnp.where(sel1 | sel2, 1.0, 0.0)
    ti = lax.broadcasted_iota(I32, (tm, tm), 0)
    si = lax.broadcasted_iota(I32, (tm, tm), 1)
    before = _dot((si < ti).astype(BF16), onehot.astype(BF16)) + run_ref[...]
    r1 = jnp.sum(jnp.where(sel1, before, 0.0), axis=-1, keepdims=True).astype(I32)
    r2 = jnp.sum(jnp.where(sel2, before, 0.0), axis=-1, keepdims=True).astype(I32)
    run_ref[...] = run_ref[...] + jnp.sum(onehot, axis=0, keepdims=True)

    col = lax.broadcasted_iota(I32, (tm, TOP_K), 1)
    exp_ref[...] = jnp.where(col == 0, i1, i2)
    rank_ref[...] = jnp.where(col == 0, r1, r2)
    gate_ref[...] = jnp.where(col == 0, g1, g2)
    cnt_ref[...] = run_ref[...].astype(I32)


def _router(x2, g, w_router, tm=512):
    n, d = x2.shape
    assert d == SUB * V7X_LANES
    tm = min(tm, n)
    n_experts = w_router.shape[1]
    wr = jnp.zeros((d, V7X_LANES), F32).at[:, :n_experts].set(w_router)
    per_token = lambda dt: jax.ShapeDtypeStruct((n, TOP_K), dt)
    return pl.pallas_call(
        functools.partial(_router_kernel, n_experts=n_experts),
        grid=(n // tm,),
        in_specs=[
            pl.BlockSpec((tm, d), lambda i: (i, 0)),
            _const_spec((1, d)),
            _const_spec((d, V7X_LANES)),
        ],
        out_specs=[
            pl.BlockSpec((tm * SUB, V7X_LANES), lambda i: (i, 0)),
            pl.BlockSpec((tm, TOP_K), lambda i: (i, 0)),
            pl.BlockSpec((tm, TOP_K), lambda i: (i, 0)),
            pl.BlockSpec((tm, TOP_K), lambda i: (i, 0)),
            pl.BlockSpec((1, V7X_LANES), lambda i: (0, 0)),
        ],
        out_shape=[
            jax.ShapeDtypeStruct((n * SUB, V7X_LANES), F32),
            per_token(I32), per_token(I32), per_token(F32),
            jax.ShapeDtypeStruct((1, V7X_LANES), I32),
        ],
        scratch_shapes=[pltpu.VMEM((1, V7X_LANES), F32)],
        compiler_params=_cparams(1),
        name="router",
    )(x2, g.reshape(1, d), wr)


def _dispatch_kernel(slot_ref, pstart_ref, plen_ref, used_ref, hn_ref, xs_ref, zero_ref, sem, zsem,
                     *, n_experts, tile):
    tm = hn_ref.shape[0] // SUB
    n_tiles = xs_ref.shape[0] // (tile * SUB)

    def row_copy(r, k):
        return pltpu.make_async_copy(
            _row_tile(hn_ref, r), _row_tile(xs_ref, slot_ref[TOP_K * r + k]), sem)

    _start_row_copies(tm, row_copy)
    for k in range(TOP_K):
        pltpu.make_async_copy(hn_ref, xs_ref.at[pl.ds(0, tm * SUB)], sem).wait()

    def rows(ref, start, count):
        return ref.at[pl.ds(pl.multiple_of(start * SUB, SUB), count * SUB)]

    def for_each_pad_copy(act):
        for e in range(n_experts):
            cur = pstart_ref[e]
            length = plen_ref[e]
            p = 1
            while p < tile:
                take = (length & p) != 0

                @pl.when(take)
                def _():
                    act(pltpu.make_async_copy(rows(zero_ref, 0, p), rows(xs_ref, cur, p), zsem))
                cur = cur + jnp.where(take, p, 0)
                p *= 2
        for i in range(n_experts):
            t = used_ref[0] + i

            @pl.when(t < n_tiles)
            def _():
                act(pltpu.make_async_copy(zero_ref, rows(xs_ref, t * tile, tile), zsem))

    @pl.when(pl.program_id(0) == pl.num_programs(0) - 1)
    def _():
        zero_ref[...] = jnp.zeros_like(zero_ref)
        for_each_pad_copy(lambda cp: cp.start())
        for_each_pad_copy(lambda cp: cp.wait())


def _dispatch(hn, slots_flat, pad_start, pad_len, used_tiles, n_tiles, tile, tm=1024):
    n = hn.shape[0] // SUB
    tm = min(tm, n)
    n_experts = pad_start.shape[0]
    smem = pl.BlockSpec(memory_space=pltpu.SMEM)
    return pl.pallas_call(
        functools.partial(_dispatch_kernel, n_experts=n_experts, tile=tile),
        grid=(n // tm,),
        in_specs=[
            pl.BlockSpec((TOP_K * tm,), lambda i: (i,), memory_space=pltpu.SMEM),
            smem, smem, smem,
            pl.BlockSpec((tm * SUB, V7X_LANES), lambda i: (i, 0)),
        ],
        out_specs=pl.BlockSpec(memory_space=pl.ANY),
        out_shape=jax.ShapeDtypeStruct((n_tiles * tile * SUB, V7X_LANES), F32),
        scratch_shapes=[
            pltpu.VMEM((tile * SUB, V7X_LANES), F32),
            pltpu.SemaphoreType.DMA(()),
            pltpu.SemaphoreType.DMA(()),
        ],
        compiler_params=_cparams(1),
        name="dispatch",
    )(slots_flat, pad_start, pad_len, used_tiles, hn)


def _grouped_kernel(exp_ref, used_ref, nsub_ref, x_ref, w1_ref, w3_ref, w2_ref, o_ref,
                    xb_ref, acc_ref, *, sub):
    i = pl.program_id(0)
    j = pl.program_id(1)
    last = pl.num_programs(1) - 1
    valid = i < used_ref[0]
    tile = acc_ref.shape[0]

    @pl.when(valid & (j == 0))
    def _():
        for c in range(SUB):
            xb_ref[:, c * V7X_LANES:(c + 1) * V7X_LANES] = _load_row_tiles(x_ref, tile, c).astype(BF16)
        acc_ref[...] = jnp.zeros_like(acc_ref)

    @pl.when(valid)
    def _():
        def sub_tile(s, w1b, w3b, w2b):
            rows = slice(s * sub, (s + 1) * sub)
            xb = xb_ref[rows, :]
            h = _silu(_dot(xb, w1b)) * _dot(xb, w3b)
            acc_ref[rows, :] += _dot(h.astype(BF16), w2b)

        sub_tile(0, w1_ref[...], w3_ref[...], w2_ref[...])
        for s in range(1, tile // sub):
            @pl.when(s < nsub_ref[i])
            def _():
                sub_tile(s, w1_ref[...], w3_ref[...], w2_ref[...])

    @pl.when(valid & (j == last))
    def _():
        _store_row_tiles(o_ref, acc_ref[...])

    @pl.when(jnp.logical_not(valid) & (j == last))
    def _():
        o_ref[...] = jnp.zeros_like(o_ref)


def _grouped(xs, w1, w3, w2, tile_exp, used_tiles, tile_nsub, tile, sub, tf=512):
    d = w1.shape[1]
    n_tiles = xs.shape[0] // (tile * SUB)
    nj = w1.shape[2] // tf

    def jsel(i, j, used):
        return jnp.where(i < used[0], j, nj - 1)

    grid_spec = pltpu.PrefetchScalarGridSpec(
        num_scalar_prefetch=3,
        grid=(n_tiles, nj),
        in_specs=[
            pl.BlockSpec((tile * SUB, V7X_LANES), lambda i, j, ex, us, ns: (i, 0)),
            pl.BlockSpec((None, d, tf), lambda i, j, ex, us, ns: (ex[i], 0, jsel(i, j, us))),
            pl.BlockSpec((None, d, tf), lambda i, j, ex, us, ns: (ex[i], 0, jsel(i, j, us))),
            pl.BlockSpec((None, tf, d), lambda i, j, ex, us, ns: (ex[i], jsel(i, j, us), 0)),
        ],
        out_specs=pl.BlockSpec((tile * SUB, V7X_LANES), lambda i, j, ex, us, ns: (i, 0)),
        scratch_shapes=[pltpu.VMEM((tile, d), BF16), pltpu.VMEM((tile, d), F32)],
    )
    return pl.pallas_call(
        functools.partial(_grouped_kernel, sub=sub),
        grid_spec=grid_spec,
        out_shape=jax.ShapeDtypeStruct(xs.shape, F32),
        compiler_params=_cparams(2),
        name="grouped_swiglu",
    )(tile_exp, used_tiles, tile_nsub, xs, w1, w3, w2)


def _combine_kernel(slot_ref, h_ref, gate_ref, g_ref, ys_ref, o_ref, buf_ref, sem):
    tm, d = h_ref.shape

    def row_copy(r, k):
        return pltpu.make_async_copy(
            _row_tile(ys_ref, slot_ref[TOP_K * r + k]), _row_tile(buf_ref.at[k], r), sem)

    _start_row_copies(tm, row_copy)
    for k in range(TOP_K):
        pltpu.make_async_copy(ys_ref.at[pl.ds(0, tm * SUB)], buf_ref.at[k], sem).wait()

    gate = gate_ref[...]
    ss = jnp.zeros((tm, 1), F32)
    for c in range(SUB):
        lanes = slice(c * V7X_LANES, (c + 1) * V7X_LANES)
        piece = (h_ref[:, lanes] + gate[:, 0:1] * _load_row_tiles(buf_ref.at[0], tm, c)
                 + gate[:, 1:2] * _load_row_tiles(buf_ref.at[1], tm, c))
        o_ref[:, lanes] = piece
        ss = ss + jnp.sum(piece * piece, axis=-1, keepdims=True)
    o_ref[...] = o_ref[...] * lax.rsqrt(ss / d + EPS) * g_ref[...]


def _combine(h2, gates, slots_flat, ys, g, tm=1024):
    n, d = h2.shape
    tm = min(tm, n)
    return pl.pallas_call(
        _combine_kernel,
        grid=(n // tm,),
        in_specs=[
            pl.BlockSpec((TOP_K * tm,), lambda i: (i,), memory_space=pltpu.SMEM),
            pl.BlockSpec((tm, d), lambda i: (i, 0)),
            pl.BlockSpec((tm, TOP_K), lambda i: (i, 0)),
            _const_spec((1, d)),
            pl.BlockSpec(memory_space=pl.ANY),
        ],
        out_specs=pl.BlockSpec((tm, d), lambda i: (i, 0)),
        out_shape=jax.ShapeDtypeStruct((n, d), F32),
        scratch_shapes=[pltpu.VMEM((TOP_K, tm * SUB, V7X_LANES), F32),
                        pltpu.SemaphoreType.DMA(())],
        compiler_params=_cparams(1),
        name="combine",
    )(slots_flat, h2, gates, g.reshape(1, d), ys)


def _moe(h2, g_ffn, w_router, w1, w3, w2, g_final, tile=1024, sub=512):
    n, d = h2.shape
    n_experts = w_router.shape[1]
    tile = min(tile, n)
    n_tiles = TOP_K * n // tile + n_experts
    hn, expert, rank, gates, counts = _router(h2, g_ffn, w_router)
    counts = counts[0, :n_experts]

    tiles_per = (counts + tile - 1) // tile
    ends = jnp.cumsum(tiles_per)
    starts = ends - tiles_per
    used = ends[-1:]
    onehot = expert[:, :, None] == jnp.arange(n_experts, dtype=I32)
    slots = rank + jnp.sum(jnp.where(onehot, starts * tile, 0), axis=-1)
    slots_flat = slots.reshape(-1).astype(I32)
    pad_start = (starts * tile + counts).astype(I32)
    pad_len = (tiles_per * tile - counts).astype(I32)
    tidx = jnp.minimum(jnp.arange(n_tiles, dtype=I32), used - 1)
    tile_exp = jnp.sum((tidx[:, None] >= ends[None, :]).astype(I32), axis=1)
    sub = min(sub, tile)
    tile_rows = jnp.clip(counts[tile_exp] - (tidx - starts[tile_exp]) * tile, 0, tile)
    tile_nsub = (tile_rows + sub - 1) // sub

    xs = _dispatch(hn, slots_flat, pad_start, pad_len, used.astype(I32), n_tiles, tile)
    ys = _grouped(xs, w1, w3, w2, tile_exp.astype(I32), used.astype(I32), tile_nsub.astype(I32),
                  tile, sub)
    return _combine(h2, gates, slots_flat, ys, g_final)


def kernel(x, ev_norm_mix, ev_w_in, ev_conv_w, hgrn_lower_bounds, ev_gnorm, ev_w_out,
           ev_norm_ffn, ev_ffn_w1, ev_ffn_w3, ev_ffn_w2,
           od_norm_mix, od_w_pw1, od_b_pw1, od_w_dw, od_b_dw, od_ln_g, od_ln_b,
           od_w_pw2, od_b_pw2, od_norm_ffn, od_router, od_moe_w1, od_moe_w3, od_moe_w2,
           final_norm):
    bsz, seq, d = x.shape
    assert ev_w_in.shape[0] == 1 and od_w_pw1.shape[0] == 1
    n_experts, _, n_ff = od_moe_w1.shape[1:]
    flat = lambda w: w.reshape(-1, w.shape[-1])
    h, w1b = _mixer0(x, ev_norm_mix[0], ev_w_in[0], ev_conv_w[0], hgrn_lower_bounds, ev_gnorm[0],
                     ev_w_out[0], layer=0, cast_w=flat(od_moe_w1[0]))
    h, w3b = _ffn(h.reshape(bsz * seq, d), ev_norm_ffn[0], ev_ffn_w1[0], ev_ffn_w3[0],
                  ev_ffn_w2[0], cast_w=flat(od_moe_w3[0]))
    h, w2b = _conformer(h.reshape(bsz, seq, d), od_norm_mix[0], od_w_pw1[0], od_b_pw1[0],
                        od_w_dw[0], od_b_dw[0], od_ln_g[0], od_ln_b[0], od_w_pw2[0], od_b_pw2[0],
                        cast_w=flat(od_moe_w2[0]))
    out = _moe(h.reshape(bsz * seq, d), od_norm_ffn[0], od_router[0],
               w1b.reshape(n_experts, d, n_ff), w3b.reshape(n_experts, d, n_ff),
               w2b.reshape(n_experts, n_ff, d), final_norm)
    return out.reshape(bsz, seq, d)
```

```python
import functools

import jax
import jax.numpy as jnp
from jax import lax
from jax.experimental import pallas as pl
from jax.experimental.pallas import tpu as pltpu

F32 = jnp.float32
BF16 = jnp.bfloat16
I32 = jnp.int32

EPS = 1e-6
CHUNK = 64
CHUNK_SLOTS = 2
SUB = 8
HEAD_DIM = 128
TOP_K = 2

V7X_LANES = 128
V7X_MXU_DIM = 256
V7X_VMEM_BYTES = 64 * 1024 * 1024
VMEM_LIMIT = V7X_VMEM_BYTES - 8 * 1024 * 1024


def _cparams(n_axes):
    return pltpu.CompilerParams(
        dimension_semantics=("arbitrary",) * n_axes, vmem_limit_bytes=VMEM_LIMIT)


def _const_spec(shape):
    nd = len(shape)
    return pl.BlockSpec(shape, lambda *_: (0,) * nd, pipeline_mode=pl.Buffered(1))


def _rms(x, g):
    return x * lax.rsqrt(jnp.mean(x * x, axis=-1, keepdims=True) + EPS) * g


def _silu(x):
    return x * jax.nn.sigmoid(x)


def _dot(a, b):
    return jnp.dot(a, b, preferred_element_type=F32)


def _dot_nt(a, b):
    return lax.dot_general(a, b, (((1,), (1,)), ((), ())), preferred_element_type=F32)


def _store_row_tiles(ref, val):
    m = val.shape[0]
    for c in range(SUB):
        ref[pl.ds(c, m, stride=SUB), :] = val[:, c * V7X_LANES:(c + 1) * V7X_LANES]


def _load_row_tiles(ref, m, c):
    return ref[pl.ds(c, m, stride=SUB), :]


def _row_tile(ref, row):
    return ref.at[pl.ds(pl.multiple_of(row * SUB, SUB), SUB)]


ROW_COPY_UNROLL = 8


def _start_row_copies(n_rows, row_copy):
    def body(i, carry):
        for u in range(ROW_COPY_UNROLL):
            for k in range(TOP_K):
                row_copy(i * ROW_COPY_UNROLL + u, k).start(priority=(u * TOP_K + k) % 2)
        return carry

    lax.fori_loop(0, n_rows // ROW_COPY_UNROLL, body, 0)


def _mixer0_tile(x_ref, xn_ref, gmix_ref, win_ref, convw_ref, lbp_ref, gn_ref, wout_ref, o_ref,
                 proj_ref, next_ref, zbuf_ref, st_ref, y_ref, w_ref, s_ref, *, layer, conv_dim,
                 width):
    tt = x_ref.shape[0]
    heads = width // HEAD_DIM
    c3 = 3 * conv_dim

    x = x_ref[...]
    next_ref[...] = _dot(_rms(xn_ref[...], gmix_ref[...]).astype(BF16), win_ref[...])

    z = proj_ref[:, 2 * conv_dim:c3] * proj_ref[:, 0:conv_dim]
    zbuf_ref[SUB:SUB + tt, :] = z
    cw = convw_ref[...]
    conv = (cw[2:3, :] * z + cw[1:2, :] * zbuf_ref[SUB - 1:SUB - 1 + tt, :]
            + cw[0:1, :] * zbuf_ref[SUB - 2:SUB - 2 + tt, :])
    y_ref[:, 0:conv_dim] = (proj_ref[:, conv_dim:2 * conv_dim] * conv).astype(BF16)
    zbuf_ref[0:SUB, :] = zbuf_ref[tt:tt + SUB, :]

    lbp = lbp_ref[...]
    lbe = jnp.exp(lbp - jnp.max(lbp, axis=0, keepdims=True))
    lb = jnp.sum(lbe[0:layer + 1, :], axis=0, keepdims=True) / jnp.sum(lbe, axis=0, keepdims=True)

    row = lax.broadcasted_iota(I32, (CHUNK, width), 0)
    row8 = lax.broadcasted_iota(I32, (SUB, width), 0)
    ti = lax.broadcasted_iota(I32, (CHUNK, CHUNK), 0)
    si = lax.broadcasted_iota(I32, (CHUNK, CHUNK), 1)
    tri = (si <= ti).astype(BF16)
    bdr = lax.broadcasted_iota(I32, (V7X_MXU_DIM, V7X_MXU_DIM), 0) // HEAD_DIM
    bdc = lax.broadcasted_iota(I32, (V7X_MXU_DIM, V7X_MXU_DIM), 1) // HEAD_DIM
    head_ones = (bdr == bdc).astype(BF16)
    gn = gn_ref[...]

    def bcast_row(val, r, n):
        return jnp.broadcast_to(val[r:r + 1, :], (n, val.shape[1]))

    def chunk(r0, slot):
        rows = slice(r0, r0 + CHUNK)
        ws_ref = w_ref.at[slot]
        q = proj_ref[rows, c3:c3 + width]
        f = proj_ref[rows, c3 + width:c3 + 2 * width]
        v = proj_ref[rows, c3 + 2 * width:c3 + 3 * width]
        og = proj_ref[rows, c3 + 3 * width:c3 + 4 * width]
        fg = lb + (1.0 - lb) * jax.nn.sigmoid(f)
        k = 1.0 - fg
        g_rest = jnp.log(fg)
        b = jnp.zeros((CHUNK, width), F32)
        for _ in range(3):
            g_part = g_rest.astype(BF16)
            b = b + _dot(tri, g_part)
            g_rest = g_rest - g_part.astype(F32)
        b_last = b[CHUNK - 1:CHUNK, :]
        v16 = v.astype(BF16)

        q_s = (q * jnp.exp(b)).astype(BF16)
        k_s = (k * jnp.exp(b_last - b)).astype(BF16)
        e_last = jnp.exp(b_last)

        p_acc = [jnp.zeros((CHUNK, CHUNK), F32) for _ in range(heads)]
        half = CHUNK // 2
        while half >= SUB:
            blk = 2 * half
            anc = jnp.concatenate(
                [bcast_row(b, i * blk + half, blk) for i in range(CHUNK // blk)], axis=0)
            upper = (row % blk) >= half
            q_l = (q * jnp.where(upper, jnp.exp(b - anc), 0.0)).astype(BF16)
            k_l = (k * jnp.where(upper, 0.0, jnp.exp(anc - b))).astype(BF16)
            same = (ti // blk) == (si // blk)
            for h in range(heads):
                hs = slice(h * HEAD_DIM, (h + 1) * HEAD_DIM)
                p_acc[h] = p_acc[h] + jnp.where(same, _dot_nt(q_l[:, hs], k_l[:, hs]), 0.0)
            half //= 2

        for rb in range(CHUNK // SUB):
            rs = slice(rb * SUB, (rb + 1) * SUB)
            b8, q8, k8 = b[rs, :], q[rs, :], k[rs, :]
            for s in range(0, SUB, 2):
                pair = []
                for s1 in (s, s + 1):
                    dec = jnp.where(row8 >= s1, jnp.exp(b8 - bcast_row(b8, s1, SUB)), 0.0)
                    pair.append(q8 * dec * bcast_row(k8, s1, SUB))
                ws_ref[(rb * SUB + s) * SUB:(rb * SUB + s + 2) * SUB, :] = (
                    jnp.concatenate(pair, axis=0).astype(BF16))
        sc_ref = s_ref.at[slot]
        for j in range(width // V7X_MXU_DIM):
            lanes = slice(j * V7X_MXU_DIM, (j + 1) * V7X_MXU_DIM)
            sc_ref[:, lanes] = _dot(ws_ref[:, lanes], head_ones)
        o_diag = []
        for rb in range(CHUNK // SUB):
            sc = sc_ref[rb * SUB * SUB:(rb + 1) * SUB * SUB, :]
            v8 = v[rb * SUB:(rb + 1) * SUB, :]
            acc = None
            for s in range(SUB):
                term = sc[s * SUB:(s + 1) * SUB, :] * bcast_row(v8, s, SUB)
                acc = term if acc is None else acc + term
            o_diag.append(acc)
        o_diag = jnp.concatenate(o_diag, axis=0)

        outs = []
        for h in range(heads):
            hs = slice(h * HEAD_DIM, (h + 1) * HEAD_DIM)
            st = st_ref[h]
            o_h = (_dot_nt(q_s[:, hs], st.astype(BF16)) + _dot(p_acc[h].astype(BF16), v16[:, hs])
                   + o_diag[:, hs])
            st_ref[h] = e_last[:, hs] * st + _dot(v16[:, hs].T, k_s[:, hs])
            o_n = o_h * lax.rsqrt(jnp.mean(o_h * o_h, axis=-1, keepdims=True) + EPS)
            outs.append(o_n)
        o = jnp.concatenate(outs, axis=1) * gn * _silu(og)
        y_ref[rows, conv_dim:conv_dim + width] = o.astype(BF16)

    for c in range(tt // CHUNK):
        chunk(c * CHUNK, c % CHUNK_SLOTS)
    o_ref[...] = x + _dot(y_ref[...], wout_ref[...])


def _mixer0_kernel(*refs, n_cast, tiles_per_seq, **static):
    x_ref, xn_ref, gmix_ref, win_ref, convw_ref, lbp_ref, gn_ref, wout_ref = refs[:8]
    cast_refs = refs[8:8 + n_cast]
    o_ref = refs[8 + n_cast]
    cast_o_refs = refs[9 + n_cast:9 + 2 * n_cast]
    proj_a_ref, proj_b_ref, zbuf_ref, st_ref, y_ref, w_ref, s_ref = refs[9 + 2 * n_cast:]
    i = pl.program_id(0)

    @pl.when(i == 0)
    def _():
        proj_a_ref[...] = _dot(_rms(x_ref[...], gmix_ref[...]).astype(BF16), win_ref[...])

    @pl.when(i % tiles_per_seq == 0)
    def _():
        st_ref[...] = jnp.zeros_like(st_ref)
        zbuf_ref[0:SUB, :] = jnp.zeros((SUB, zbuf_ref.shape[1]), F32)

    for parity, (cur, nxt) in enumerate(((proj_a_ref, proj_b_ref), (proj_b_ref, proj_a_ref))):
        @pl.when(i % 2 == parity)
        def _():
            _mixer0_tile(x_ref, xn_ref, gmix_ref, win_ref, convw_ref, lbp_ref, gn_ref, wout_ref,
                         o_ref, cur, nxt, zbuf_ref, st_ref, y_ref, w_ref, s_ref, **static)

    for src, dst in zip(cast_refs, cast_o_refs):
        dst[...] = src[...].astype(BF16)


BF16_ROWS = 16


def _cast_job(w2d, n_steps, step_of):
    rows, cols = w2d.shape
    n_blocks = max(n for n in range(1, n_steps + 1)
                   if rows % n == 0 and (rows // n) % BF16_ROWS == 0)
    spec = pl.BlockSpec((rows // n_blocks, cols),
                        lambda *ids: (jnp.minimum(step_of(*ids), n_blocks - 1), 0))
    return spec, jax.ShapeDtypeStruct(w2d.shape, BF16)


def _mixer0(x, gmix, w_in, conv_w, lbp, gnorm, w_out, layer, cast_ws, tt=256):
    bsz, seq, d = x.shape
    conv_dim = conv_w.shape[1]
    width = lbp.shape[1]
    n_in = w_in.shape[1]
    tt = min(tt, seq)
    heads = width // HEAD_DIM
    gn = jnp.tile(gnorm.reshape(1, HEAD_DIM), (1, heads))
    n_t = seq // tt
    n_tiles = bsz * n_t
    x2 = x.reshape(bsz * seq, d)
    cast_specs, cast_shapes = zip(*[_cast_job(w, n_tiles, lambda i: i) for w in cast_ws])
    kern = functools.partial(_mixer0_kernel, n_cast=len(cast_ws), tiles_per_seq=n_t, layer=layer,
                             conv_dim=conv_dim, width=width)
    out, *cast_out = pl.pallas_call(
        kern,
        grid=(n_tiles,),
        in_specs=[
            pl.BlockSpec((tt, d), lambda i: (i, 0)),
            pl.BlockSpec((tt, d), lambda i: (jnp.minimum(i + 1, n_tiles - 1), 0)),
            _const_spec((1, d)),
            _const_spec((d, n_in)),
            _const_spec(conv_w.shape),
            _const_spec(lbp.shape),
            _const_spec((1, width)),
            _const_spec(w_out.shape),
            *cast_specs,
        ],
        out_specs=[pl.BlockSpec((tt, d), lambda i: (i, 0)), *cast_specs],
        out_shape=[jax.ShapeDtypeStruct(x2.shape, F32), *cast_shapes],
        scratch_shapes=[
            pltpu.VMEM((tt, n_in), F32),
            pltpu.VMEM((tt, n_in), F32),
            pltpu.VMEM((tt + SUB, conv_dim), F32),
            pltpu.VMEM((heads, HEAD_DIM, HEAD_DIM), F32),
            pltpu.VMEM((tt, conv_dim + width), BF16),
            pltpu.VMEM((CHUNK_SLOTS, CHUNK * SUB, width), BF16),
            pltpu.VMEM((CHUNK_SLOTS, CHUNK * SUB, width), F32),
        ],
        compiler_params=_cparams(1),
        name="mixer0",
    )(x2, x2, gmix.reshape(1, d), w_in.astype(BF16), conv_w, lbp, gn, w_out.astype(BF16), *cast_ws)
    return out.reshape(x.shape), cast_out


def _ffn_kernel(x_ref, g_ref, w1_ref, w3_ref, w2_ref, cast_ref, o_ref, cast_o_ref, *, tf):
    x = x_ref[...]
    hn = _rms(x, g_ref[...]).astype(BF16)
    acc = x
    for j in range(w1_ref.shape[1] // tf):
        cols = slice(j * tf, (j + 1) * tf)
        h = _silu(_dot(hn, w1_ref[:, cols])) * _dot(hn, w3_ref[:, cols])
        acc = acc + _dot(h.astype(BF16), w2_ref[cols, :])
    o_ref[...] = acc
    cast_o_ref[...] = cast_ref[...].astype(BF16)


def _ffn(x2, g, w1, w3, w2, cast_w, tm=512, tf=256):
    n, d = x2.shape
    tm = min(tm, n)
    cast_spec, cast_shape = _cast_job(cast_w, n // tm, lambda i: i)
    return pl.pallas_call(
        functools.partial(_ffn_kernel, tf=tf),
        grid=(n // tm,),
        in_specs=[
            pl.BlockSpec((tm, d), lambda i: (i, 0)),
            _const_spec((1, d)),
            _const_spec(w1.shape),
            _const_spec(w3.shape),
            _const_spec(w2.shape),
            cast_spec,
        ],
        out_specs=[pl.BlockSpec((tm, d), lambda i: (i, 0)), cast_spec],
        out_shape=[jax.ShapeDtypeStruct((n, d), F32), cast_shape],
        compiler_params=_cparams(1),
        name="ffn0",
    )(x2, g.reshape(1, d), w1, w3, w2, cast_w)


def _conformer_kernel(x_ref, g_ref, w1_ref, b1_ref, wdw_ref, bdw_ref, lng_ref, lnb_ref,
                      w2_ref, b2_ref, cast_ref, o_ref, cast_o_ref, ubuf_ref, conv_ref, *, halo):
    tt = x_ref.shape[1]
    d = x_ref.shape[2]
    kw = wdw_ref.shape[0]

    @pl.when(pl.program_id(1) == 0)
    def _():
        ubuf_ref[0:halo, :] = jnp.zeros((halo, d), F32)
        ubuf_ref[halo + tt:halo + tt + SUB, :] = jnp.zeros((SUB, d), F32)

    x = x_ref[0]
    hn = _rms(x, g_ref[...]).astype(BF16)
    p = _dot(hn, w1_ref[...]) + b1_ref[...]
    ubuf_ref[halo:halo + tt, :] = p[:, 0:d] * jax.nn.sigmoid(p[:, d:2 * d])
    off0 = halo - (kw - 1)
    seg = min(tt, 128)
    for c in range(d // V7X_LANES):
        lanes = slice(c * V7X_LANES, (c + 1) * V7X_LANES)
        wcol = wdw_ref[:, lanes]
        for base in range(0, tt, seg):
            acc = None
            for r in range(SUB):
                part = None
                for a in range((off0 + kw - 1) // SUB + 1):
                    k = SUB * a + r - off0
                    if 0 <= k < kw:
                        win = ubuf_ref[base + SUB * a:base + SUB * a + seg + SUB, lanes]
                        term = wcol[k:k + 1, :] * win
                        part = term if part is None else part + term
                if part is not None:
                    shifted = part[r:r + seg, :]
                    acc = shifted if acc is None else acc + shifted
            conv_ref[base:base + seg, lanes] = acc + bdw_ref[:, lanes]
    acc = conv_ref[...]
    ubuf_ref[0:halo, :] = ubuf_ref[tt:tt + halo, :]
    mu = jnp.mean(acc, axis=-1, keepdims=True)
    xc = acc - mu
    var = jnp.mean(xc * xc, axis=-1, keepdims=True)
    u = _silu(xc * lax.rsqrt(var + EPS) * lng_ref[...] + lnb_ref[...])
    o_ref[0] = x + _dot(u.astype(BF16), w2_ref[...]) + b2_ref[...]
    cast_o_ref[...] = cast_ref[...].astype(BF16)


def _conformer(x, g, w_pw1, b_pw1, w_dw, b_dw, ln_g, ln_b, w_pw2, b_pw2, cast_w, tt=256):
    bsz, seq, d = x.shape
    tt = min(tt, seq)
    kw = w_dw.shape[0]
    halo = -(-(kw - 1) // SUB) * SUB
    r = lambda a: a.reshape(1, -1)
    n_t = seq // tt
    cast_spec, cast_shape = _cast_job(cast_w, bsz * n_t, lambda b, t: b * n_t + t)
    return pl.pallas_call(
        functools.partial(_conformer_kernel, halo=halo),
        grid=(bsz, seq // tt),
        in_specs=[
            pl.BlockSpec((1, tt, d), lambda b, t: (b, t, 0)),
            _const_spec((1, d)),
            _const_spec(w_pw1.shape),
            _const_spec((1, 2 * d)),
            _const_spec(w_dw.shape),
            _const_spec((1, d)),
            _const_spec((1, d)),
            _const_spec((1, d)),
            _const_spec(w_pw2.shape),
            _const_spec((1, d)),
            cast_spec,
        ],
        out_specs=[pl.BlockSpec((1, tt, d), lambda b, t: (b, t, 0)), cast_spec],
        out_shape=[jax.ShapeDtypeStruct(x.shape, F32), cast_shape],
        scratch_shapes=[pltpu.VMEM((halo + tt + SUB, d), F32), pltpu.VMEM((tt, d), F32)],
        compiler_params=_cparams(2),
        name="conformer",
    )(x, r(g), w_pw1, r(b_pw1), w_dw, r(b_dw), r(ln_g), r(ln_b), w_pw2, r(b_pw2), cast_w)


def _router_kernel(x_ref, g_ref, wr_ref, hn_ref, meta_ref, gate_ref, cnt_ref, run_ref,
                   *, n_experts):
    tm = x_ref.shape[0]

    @pl.when(pl.program_id(0) == 0)
    def _():
        run_ref[...] = jnp.zeros_like(run_ref)

    hn = _rms(x_ref[...], g_ref[...])
    _store_row_tiles(hn_ref, hn)
    wr = wr_ref[...]
    hn_hi, wr_hi = hn.astype(BF16), wr.astype(BF16)
    hn_lo = (hn - hn_hi.astype(F32)).astype(BF16)
    wr_lo = (wr - wr_hi.astype(F32)).astype(BF16)
    logits = _dot(hn_hi, wr_hi) + (_dot(hn_lo, wr_hi) + _dot(hn_hi, wr_lo))
    lane = lax.broadcasted_iota(I32, logits.shape, 1)
    neg = jnp.float32(-jnp.inf)
    big = jnp.int32(V7X_LANES)
    logits = jnp.where(lane < n_experts, logits, neg)
    m1 = jnp.max(logits, axis=-1, keepdims=True)
    i1 = jnp.min(jnp.where(logits == m1, lane, big), axis=-1, keepdims=True)
    rest = jnp.where(lane == i1, neg, logits)
    m2 = jnp.max(rest, axis=-1, keepdims=True)
    i2 = jnp.min(jnp.where(rest == m2, lane, big), axis=-1, keepdims=True)
    e2 = jnp.exp(m2 - m1)
    g1 = 1.0 / (1.0 + e2)
    g2 = e2 / (1.0 + e2)

    sel1 = lane == i1
    sel2 = lane == i2
    onehot = jnp.where(sel1 | sel2, 1.0, 0.0)
    ti = lax.broadcasted_iota(I32, (tm, tm), 0)
    si = lax.broadcasted_iota(I32, (tm, tm), 1)
    before = _dot((si < ti).astype(BF16), onehot.astype(BF16)) + run_ref[...]
    r1 = jnp.sum(jnp.where(sel1, before, 0.0), axis=-1, keepdims=True)
    r2 = jnp.sum(jnp.where(sel2, before, 0.0), axis=-1, keepdims=True)
    run_ref[...] = run_ref[...] + jnp.sum(onehot, axis=0, keepdims=True)

    meta = jnp.where(lane == 0, i1.astype(F32), jnp.where(lane == 1, i2.astype(F32),
                     jnp.where(lane == 2, r1, jnp.where(lane == 3, r2, 0.0))))
    meta_ref[...] = meta.T[0:SUB, :].astype(I32)
    col = lax.broadcasted_iota(I32, (tm, TOP_K), 1)
    gate_ref[...] = jnp.where(col == 0, g1, g2)
    cnt_ref[...] = run_ref[...].astype(I32)


def _router(x2, g, w_router, tm=512):
    n, d = x2.shape
    assert d == SUB * V7X_LANES
    tm = min(tm, n)
    n_experts = w_router.shape[1]
    wr = jnp.zeros((d, V7X_LANES), F32).at[:, :n_experts].set(w_router)
    return pl.pallas_call(
        functools.partial(_router_kernel, n_experts=n_experts),
        grid=(n // tm,),
        in_specs=[
            pl.BlockSpec((tm, d), lambda i: (i, 0)),
            _const_spec((1, d)),
            _const_spec((d, V7X_LANES)),
        ],
        out_specs=[
            pl.BlockSpec((tm * SUB, V7X_LANES), lambda i: (i, 0)),
            pl.BlockSpec((SUB, tm), lambda i: (0, i)),
            pl.BlockSpec((tm, TOP_K), lambda i: (i, 0)),
            pl.BlockSpec((1, V7X_LANES), lambda i: (0, 0)),
        ],
        out_shape=[
            jax.ShapeDtypeStruct((n * SUB, V7X_LANES), F32),
            jax.ShapeDtypeStruct((SUB, n), I32),
            jax.ShapeDtypeStruct((n, TOP_K), F32),
            jax.ShapeDtypeStruct((1, V7X_LANES), I32),
        ],
        scratch_shapes=[pltpu.VMEM((1, V7X_LANES), F32)],
        compiler_params=_cparams(1),
        name="router",
    )(x2, g.reshape(1, d), wr)


def _dispatch_kernel(slot0_ref, slot1_ref, pstart_ref, plen_ref, used_ref, hn_ref, xs_ref,
                     zero_ref, sem, zsem, *, n_experts, tile):
    tm = hn_ref.shape[0] // SUB
    n_tiles = xs_ref.shape[0] // (tile * SUB)
    slot_refs = (slot0_ref, slot1_ref)

    def row_copy(r, k):
        return pltpu.make_async_copy(
            _row_tile(hn_ref, r), _row_tile(xs_ref, slot_refs[k][r]), sem)

    _start_row_copies(tm, row_copy)
    for k in range(TOP_K):
        pltpu.make_async_copy(hn_ref, xs_ref.at[pl.ds(0, tm * SUB)], sem).wait()

    def rows(ref, start, count):
        return ref.at[pl.ds(pl.multiple_of(start * SUB, SUB), count * SUB)]

    def for_each_pad_copy(act):
        for e in range(n_experts):
            cur = pstart_ref[e]
            length = plen_ref[e]
            p = 1
            while p < tile:
                take = (length & p) != 0

                @pl.when(take)
                def _():
                    act(pltpu.make_async_copy(rows(zero_ref, 0, p), rows(xs_ref, cur, p), zsem))
                cur = cur + jnp.where(take, p, 0)
                p *= 2
        for i in range(n_experts):
            t = used_ref[0] + i

            @pl.when(t < n_tiles)
            def _():
                act(pltpu.make_async_copy(zero_ref, rows(xs_ref, t * tile, tile), zsem))

    @pl.when(pl.program_id(0) == pl.num_programs(0) - 1)
    def _():
        zero_ref[...] = jnp.zeros_like(zero_ref)
        for_each_pad_copy(lambda cp: cp.start())
        for_each_pad_copy(lambda cp: cp.wait())


def _dispatch(hn, slots, pad_start, pad_len, used_tiles, n_tiles, tile, tm=1024):
    n = hn.shape[0] // SUB
    tm = min(tm, n)
    n_experts = pad_start.shape[0]
    smem = pl.BlockSpec(memory_space=pltpu.SMEM)
    slot_spec = pl.BlockSpec((tm,), lambda i: (i,), memory_space=pltpu.SMEM)
    return pl.pallas_call(
        functools.partial(_dispatch_kernel, n_experts=n_experts, tile=tile),
        grid=(n // tm,),
        in_specs=[
            slot_spec, slot_spec,
            smem, smem, smem,
            pl.BlockSpec((tm * SUB, V7X_LANES), lambda i: (i, 0)),
        ],
        out_specs=pl.BlockSpec(memory_space=pl.ANY),
        out_shape=jax.ShapeDtypeStruct((n_tiles * tile * SUB, V7X_LANES), F32),
        scratch_shapes=[
            pltpu.VMEM((tile * SUB, V7X_LANES), F32),
            pltpu.SemaphoreType.DMA(()),
            pltpu.SemaphoreType.DMA(()),
        ],
        compiler_params=_cparams(1),
        name="dispatch",
    )(slots[0], slots[1], pad_start, pad_len, used_tiles, hn)


def _grouped_kernel(exp_ref, used_ref, nsub_ref, x_ref, w1_ref, w3_ref, w2_ref, o_ref,
                    xb_ref, acc_ref, *, sub):
    i = pl.program_id(0)
    j = pl.program_id(1)
    last = pl.num_programs(1) - 1
    valid = i < used_ref[0]
    tile = acc_ref.shape[0]

    @pl.when(valid & (j == 0))
    def _():
        for c in range(SUB):
            xb_ref[:, c * V7X_LANES:(c + 1) * V7X_LANES] = _load_row_tiles(x_ref, tile, c).astype(BF16)
        acc_ref[...] = jnp.zeros_like(acc_ref)

    @pl.when(valid)
    def _():
        def sub_tile(s, w1b, w3b, w2b):
            rows = slice(s * sub, (s + 1) * sub)
            xb = xb_ref[rows, :]
            h = _silu(_dot(xb, w1b)) * _dot(xb, w3b)
            acc_ref[rows, :] += _dot(h.astype(BF16), w2b)

        sub_tile(0, w1_ref[...], w3_ref[...], w2_ref[...])
        for s in range(1, tile // sub):
            @pl.when(s < nsub_ref[i])
            def _():
                sub_tile(s, w1_ref[...], w3_ref[...], w2_ref[...])

    @pl.when(valid & (j == last))
    def _():
        _store_row_tiles(o_ref, acc_ref[...])

    @pl.when(jnp.logical_not(valid) & (j == last))
    def _():
        o_ref[...] = jnp.zeros_like(o_ref)


def _grouped(xs, w1, w3, w2, tile_exp, used_tiles, tile_nsub, tile, sub, tf=512):
    d = w1.shape[1]
    n_tiles = xs.shape[0] // (tile * SUB)
    nj = w1.shape[2] // tf

    def jsel(i, j, used):
        return jnp.where(i < used[0], j, nj - 1)

    grid_spec = pltpu.PrefetchScalarGridSpec(
        num_scalar_prefetch=3,
        grid=(n_tiles, nj),
        in_specs=[
            pl.BlockSpec((tile * SUB, V7X_LANES), lambda i, j, ex, us, ns: (i, 0)),
            pl.BlockSpec((None, d, tf), lambda i, j, ex, us, ns: (ex[i], 0, jsel(i, j, us))),
            pl.BlockSpec((None, d, tf), lambda i, j, ex, us, ns: (ex[i], 0, jsel(i, j, us))),
            pl.BlockSpec((None, tf, d), lambda i, j, ex, us, ns: (ex[i], jsel(i, j, us), 0)),
        ],
        out_specs=pl.BlockSpec((tile * SUB, V7X_LANES), lambda i, j, ex, us, ns: (i, 0)),
        scratch_shapes=[pltpu.VMEM((tile, d), BF16), pltpu.VMEM((tile, d), F32)],
    )
    return pl.pallas_call(
        functools.partial(_grouped_kernel, sub=sub),
        grid_spec=grid_spec,
        out_shape=jax.ShapeDtypeStruct(xs.shape, F32),
        compiler_params=_cparams(2),
        name="grouped_swiglu",
    )(tile_exp, used_tiles, tile_nsub, xs, w1, w3, w2)


def _combine_kernel(slot0_ref, slot1_ref, h_ref, gate_ref, g_ref, ys_ref, o_ref, buf_ref, sem):
    tm, d = h_ref.shape
    slot_refs = (slot0_ref, slot1_ref)

    def row_copy(r, k):
        return pltpu.make_async_copy(
            _row_tile(ys_ref, slot_refs[k][r]), _row_tile(buf_ref.at[k], r), sem)

    _start_row_copies(tm, row_copy)
    for k in range(TOP_K):
        pltpu.make_async_copy(ys_ref.at[pl.ds(0, tm * SUB)], buf_ref.at[k], sem).wait()

    gate = gate_ref[...]
    ss = jnp.zeros((tm, 1), F32)
    for c in range(SUB):
        lanes = slice(c * V7X_LANES, (c + 1) * V7X_LANES)
        piece = (h_ref[:, lanes] + gate[:, 0:1] * _load_row_tiles(buf_ref.at[0], tm, c)
                 + gate[:, 1:2] * _load_row_tiles(buf_ref.at[1], tm, c))
        o_ref[:, lanes] = piece
        ss = ss + jnp.sum(piece * piece, axis=-1, keepdims=True)
    o_ref[...] = o_ref[...] * lax.rsqrt(ss / d + EPS) * g_ref[...]


def _combine(h2, gates, slots, ys, g, tm=1024):
    n, d = h2.shape
    tm = min(tm, n)
    slot_spec = pl.BlockSpec((tm,), lambda i: (i,), memory_space=pltpu.SMEM)
    return pl.pallas_call(
        _combine_kernel,
        grid=(n // tm,),
        in_specs=[
            slot_spec, slot_spec,
            pl.BlockSpec((tm, d), lambda i: (i, 0)),
            pl.BlockSpec((tm, TOP_K), lambda i: (i, 0)),
            _const_spec((1, d)),
            pl.BlockSpec(memory_space=pl.ANY),
        ],
        out_specs=pl.BlockSpec((tm, d), lambda i: (i, 0)),
        out_shape=jax.ShapeDtypeStruct((n, d), F32),
        scratch_shapes=[pltpu.VMEM((TOP_K, tm * SUB, V7X_LANES), F32),
                        pltpu.SemaphoreType.DMA(())],
        compiler_params=_cparams(1),
        name="combine",
    )(slots[0], slots[1], h2, gates, g.reshape(1, d), ys)


def _moe(h2, g_ffn, w_router, w1, w3, w2, g_final, tile=1024, sub=512):
    n, d = h2.shape
    n_experts = w_router.shape[1]
    tile = min(tile, n)
    n_tiles = TOP_K * n // tile + n_experts
    hn, meta, gates, counts = _router(h2, g_ffn, w_router)
    expert, rank = meta[0:TOP_K], meta[TOP_K:2 * TOP_K]
    counts = counts[0, :n_experts]

    tiles_per = (counts + tile - 1) // tile
    ends = jnp.cumsum(tiles_per)
    starts = ends - tiles_per
    used = ends[-1:]
    slots = rank
    for e in range(n_experts):
        slots = slots + jnp.where(expert == e, starts[e] * tile, 0)
    slots = slots.astype(I32)
    pad_start = (starts * tile + counts).astype(I32)
    pad_len = (tiles_per * tile - counts).astype(I32)
    tidx = jnp.minimum(jnp.arange(n_tiles, dtype=I32), used - 1)
    tile_exp = jnp.sum((tidx[:, None] >= ends[None, :]).astype(I32), axis=1)
    sub = min(sub, tile)
    tile_rows = jnp.clip(counts[tile_exp] - (tidx - starts[tile_exp]) * tile, 0, tile)
    tile_nsub = (tile_rows + sub - 1) // sub

    xs = _dispatch(hn, slots, pad_start, pad_len, used.astype(I32), n_tiles, tile)
    ys = _grouped(xs, w1, w3, w2, tile_exp.astype(I32), used.astype(I32), tile_nsub.astype(I32),
                  tile, sub)
    return _combine(h2, gates, slots, ys, g_final)


def kernel(x, ev_norm_mix, ev_w_in, ev_conv_w, hgrn_lower_bounds, ev_gnorm, ev_w_out,
           ev_norm_ffn, ev_ffn_w1, ev_ffn_w3, ev_ffn_w2,
           od_norm_mix, od_w_pw1, od_b_pw1, od_w_dw, od_b_dw, od_ln_g, od_ln_b,
           od_w_pw2, od_b_pw2, od_norm_ffn, od_router, od_moe_w1, od_moe_w3, od_moe_w2,
           final_norm):
    bsz, seq, d = x.shape
    assert ev_w_in.shape[0] == 1 and od_w_pw1.shape[0] == 1
    n_experts, _, n_ff = od_moe_w1.shape[1:]
    flat = lambda w: w.reshape(-1, w.shape[-1])
    h, (w1b, f1b, f3b, f2b, pw1b, pw2b) = _mixer0(
        x, ev_norm_mix[0], ev_w_in[0], ev_conv_w[0], hgrn_lower_bounds, ev_gnorm[0], ev_w_out[0],
        layer=0, cast_ws=[flat(od_moe_w1[0]), ev_ffn_w1[0], ev_ffn_w3[0], ev_ffn_w2[0],
                          od_w_pw1[0], od_w_pw2[0]])
    h, w3b = _ffn(h.reshape(bsz * seq, d), ev_norm_ffn[0], f1b, f3b, f2b,
                  cast_w=flat(od_moe_w3[0]))
    h, w2b = _conformer(h.reshape(bsz, seq, d), od_norm_mix[0], pw1b, od_b_pw1[0],
                        od_w_dw[0], od_b_dw[0], od_ln_g[0], od_ln_b[0], pw2b, od_b_pw2[0],
                        cast_w=flat(od_moe_w2[0]))
    out = _moe(h.reshape(bsz * seq, d), od_norm_ffn[0], od_router[0],
               w1b.reshape(n_experts, d, n_ff), w3b.reshape(n_experts, d, n_ff),
               w2b.reshape(n_experts, n_ff, d), final_norm)
    return out.reshape(bsz, seq, d)
```

```python
import functools

import jax
import jax.numpy as jnp
from jax import lax
from jax.experimental import pallas as pl
from jax.experimental.pallas import tpu as pltpu

F32 = jnp.float32
BF16 = jnp.bfloat16
I32 = jnp.int32

EPS = 1e-6
CHUNK = 64
CHUNK_SLOTS = 2
PROJ_PIECE = 256
SUB = 8
HEAD_DIM = 128
TOP_K = 2

V7X_LANES = 128
V7X_MXU_DIM = 256
V7X_VMEM_BYTES = 64 * 1024 * 1024
VMEM_LIMIT = V7X_VMEM_BYTES - 8 * 1024 * 1024


def _cparams(n_axes):
    return pltpu.CompilerParams(
        dimension_semantics=("arbitrary",) * n_axes, vmem_limit_bytes=VMEM_LIMIT)


def _const_spec(shape):
    nd = len(shape)
    return pl.BlockSpec(shape, lambda *_: (0,) * nd, pipeline_mode=pl.Buffered(1))


def _rms(x, g):
    return x * lax.rsqrt(jnp.mean(x * x, axis=-1, keepdims=True) + EPS) * g


def _silu(x):
    return x * jax.nn.sigmoid(x)


def _dot(a, b):
    return jnp.dot(a, b, preferred_element_type=F32)


def _dot_nt(a, b):
    return lax.dot_general(a, b, (((1,), (1,)), ((), ())), preferred_element_type=F32)


def _store_row_tiles(ref, val):
    m = val.shape[0]
    for c in range(SUB):
        ref[pl.ds(c, m, stride=SUB), :] = val[:, c * V7X_LANES:(c + 1) * V7X_LANES]


def _load_row_tiles(ref, m, c):
    return ref[pl.ds(c, m, stride=SUB), :]


def _row_tile(ref, row):
    return ref.at[pl.ds(pl.multiple_of(row * SUB, SUB), SUB)]


ROW_COPY_UNROLL = 8


def _start_row_copies(n_rows, row_copy):
    def body(i, carry):
        for u in range(ROW_COPY_UNROLL):
            for k in range(TOP_K):
                row_copy(i * ROW_COPY_UNROLL + u, k).start(priority=(u * TOP_K + k) % 2)
        return carry

    lax.fori_loop(0, n_rows // ROW_COPY_UNROLL, body, 0)


def _mixer0_tile(x_ref, xn_ref, gmix_ref, win_ref, convw_ref, lbp_ref, gn_ref, wout_ref, o_ref,
                 proj_ref, next_ref, zbuf_ref, st_ref, y_ref, w_ref, s_ref, hn_ref, *, layer,
                 conv_dim, width):
    tt = x_ref.shape[0]
    heads = width // HEAD_DIM
    c3 = 3 * conv_dim

    x = x_ref[...]
    hn_ref[...] = _rms(xn_ref[...], gmix_ref[...]).astype(BF16)
    n_in = win_ref.shape[1]
    pieces = [slice(c0, min(c0 + PROJ_PIECE, n_in)) for c0 in range(0, n_in, PROJ_PIECE)]

    def project_next(cols):
        next_ref[:, cols] = _dot(hn_ref[...], win_ref[:, cols])

    z = proj_ref[:, 2 * conv_dim:c3] * proj_ref[:, 0:conv_dim]
    zbuf_ref[SUB:SUB + tt, :] = z
    cw = convw_ref[...]
    conv = (cw[2:3, :] * z + cw[1:2, :] * zbuf_ref[SUB - 1:SUB - 1 + tt, :]
            + cw[0:1, :] * zbuf_ref[SUB - 2:SUB - 2 + tt, :])
    y_ref[:, 0:conv_dim] = (proj_ref[:, conv_dim:2 * conv_dim] * conv).astype(BF16)
    zbuf_ref[0:SUB, :] = zbuf_ref[tt:tt + SUB, :]

    lbp = lbp_ref[...]
    lbe = jnp.exp(lbp - jnp.max(lbp, axis=0, keepdims=True))
    lb = jnp.sum(lbe[0:layer + 1, :], axis=0, keepdims=True) / jnp.sum(lbe, axis=0, keepdims=True)

    row = lax.broadcasted_iota(I32, (CHUNK, width), 0)
    row8 = lax.broadcasted_iota(I32, (SUB, width), 0)
    ti = lax.broadcasted_iota(I32, (CHUNK, CHUNK), 0)
    si = lax.broadcasted_iota(I32, (CHUNK, CHUNK), 1)
    tri = (si <= ti).astype(BF16)
    bdr = lax.broadcasted_iota(I32, (V7X_MXU_DIM, V7X_MXU_DIM), 0) // HEAD_DIM
    bdc = lax.broadcasted_iota(I32, (V7X_MXU_DIM, V7X_MXU_DIM), 1) // HEAD_DIM
    head_ones = (bdr == bdc).astype(BF16)
    gn = gn_ref[...]

    def bcast_row(val, r, n):
        return jnp.broadcast_to(val[r:r + 1, :], (n, val.shape[1]))

    def chunk(r0, slot, fill):
        rows = slice(r0, r0 + CHUNK)
        ws_ref = w_ref.at[slot]
        q = proj_ref[rows, c3:c3 + width]
        f = proj_ref[rows, c3 + width:c3 + 2 * width]
        v = proj_ref[rows, c3 + 2 * width:c3 + 3 * width]
        og = proj_ref[rows, c3 + 3 * width:c3 + 4 * width]
        fg = lb + (1.0 - lb) * jax.nn.sigmoid(f)
        k = 1.0 - fg
        g_rest = jnp.log(fg)
        b = jnp.zeros((CHUNK, width), F32)
        for _ in range(3):
            g_part = g_rest.astype(BF16)
            b = b + _dot(tri, g_part)
            g_rest = g_rest - g_part.astype(F32)
        b_last = b[CHUNK - 1:CHUNK, :]
        v16 = v.astype(BF16)
        fill()

        q_s = (q * jnp.exp(b)).astype(BF16)
        k_s = (k * jnp.exp(b_last - b)).astype(BF16)
        e_last = jnp.exp(b_last)

        p_acc = [jnp.zeros((CHUNK, CHUNK), F32) for _ in range(heads)]
        half = CHUNK // 2
        while half >= SUB:
            blk = 2 * half
            anc = jnp.concatenate(
                [bcast_row(b, i * blk + half, blk) for i in range(CHUNK // blk)], axis=0)
            upper = (row % blk) >= half
            q_l = (q * jnp.where(upper, jnp.exp(b - anc), 0.0)).astype(BF16)
            k_l = (k * jnp.where(upper, 0.0, jnp.exp(anc - b))).astype(BF16)
            same = (ti // blk) == (si // blk)
            for h in range(heads):
                hs = slice(h * HEAD_DIM, (h + 1) * HEAD_DIM)
                p_acc[h] = p_acc[h] + jnp.where(same, _dot_nt(q_l[:, hs], k_l[:, hs]), 0.0)
            half //= 2
        fill()

        for rb in range(CHUNK // SUB):
            if rb == CHUNK // SUB // 2:
                fill()
            rs = slice(rb * SUB, (rb + 1) * SUB)
            b8, q8, k8 = b[rs, :], q[rs, :], k[rs, :]
            for s in range(0, SUB, 2):
                pair = []
                for s1 in (s, s + 1):
                    dec = jnp.where(row8 >= s1, jnp.exp(b8 - bcast_row(b8, s1, SUB)), 0.0)
                    pair.append(q8 * dec * bcast_row(k8, s1, SUB))
                ws_ref[(rb * SUB + s) * SUB:(rb * SUB + s + 2) * SUB, :] = (
                    jnp.concatenate(pair, axis=0).astype(BF16))
        sc_ref = s_ref.at[slot]
        for j in range(width // V7X_MXU_DIM):
            lanes = slice(j * V7X_MXU_DIM, (j + 1) * V7X_MXU_DIM)
            sc_ref[:, lanes] = _dot(ws_ref[:, lanes], head_ones)
        o_diag = []
        for rb in range(CHUNK // SUB):
            sc = sc_ref[rb * SUB * SUB:(rb + 1) * SUB * SUB, :]
            v8 = v[rb * SUB:(rb + 1) * SUB, :]
            acc = None
            for s in range(SUB):
                term = sc[s * SUB:(s + 1) * SUB, :] * bcast_row(v8, s, SUB)
                acc = term if acc is None else acc + term
            o_diag.append(acc)
        o_diag = jnp.concatenate(o_diag, axis=0)
        fill()

        outs = []
        for h in range(heads):
            hs = slice(h * HEAD_DIM, (h + 1) * HEAD_DIM)
            st = st_ref[h]
            o_h = (_dot_nt(q_s[:, hs], st.astype(BF16)) + _dot(p_acc[h].astype(BF16), v16[:, hs])
                   + o_diag[:, hs])
            st_ref[h] = e_last[:, hs] * st + _dot(v16[:, hs].T, k_s[:, hs])
            o_n = o_h * lax.rsqrt(jnp.mean(o_h * o_h, axis=-1, keepdims=True) + EPS)
            outs.append(o_n)
        o = jnp.concatenate(outs, axis=1) * gn * _silu(og)
        y_ref[rows, conv_dim:conv_dim + width] = o.astype(BF16)

    todo = iter(pieces)

    def fill():
        cols = next(todo, None)
        if cols is not None:
            project_next(cols)

    for c in range(tt // CHUNK):
        chunk(c * CHUNK, c % CHUNK_SLOTS, fill)
    for cols in todo:
        project_next(cols)
    o_ref[...] = x + _dot(y_ref[...], wout_ref[...])


def _mixer0_kernel(*refs, n_cast, tiles_per_seq, **static):
    x_ref, xn_ref, gmix_ref, win_ref, convw_ref, lbp_ref, gn_ref, wout_ref = refs[:8]
    cast_refs = refs[8:8 + n_cast]
    o_ref = refs[8 + n_cast]
    cast_o_refs = refs[9 + n_cast:9 + 2 * n_cast]
    proj_a_ref, proj_b_ref, zbuf_ref, st_ref, y_ref, w_ref, s_ref, hn_ref = refs[9 + 2 * n_cast:]
    i = pl.program_id(0)

    @pl.when(i == 0)
    def _():
        proj_a_ref[...] = _dot(_rms(x_ref[...], gmix_ref[...]).astype(BF16), win_ref[...])

    @pl.when(i % tiles_per_seq == 0)
    def _():
        st_ref[...] = jnp.zeros_like(st_ref)
        zbuf_ref[0:SUB, :] = jnp.zeros((SUB, zbuf_ref.shape[1]), F32)

    for parity, (cur, nxt) in enumerate(((proj_a_ref, proj_b_ref), (proj_b_ref, proj_a_ref))):
        @pl.when(i % 2 == parity)
        def _():
            _mixer0_tile(x_ref, xn_ref, gmix_ref, win_ref, convw_ref, lbp_ref, gn_ref, wout_ref,
                         o_ref, cur, nxt, zbuf_ref, st_ref, y_ref, w_ref, s_ref, hn_ref, **static)

    for src, dst in zip(cast_refs, cast_o_refs):
        dst[...] = src[...].astype(BF16)


BF16_ROWS = 16


def _cast_job(w2d, n_steps, step_of):
    rows, cols = w2d.shape
    n_blocks = max(n for n in range(1, n_steps + 1)
                   if rows % n == 0 and (rows // n) % BF16_ROWS == 0)
    spec = pl.BlockSpec((rows // n_blocks, cols),
                        lambda *ids: (jnp.minimum(step_of(*ids), n_blocks - 1), 0))
    return spec, jax.ShapeDtypeStruct(w2d.shape, BF16)


def _mixer0(x, gmix, w_in, conv_w, lbp, gnorm, w_out, layer, cast_ws, tt=256):
    bsz, seq, d = x.shape
    conv_dim = conv_w.shape[1]
    width = lbp.shape[1]
    n_in = w_in.shape[1]
    tt = min(tt, seq)
    heads = width // HEAD_DIM
    gn = jnp.tile(gnorm.reshape(1, HEAD_DIM), (1, heads))
    n_t = seq // tt
    n_tiles = bsz * n_t
    x2 = x.reshape(bsz * seq, d)
    cast_specs, cast_shapes = zip(*[_cast_job(w, n_tiles, lambda i: i) for w in cast_ws])
    kern = functools.partial(_mixer0_kernel, n_cast=len(cast_ws), tiles_per_seq=n_t, layer=layer,
                             conv_dim=conv_dim, width=width)
    out, *cast_out = pl.pallas_call(
        kern,
        grid=(n_tiles,),
        in_specs=[
            pl.BlockSpec((tt, d), lambda i: (i, 0)),
            pl.BlockSpec((tt, d), lambda i: (jnp.minimum(i + 1, n_tiles - 1), 0)),
            _const_spec((1, d)),
            _const_spec((d, n_in)),
            _const_spec(conv_w.shape),
            _const_spec(lbp.shape),
            _const_spec((1, width)),
            _const_spec(w_out.shape),
            *cast_specs,
        ],
        out_specs=[pl.BlockSpec((tt, d), lambda i: (i, 0)), *cast_specs],
        out_shape=[jax.ShapeDtypeStruct(x2.shape, F32), *cast_shapes],
        scratch_shapes=[
            pltpu.VMEM((tt, n_in), F32),
            pltpu.VMEM((tt, n_in), F32),
            pltpu.VMEM((tt + SUB, conv_dim), F32),
            pltpu.VMEM((heads, HEAD_DIM, HEAD_DIM), F32),
            pltpu.VMEM((tt, conv_dim + width), BF16),
            pltpu.VMEM((CHUNK_SLOTS, CHUNK * SUB, width), BF16),
            pltpu.VMEM((CHUNK_SLOTS, CHUNK * SUB, width), F32),
            pltpu.VMEM((tt, d), BF16),
        ],
        compiler_params=_cparams(1),
        name="mixer0",
    )(x2, x2, gmix.reshape(1, d), w_in.astype(BF16), conv_w, lbp, gn, w_out.astype(BF16), *cast_ws)
    return out.reshape(x.shape), cast_out


def _ffn_kernel(x_ref, g_ref, w1_ref, w3_ref, w2_ref, cast_ref, o_ref, cast_o_ref, *, tf):
    x = x_ref[...]
    hn = _rms(x, g_ref[...]).astype(BF16)
    acc = x
    for j in range(w1_ref.shape[1] // tf):
        cols = slice(j * tf, (j + 1) * tf)
        h = _silu(_dot(hn, w1_ref[:, cols])) * _dot(hn, w3_ref[:, cols])
        acc = acc + _dot(h.astype(BF16), w2_ref[cols, :])
    o_ref[...] = acc
    cast_o_ref[...] = cast_ref[...].astype(BF16)


def _ffn(x2, g, w1, w3, w2, cast_w, tm=512, tf=256):
    n, d = x2.shape
    tm = min(tm, n)
    cast_spec, cast_shape = _cast_job(cast_w, n // tm, lambda i: i)
    return pl.pallas_call(
        functools.partial(_ffn_kernel, tf=tf),
        grid=(n // tm,),
        in_specs=[
            pl.BlockSpec((tm, d), lambda i: (i, 0)),
            _const_spec((1, d)),
            _const_spec(w1.shape),
            _const_spec(w3.shape),
            _const_spec(w2.shape),
            cast_spec,
        ],
        out_specs=[pl.BlockSpec((tm, d), lambda i: (i, 0)), cast_spec],
        out_shape=[jax.ShapeDtypeStruct((n, d), F32), cast_shape],
        compiler_params=_cparams(1),
        name="ffn0",
    )(x2, g.reshape(1, d), w1, w3, w2, cast_w)


def _conformer_kernel(x_ref, g_ref, w1_ref, b1_ref, wdw_ref, bdw_ref, lng_ref, lnb_ref,
                      w2_ref, b2_ref, cast_ref, o_ref, cast_o_ref, ubuf_ref, conv_ref, *, halo):
    tt = x_ref.shape[1]
    d = x_ref.shape[2]
    kw = wdw_ref.shape[0]

    @pl.when(pl.program_id(1) == 0)
    def _():
        ubuf_ref[0:halo, :] = jnp.zeros((halo, d), F32)
        ubuf_ref[halo + tt:halo + tt + SUB, :] = jnp.zeros((SUB, d), F32)

    x = x_ref[0]
    hn = _rms(x, g_ref[...]).astype(BF16)
    p = _dot(hn, w1_ref[...]) + b1_ref[...]
    ubuf_ref[halo:halo + tt, :] = p[:, 0:d] * jax.nn.sigmoid(p[:, d:2 * d])
    off0 = halo - (kw - 1)
    seg = min(tt, 128)
    for c in range(d // V7X_LANES):
        lanes = slice(c * V7X_LANES, (c + 1) * V7X_LANES)
        wcol = wdw_ref[:, lanes]
        for base in range(0, tt, seg):
            acc = None
            for r in range(SUB):
                part = None
                for a in range((off0 + kw - 1) // SUB + 1):
                    k = SUB * a + r - off0
                    if 0 <= k < kw:
                        win = ubuf_ref[base + SUB * a:base + SUB * a + seg + SUB, lanes]
                        term = wcol[k:k + 1, :] * win
                        part = term if part is None else part + term
                if part is not None:
                    shifted = part[r:r + seg, :]
                    acc = shifted if acc is None else acc + shifted
            conv_ref[base:base + seg, lanes] = acc + bdw_ref[:, lanes]
    acc = conv_ref[...]
    ubuf_ref[0:halo, :] = ubuf_ref[tt:tt + halo, :]
    mu = jnp.mean(acc, axis=-1, keepdims=True)
    xc = acc - mu
    var = jnp.mean(xc * xc, axis=-1, keepdims=True)
    u = _silu(xc * lax.rsqrt(var + EPS) * lng_ref[...] + lnb_ref[...])
    o_ref[0] = x + _dot(u.astype(BF16), w2_ref[...]) + b2_ref[...]
    cast_o_ref[...] = cast_ref[...].astype(BF16)


def _conformer(x, g, w_pw1, b_pw1, w_dw, b_dw, ln_g, ln_b, w_pw2, b_pw2, cast_w, tt=256):
    bsz, seq, d = x.shape
    tt = min(tt, seq)
    kw = w_dw.shape[0]
    halo = -(-(kw - 1) // SUB) * SUB
    r = lambda a: a.reshape(1, -1)
    n_t = seq // tt
    cast_spec, cast_shape = _cast_job(cast_w, bsz * n_t, lambda b, t: b * n_t + t)
    return pl.pallas_call(
        functools.partial(_conformer_kernel, halo=halo),
        grid=(bsz, seq // tt),
        in_specs=[
            pl.BlockSpec((1, tt, d), lambda b, t: (b, t, 0)),
            _const_spec((1, d)),
            _const_spec(w_pw1.shape),
            _const_spec((1, 2 * d)),
            _const_spec(w_dw.shape),
            _const_spec((1, d)),
            _const_spec((1, d)),
            _const_spec((1, d)),
            _const_spec(w_pw2.shape),
            _const_spec((1, d)),
            cast_spec,
        ],
        out_specs=[pl.BlockSpec((1, tt, d), lambda b, t: (b, t, 0)), cast_spec],
        out_shape=[jax.ShapeDtypeStruct(x.shape, F32), cast_shape],
        scratch_shapes=[pltpu.VMEM((halo + tt + SUB, d), F32), pltpu.VMEM((tt, d), F32)],
        compiler_params=_cparams(2),
        name="conformer",
    )(x, r(g), w_pw1, r(b_pw1), w_dw, r(b_dw), r(ln_g), r(ln_b), w_pw2, r(b_pw2), cast_w)


def _router_kernel(x_ref, g_ref, wr_ref, hn_ref, meta_ref, gate_ref, cnt_ref, run_ref,
                   *, n_experts):
    tm = x_ref.shape[0]

    @pl.when(pl.program_id(0) == 0)
    def _():
        run_ref[...] = jnp.zeros_like(run_ref)

    hn = _rms(x_ref[...], g_ref[...])
    _store_row_tiles(hn_ref, hn)
    wr = wr_ref[...]
    hn_hi, wr_hi = hn.astype(BF16), wr.astype(BF16)
    hn_lo = (hn - hn_hi.astype(F32)).astype(BF16)
    wr_lo = (wr - wr_hi.astype(F32)).astype(BF16)
    logits = _dot(hn_hi, wr_hi) + (_dot(hn_lo, wr_hi) + _dot(hn_hi, wr_lo))
    lane = lax.broadcasted_iota(I32, logits.shape, 1)
    neg = jnp.float32(-jnp.inf)
    big = jnp.int32(V7X_LANES)
    logits = jnp.where(lane < n_experts, logits, neg)
    m1 = jnp.max(logits, axis=-1, keepdims=True)
    i1 = jnp.min(jnp.where(logits == m1, lane, big), axis=-1, keepdims=True)
    rest = jnp.where(lane == i1, neg, logits)
    m2 = jnp.max(rest, axis=-1, keepdims=True)
    i2 = jnp.min(jnp.where(rest == m2, lane, big), axis=-1, keepdims=True)
    e2 = jnp.exp(m2 - m1)
    g1 = 1.0 / (1.0 + e2)
    g2 = e2 / (1.0 + e2)

    sel1 = lane == i1
    sel2 = lane == i2
    onehot = jnp.where(sel1 | sel2, 1.0, 0.0)
    ti = lax.broadcasted_iota(I32, (tm, tm), 0)
    si = lax.broadcasted_iota(I32, (tm, tm), 1)
    before = _dot((si < ti).astype(BF16), onehot.astype(BF16)) + run_ref[...]
    r1 = jnp.sum(jnp.where(sel1, before, 0.0), axis=-1, keepdims=True)
    r2 = jnp.sum(jnp.where(sel2, before, 0.0), axis=-1, keepdims=True)
    run_ref[...] = run_ref[...] + jnp.sum(onehot, axis=0, keepdims=True)

    meta = jnp.where(lane == 0, i1.astype(F32), jnp.where(lane == 1, i2.astype(F32),
                     jnp.where(lane == 2, r1, jnp.where(lane == 3, r2, 0.0))))
    meta_ref[...] = meta.T[0:SUB, :].astype(I32)
    col = lax.broadcasted_iota(I32, (tm, TOP_K), 1)
    gate_ref[...] = jnp.where(col == 0, g1, g2)
    cnt_ref[...] = run_ref[...].astype(I32)


def _router(x2, g, w_router, tm=512):
    n, d = x2.shape
    assert d == SUB * V7X_LANES
    tm = min(tm, n)
    n_experts = w_router.shape[1]
    wr = jnp.zeros((d, V7X_LANES), F32).at[:, :n_experts].set(w_router)
    return pl.pallas_call(
        functools.partial(_router_kernel, n_experts=n_experts),
        grid=(n // tm,),
        in_specs=[
            pl.BlockSpec((tm, d), lambda i: (i, 0)),
            _const_spec((1, d)),
            _const_spec((d, V7X_LANES)),
        ],
        out_specs=[
            pl.BlockSpec((tm * SUB, V7X_LANES), lambda i: (i, 0)),
            pl.BlockSpec((SUB, tm), lambda i: (0, i)),
            pl.BlockSpec((tm, TOP_K), lambda i: (i, 0)),
            pl.BlockSpec((1, V7X_LANES), lambda i: (0, 0)),
        ],
        out_shape=[
            jax.ShapeDtypeStruct((n * SUB, V7X_LANES), F32),
            jax.ShapeDtypeStruct((SUB, n), I32),
            jax.ShapeDtypeStruct((n, TOP_K), F32),
            jax.ShapeDtypeStruct((1, V7X_LANES), I32),
        ],
        scratch_shapes=[pltpu.VMEM((1, V7X_LANES), F32)],
        compiler_params=_cparams(1),
        name="router",
    )(x2, g.reshape(1, d), wr)


def _dispatch_kernel(slot0_ref, slot1_ref, pstart_ref, plen_ref, used_ref, hn_ref, xs_ref,
                     zero_ref, sem, zsem, *, n_experts, tile):
    tm = hn_ref.shape[0] // SUB
    n_tiles = xs_ref.shape[0] // (tile * SUB)
    slot_refs = (slot0_ref, slot1_ref)

    def row_copy(r, k):
        return pltpu.make_async_copy(
            _row_tile(hn_ref, r), _row_tile(xs_ref, slot_refs[k][r]), sem)

    _start_row_copies(tm, row_copy)
    for k in range(TOP_K):
        pltpu.make_async_copy(hn_ref, xs_ref.at[pl.ds(0, tm * SUB)], sem).wait()

    def rows(ref, start, count):
        return ref.at[pl.ds(pl.multiple_of(start * SUB, SUB), count * SUB)]

    def for_each_pad_copy(act):
        for e in range(n_experts):
            cur = pstart_ref[e]
            length = plen_ref[e]
            p = 1
            while p < tile:
                take = (length & p) != 0

                @pl.when(take)
                def _():
                    act(pltpu.make_async_copy(rows(zero_ref, 0, p), rows(xs_ref, cur, p), zsem))
                cur = cur + jnp.where(take, p, 0)
                p *= 2
        for i in range(n_experts):
            t = used_ref[0] + i

            @pl.when(t < n_tiles)
            def _():
                act(pltpu.make_async_copy(zero_ref, rows(xs_ref, t * tile, tile), zsem))

    @pl.when(pl.program_id(0) == pl.num_programs(0) - 1)
    def _():
        zero_ref[...] = jnp.zeros_like(zero_ref)
        for_each_pad_copy(lambda cp: cp.start())
        for_each_pad_copy(lambda cp: cp.wait())


def _dispatch(hn, slots, pad_start, pad_len, used_tiles, n_tiles, tile, tm=1024):
    n = hn.shape[0] // SUB
    tm = min(tm, n)
    n_experts = pad_start.shape[0]
    smem = pl.BlockSpec(memory_space=pltpu.SMEM)
    slot_spec = pl.BlockSpec((tm,), lambda i: (i,), memory_space=pltpu.SMEM)
    return pl.pallas_call(
        functools.partial(_dispatch_kernel, n_experts=n_experts, tile=tile),
        grid=(n // tm,),
        in_specs=[
            slot_spec, slot_spec,
            smem, smem, smem,
            pl.BlockSpec((tm * SUB, V7X_LANES), lambda i: (i, 0)),
        ],
        out_specs=pl.BlockSpec(memory_space=pl.ANY),
        out_shape=jax.ShapeDtypeStruct((n_tiles * tile * SUB, V7X_LANES), F32),
        scratch_shapes=[
            pltpu.VMEM((tile * SUB, V7X_LANES), F32),
            pltpu.SemaphoreType.DMA(()),
            pltpu.SemaphoreType.DMA(()),
        ],
        compiler_params=_cparams(1),
        name="dispatch",
    )(slots[0], slots[1], pad_start, pad_len, used_tiles, hn)


def _grouped_kernel(exp_ref, used_ref, nsub_ref, x_ref, w1_ref, w3_ref, w2_ref, o_ref,
                    xb_ref, acc_ref, *, sub):
    i = pl.program_id(0)
    j = pl.program_id(1)
    last = pl.num_programs(1) - 1
    valid = i < used_ref[0]
    tile = acc_ref.shape[0]

    @pl.when(valid & (j == 0))
    def _():
        for c in range(SUB):
            xb_ref[:, c * V7X_LANES:(c + 1) * V7X_LANES] = _load_row_tiles(x_ref, tile, c).astype(BF16)
        acc_ref[...] = jnp.zeros_like(acc_ref)

    @pl.when(valid)
    def _():
        def swiglu_rows(n_rows):
            rows = slice(0, n_rows)
            xb = xb_ref[rows, :]
            h = _silu(_dot(xb, w1_ref[...])) * _dot(xb, w3_ref[...])
            acc_ref[rows, :] += _dot(h.astype(BF16), w2_ref[...])

        for n_sub in range(1, tile // sub + 1):
            @pl.when(nsub_ref[i] == n_sub)
            def _():
                swiglu_rows(n_sub * sub)

    @pl.when(valid & (j == last))
    def _():
        _store_row_tiles(o_ref, acc_ref[...])

    @pl.when(jnp.logical_not(valid) & (j == last))
    def _():
        o_ref[...] = jnp.zeros_like(o_ref)


def _grouped(xs, w1, w3, w2, tile_exp, used_tiles, tile_nsub, tile, sub, tf=512):
    d = w1.shape[1]
    n_tiles = xs.shape[0] // (tile * SUB)
    nj = w1.shape[2] // tf

    def jsel(i, j, used):
        return jnp.where(i < used[0], j, nj - 1)

    grid_spec = pltpu.PrefetchScalarGridSpec(
        num_scalar_prefetch=3,
        grid=(n_tiles, nj),
        in_specs=[
            pl.BlockSpec((tile * SUB, V7X_LANES), lambda i, j, ex, us, ns: (i, 0)),
            pl.BlockSpec((None, d, tf), lambda i, j, ex, us, ns: (ex[i], 0, jsel(i, j, us))),
            pl.BlockSpec((None, d, tf), lambda i, j, ex, us, ns: (ex[i], 0, jsel(i, j, us))),
            pl.BlockSpec((None, tf, d), lambda i, j, ex, us, ns: (ex[i], jsel(i, j, us), 0)),
        ],
        out_specs=pl.BlockSpec((tile * SUB, V7X_LANES), lambda i, j, ex, us, ns: (i, 0)),
        scratch_shapes=[pltpu.VMEM((tile, d), BF16), pltpu.VMEM((tile, d), F32)],
    )
    return pl.pallas_call(
        functools.partial(_grouped_kernel, sub=sub),
        grid_spec=grid_spec,
        out_shape=jax.ShapeDtypeStruct(xs.shape, F32),
        compiler_params=_cparams(2),
        name="grouped_swiglu",
    )(tile_exp, used_tiles, tile_nsub, xs, w1, w3, w2)


def _combine_kernel(slot0_ref, slot1_ref, h_ref, gate_ref, g_ref, ys_ref, o_ref, buf_ref, sem):
    tm, d = h_ref.shape
    slot_refs = (slot0_ref, slot1_ref)

    def row_copy(r, k):
        return pltpu.make_async_copy(
            _row_tile(ys_ref, slot_refs[k][r]), _row_tile(buf_ref.at[k], r), sem)

    _start_row_copies(tm, row_copy)
    for k in range(TOP_K):
        pltpu.make_async_copy(ys_ref.at[pl.ds(0, tm * SUB)], buf_ref.at[k], sem).wait()

    gate = gate_ref[...]
    ss = jnp.zeros((tm, 1), F32)
    for c in range(SUB):
        lanes = slice(c * V7X_LANES, (c + 1) * V7X_LANES)
        piece = (h_ref[:, lanes] + gate[:, 0:1] * _load_row_tiles(buf_ref.at[0], tm, c)
                 + gate[:, 1:2] * _load_row_tiles(buf_ref.at[1], tm, c))
        o_ref[:, lanes] = piece
        ss = ss + jnp.sum(piece * piece, axis=-1, keepdims=True)
    o_ref[...] = o_ref[...] * lax.rsqrt(ss / d + EPS) * g_ref[...]


def _combine(h2, gates, slots, ys, g, tm=1024):
    n, d = h2.shape
    tm = min(tm, n)
    slot_spec = pl.BlockSpec((tm,), lambda i: (i,), memory_space=pltpu.SMEM)
    return pl.pallas_call(
        _combine_kernel,
        grid=(n // tm,),
        in_specs=[
            slot_spec, slot_spec,
            pl.BlockSpec((tm, d), lambda i: (i, 0)),
            pl.BlockSpec((tm, TOP_K), lambda i: (i, 0)),
            _const_spec((1, d)),
            pl.BlockSpec(memory_space=pl.ANY),
        ],
        out_specs=pl.BlockSpec((tm, d), lambda i: (i, 0)),
        out_shape=jax.ShapeDtypeStruct((n, d), F32),
        scratch_shapes=[pltpu.VMEM((TOP_K, tm * SUB, V7X_LANES), F32),
                        pltpu.SemaphoreType.DMA(())],
        compiler_params=_cparams(1),
        name="combine",
    )(slots[0], slots[1], h2, gates, g.reshape(1, d), ys)


def _moe(h2, g_ffn, w_router, w1, w3, w2, g_final, tile=1024, sub=256):
    n, d = h2.shape
    n_experts = w_router.shape[1]
    tile = min(tile, n)
    n_tiles = TOP_K * n // tile + n_experts
    hn, meta, gates, counts = _router(h2, g_ffn, w_router)
    expert, rank = meta[0:TOP_K], meta[TOP_K:2 * TOP_K]
    counts = counts[0, :n_experts]

    tiles_per = (counts + tile - 1) // tile
    ends = jnp.cumsum(tiles_per)
    starts = ends - tiles_per
    used = ends[-1:]
    slots = rank
    for e in range(n_experts):
        slots = slots + jnp.where(expert == e, starts[e] * tile, 0)
    slots = slots.astype(I32)
    pad_start = (starts * tile + counts).astype(I32)
    pad_len = (tiles_per * tile - counts).astype(I32)
    tidx = jnp.minimum(jnp.arange(n_tiles, dtype=I32), used - 1)
    tile_exp = jnp.sum((tidx[:, None] >= ends[None, :]).astype(I32), axis=1)
    sub = min(sub, tile)
    tile_rows = jnp.clip(counts[tile_exp] - (tidx - starts[tile_exp]) * tile, 0, tile)
    tile_nsub = (tile_rows + sub - 1) // sub

    xs = _dispatch(hn, slots, pad_start, pad_len, used.astype(I32), n_tiles, tile)
    ys = _grouped(xs, w1, w3, w2, tile_exp.astype(I32), used.astype(I32), tile_nsub.astype(I32),
                  tile, sub)
    return _combine(h2, gates, slots, ys, g_final)


def kernel(x, ev_norm_mix, ev_w_in, ev_conv_w, hgrn_lower_bounds, ev_gnorm, ev_w_out,
           ev_norm_ffn, ev_ffn_w1, ev_ffn_w3, ev_ffn_w2,
           od_norm_mix, od_w_pw1, od_b_pw1, od_w_dw, od_b_dw, od_ln_g, od_ln_b,
           od_w_pw2, od_b_pw2, od_norm_ffn, od_router, od_moe_w1, od_moe_w3, od_moe_w2,
           final_norm):
    bsz, seq, d = x.shape
    assert ev_w_in.shape[0] == 1 and od_w_pw1.shape[0] == 1
    n_experts, _, n_ff = od_moe_w1.shape[1:]
    flat = lambda w: w.reshape(-1, w.shape[-1])
    h, (w1b, f1b, f3b, f2b, pw1b, pw2b) = _mixer0(
        x, ev_norm_mix[0], ev_w_in[0], ev_conv_w[0], hgrn_lower_bounds, ev_gnorm[0], ev_w_out[0],
        layer=0, cast_ws=[flat(od_moe_w1[0]), ev_ffn_w1[0], ev_ffn_w3[0], ev_ffn_w2[0],
                          od_w_pw1[0], od_w_pw2[0]])
    h, w3b = _ffn(h.reshape(bsz * seq, d), ev_norm_ffn[0], f1b, f3b, f2b,
                  cast_w=flat(od_moe_w3[0]))
    h, w2b = _conformer(h.reshape(bsz, seq, d), od_norm_mix[0], pw1b, od_b_pw1[0],
                        od_w_dw[0], od_b_dw[0], od_ln_g[0], od_ln_b[0], pw2b, od_b_pw2[0],
                        cast_w=flat(od_moe_w2[0]))
    out = _moe(h.reshape(bsz * seq, d), od_norm_ffn[0], od_router[0],
               w1b.reshape(n_experts, d, n_ff), w3b.reshape(n_experts, d, n_ff),
               w2b.reshape(n_experts, n_ff, d), final_norm)
    return out.reshape(bsz, seq, d)
```

```python
import functools

import jax
import jax.numpy as jnp
from jax import lax
from jax.experimental import pallas as pl
from jax.experimental.pallas import tpu as pltpu

F32 = jnp.float32
BF16 = jnp.bfloat16
I32 = jnp.int32

EPS = 1e-6
CHUNK = 64
CHUNK_SLOTS = 2
PROJ_PIECE = 256
SUB = 8
HEAD_DIM = 128
TOP_K = 2

V7X_LANES = 128
V7X_MXU_DIM = 256
V7X_VMEM_BYTES = 64 * 1024 * 1024
VMEM_LIMIT = V7X_VMEM_BYTES - 8 * 1024 * 1024


def _cparams(n_axes):
    return pltpu.CompilerParams(
        dimension_semantics=("arbitrary",) * n_axes, vmem_limit_bytes=VMEM_LIMIT)


def _const_spec(shape):
    nd = len(shape)
    return pl.BlockSpec(shape, lambda *_: (0,) * nd, pipeline_mode=pl.Buffered(1))


def _rms(x, g):
    return x * lax.rsqrt(jnp.mean(x * x, axis=-1, keepdims=True) + EPS) * g


def _silu(x):
    return x * jax.nn.sigmoid(x)


def _dot(a, b):
    return jnp.dot(a, b, preferred_element_type=F32)


def _dot_nt(a, b):
    return lax.dot_general(a, b, (((1,), (1,)), ((), ())), preferred_element_type=F32)


def _store_row_tiles(ref, val):
    m = val.shape[0]
    for c in range(SUB):
        ref[pl.ds(c, m, stride=SUB), :] = val[:, c * V7X_LANES:(c + 1) * V7X_LANES]


def _load_row_tiles(ref, m, c):
    return ref[pl.ds(c, m, stride=SUB), :]


def _row_tile(ref, row):
    return ref.at[pl.ds(pl.multiple_of(row * SUB, SUB), SUB)]


ROW_COPY_UNROLL = 8


def _start_row_copies(n_rows, row_copy):
    def body(i, carry):
        for u in range(ROW_COPY_UNROLL):
            for k in range(TOP_K):
                row_copy(i * ROW_COPY_UNROLL + u, k).start(priority=(u * TOP_K + k) % 2)
        return carry

    lax.fori_loop(0, n_rows // ROW_COPY_UNROLL, body, 0)


def _mixer0_tile(x_ref, xn_ref, gmix_ref, win_ref, convw_ref, lbp_ref, gn_ref, wout_ref, o_ref,
                 proj_ref, next_ref, zbuf_ref, st_ref, y_ref, w_ref, s_ref, hn_ref, *, layer,
                 conv_dim, width):
    tt = x_ref.shape[0]
    heads = width // HEAD_DIM
    c3 = 3 * conv_dim

    x = x_ref[...]
    hn_ref[...] = _rms(xn_ref[...], gmix_ref[...]).astype(BF16)
    n_in = win_ref.shape[1]
    pieces = [slice(c0, min(c0 + PROJ_PIECE, n_in)) for c0 in range(0, n_in, PROJ_PIECE)]

    def project_next(cols):
        next_ref[:, cols] = _dot(hn_ref[...], win_ref[:, cols])

    z = proj_ref[:, 2 * conv_dim:c3] * proj_ref[:, 0:conv_dim]
    zbuf_ref[SUB:SUB + tt, :] = z
    cw = convw_ref[...]
    conv = (cw[2:3, :] * z + cw[1:2, :] * zbuf_ref[SUB - 1:SUB - 1 + tt, :]
            + cw[0:1, :] * zbuf_ref[SUB - 2:SUB - 2 + tt, :])
    y_ref[:, 0:conv_dim] = (proj_ref[:, conv_dim:2 * conv_dim] * conv).astype(BF16)
    zbuf_ref[0:SUB, :] = zbuf_ref[tt:tt + SUB, :]

    lbp = lbp_ref[...]
    lbe = jnp.exp(lbp - jnp.max(lbp, axis=0, keepdims=True))
    lb = jnp.sum(lbe[0:layer + 1, :], axis=0, keepdims=True) / jnp.sum(lbe, axis=0, keepdims=True)

    row = lax.broadcasted_iota(I32, (CHUNK, width), 0)
    row8 = lax.broadcasted_iota(I32, (SUB, width), 0)
    ti = lax.broadcasted_iota(I32, (CHUNK, CHUNK), 0)
    si = lax.broadcasted_iota(I32, (CHUNK, CHUNK), 1)
    tri = (si <= ti).astype(BF16)
    bdr = lax.broadcasted_iota(I32, (V7X_MXU_DIM, V7X_MXU_DIM), 0) // HEAD_DIM
    bdc = lax.broadcasted_iota(I32, (V7X_MXU_DIM, V7X_MXU_DIM), 1) // HEAD_DIM
    head_ones = (bdr == bdc).astype(BF16)
    gn = gn_ref[...]

    def bcast_row(val, r, n):
        return jnp.broadcast_to(val[r:r + 1, :], (n, val.shape[1]))

    def chunk(r0, slot, fill):
        rows = slice(r0, r0 + CHUNK)
        ws_ref = w_ref.at[slot]
        q = proj_ref[rows, c3:c3 + width]
        f = proj_ref[rows, c3 + width:c3 + 2 * width]
        v = proj_ref[rows, c3 + 2 * width:c3 + 3 * width]
        og = proj_ref[rows, c3 + 3 * width:c3 + 4 * width]
        fg = lb + (1.0 - lb) * jax.nn.sigmoid(f)
        k = 1.0 - fg
        g_rest = jnp.log(fg)
        b = jnp.zeros((CHUNK, width), F32)
        for _ in range(3):
            g_part = g_rest.astype(BF16)
            b = b + _dot(tri, g_part)
            g_rest = g_rest - g_part.astype(F32)
        b_last = b[CHUNK - 1:CHUNK, :]
        v16 = v.astype(BF16)
        fill()

        q_s = (q * jnp.exp(b)).astype(BF16)
        k_s = (k * jnp.exp(b_last - b)).astype(BF16)
        e_last = jnp.exp(b_last)

        p_acc = [jnp.zeros((CHUNK, CHUNK), F32) for _ in range(heads)]
        half = CHUNK // 2
        while half >= SUB:
            blk = 2 * half
            anc = jnp.concatenate(
                [bcast_row(b, i * blk + half, blk) for i in range(CHUNK // blk)], axis=0)
            upper = (row % blk) >= half
            q_l = (q * jnp.where(upper, jnp.exp(b - anc), 0.0)).astype(BF16)
            k_l = (k * jnp.where(upper, 0.0, jnp.exp(anc - b))).astype(BF16)
            same = (ti // blk) == (si // blk)
            for h in range(heads):
                hs = slice(h * HEAD_DIM, (h + 1) * HEAD_DIM)
                p_acc[h] = p_acc[h] + jnp.where(same, _dot_nt(q_l[:, hs], k_l[:, hs]), 0.0)
            half //= 2
        fill()

        for rb in range(CHUNK // SUB):
            if rb == CHUNK // SUB // 2:
                fill()
            rs = slice(rb * SUB, (rb + 1) * SUB)
            b8, q8, k8 = b[rs, :], q[rs, :], k[rs, :]
            for s in range(0, SUB, 2):
                pair = []
                for s1 in (s, s + 1):
                    dec = jnp.where(row8 >= s1, jnp.exp(b8 - bcast_row(b8, s1, SUB)), 0.0)
                    pair.append(q8 * dec * bcast_row(k8, s1, SUB))
                ws_ref[(rb * SUB + s) * SUB:(rb * SUB + s + 2) * SUB, :] = (
                    jnp.concatenate(pair, axis=0).astype(BF16))
        sc_ref = s_ref.at[slot]
        for j in range(width // V7X_MXU_DIM):
            lanes = slice(j * V7X_MXU_DIM, (j + 1) * V7X_MXU_DIM)
            sc_ref[:, lanes] = _dot(ws_ref[:, lanes], head_ones)
        o_diag = []
        for rb in range(CHUNK // SUB):
            sc = sc_ref[rb * SUB * SUB:(rb + 1) * SUB * SUB, :]
            v8 = v[rb * SUB:(rb + 1) * SUB, :]
            acc = None
            for s in range(SUB):
                term = sc[s * SUB:(s + 1) * SUB, :] * bcast_row(v8, s, SUB)
                acc = term if acc is None else acc + term
            o_diag.append(acc)
        o_diag = jnp.concatenate(o_diag, axis=0)
        fill()

        outs = []
        for h in range(heads):
            hs = slice(h * HEAD_DIM, (h + 1) * HEAD_DIM)
            st = st_ref[h]
            o_h = (_dot_nt(q_s[:, hs], st.astype(BF16)) + _dot(p_acc[h].astype(BF16), v16[:, hs])
                   + o_diag[:, hs])
            st_ref[h] = e_last[:, hs] * st + _dot(v16[:, hs].T, k_s[:, hs])
            o_n = o_h * lax.rsqrt(jnp.mean(o_h * o_h, axis=-1, keepdims=True) + EPS)
            outs.append(o_n)
        o = jnp.concatenate(outs, axis=1) * gn * _silu(og)
        y_ref[rows, conv_dim:conv_dim + width] = o.astype(BF16)

    todo = iter(pieces)

    def fill():
        cols = next(todo, None)
        if cols is not None:
            project_next(cols)

    for c in range(tt // CHUNK):
        chunk(c * CHUNK, c % CHUNK_SLOTS, fill)
    for cols in todo:
        project_next(cols)
    o_ref[...] = x + _dot(y_ref[...], wout_ref[...])


def _mixer0_kernel(*refs, n_cast, tiles_per_seq, **static):
    x_ref, xn_ref, gmix_ref, win_ref, convw_ref, lbp_ref, gn_ref, wout_ref = refs[:8]
    cast_refs = refs[8:8 + n_cast]
    o_ref = refs[8 + n_cast]
    cast_o_refs = refs[9 + n_cast:9 + 2 * n_cast]
    proj_a_ref, proj_b_ref, zbuf_ref, st_ref, y_ref, w_ref, s_ref, hn_ref = refs[9 + 2 * n_cast:]
    i = pl.program_id(0)

    @pl.when(i == 0)
    def _():
        proj_a_ref[...] = _dot(_rms(x_ref[...], gmix_ref[...]).astype(BF16), win_ref[...])

    @pl.when(i % tiles_per_seq == 0)
    def _():
        st_ref[...] = jnp.zeros_like(st_ref)
        zbuf_ref[0:SUB, :] = jnp.zeros((SUB, zbuf_ref.shape[1]), F32)

    for parity, (cur, nxt) in enumerate(((proj_a_ref, proj_b_ref), (proj_b_ref, proj_a_ref))):
        @pl.when(i % 2 == parity)
        def _():
            _mixer0_tile(x_ref, xn_ref, gmix_ref, win_ref, convw_ref, lbp_ref, gn_ref, wout_ref,
                         o_ref, cur, nxt, zbuf_ref, st_ref, y_ref, w_ref, s_ref, hn_ref, **static)

    for src, dst in zip(cast_refs, cast_o_refs):
        dst[...] = src[...].astype(BF16)


BF16_ROWS = 16


def _cast_job(w2d, n_steps, step_of):
    rows, cols = w2d.shape
    n_blocks = max(n for n in range(1, n_steps + 1)
                   if rows % n == 0 and (rows // n) % BF16_ROWS == 0)
    spec = pl.BlockSpec((rows // n_blocks, cols),
                        lambda *ids: (jnp.minimum(step_of(*ids), n_blocks - 1), 0))
    return spec, jax.ShapeDtypeStruct(w2d.shape, BF16)


def _mixer0(x, gmix, w_in, conv_w, lbp, gnorm, w_out, layer, cast_ws, tt=256):
    bsz, seq, d = x.shape
    conv_dim = conv_w.shape[1]
    width = lbp.shape[1]
    n_in = w_in.shape[1]
    tt = min(tt, seq)
    heads = width // HEAD_DIM
    gn = jnp.tile(gnorm.reshape(1, HEAD_DIM), (1, heads))
    n_t = seq // tt
    n_tiles = bsz * n_t
    x2 = x.reshape(bsz * seq, d)
    cast_specs, cast_shapes = zip(*[_cast_job(w, n_tiles, lambda i: i) for w in cast_ws])
    kern = functools.partial(_mixer0_kernel, n_cast=len(cast_ws), tiles_per_seq=n_t, layer=layer,
                             conv_dim=conv_dim, width=width)
    out, *cast_out = pl.pallas_call(
        kern,
        grid=(n_tiles,),
        in_specs=[
            pl.BlockSpec((tt, d), lambda i: (i, 0)),
            pl.BlockSpec((tt, d), lambda i: (jnp.minimum(i + 1, n_tiles - 1), 0)),
            _const_spec((1, d)),
            _const_spec((d, n_in)),
            _const_spec(conv_w.shape),
            _const_spec(lbp.shape),
            _const_spec((1, width)),
            _const_spec(w_out.shape),
            *cast_specs,
        ],
        out_specs=[pl.BlockSpec((tt, d), lambda i: (i, 0)), *cast_specs],
        out_shape=[jax.ShapeDtypeStruct(x2.shape, F32), *cast_shapes],
        scratch_shapes=[
            pltpu.VMEM((tt, n_in), F32),
            pltpu.VMEM((tt, n_in), F32),
            pltpu.VMEM((tt + SUB, conv_dim), F32),
            pltpu.VMEM((heads, HEAD_DIM, HEAD_DIM), F32),
            pltpu.VMEM((tt, conv_dim + width), BF16),
            pltpu.VMEM((CHUNK_SLOTS, CHUNK * SUB, width), BF16),
            pltpu.VMEM((CHUNK_SLOTS, CHUNK * SUB, width), F32),
            pltpu.VMEM((tt, d), BF16),
        ],
        compiler_params=_cparams(1),
        name="mixer0",
    )(x2, x2, gmix.reshape(1, d), w_in.astype(BF16), conv_w, lbp, gn, w_out.astype(BF16), *cast_ws)
    return out.reshape(x.shape), cast_out


def _ffn_kernel(x_ref, g_ref, w1_ref, w3_ref, w2_ref, cast_ref, o_ref, cast_o_ref, *, tf):
    x = x_ref[...]
    hn = _rms(x, g_ref[...]).astype(BF16)
    acc = x
    for j in range(w1_ref.shape[1] // tf):
        cols = slice(j * tf, (j + 1) * tf)
        h = _silu(_dot(hn, w1_ref[:, cols])) * _dot(hn, w3_ref[:, cols])
        acc = acc + _dot(h.astype(BF16), w2_ref[cols, :])
    o_ref[...] = acc
    cast_o_ref[...] = cast_ref[...].astype(BF16)


def _ffn(x2, g, w1, w3, w2, cast_w, tm=512, tf=256):
    n, d = x2.shape
    tm = min(tm, n)
    cast_spec, cast_shape = _cast_job(cast_w, n // tm, lambda i: i)
    return pl.pallas_call(
        functools.partial(_ffn_kernel, tf=tf),
        grid=(n // tm,),
        in_specs=[
            pl.BlockSpec((tm, d), lambda i: (i, 0)),
            _const_spec((1, d)),
            _const_spec(w1.shape),
            _const_spec(w3.shape),
            _const_spec(w2.shape),
            cast_spec,
        ],
        out_specs=[pl.BlockSpec((tm, d), lambda i: (i, 0)), cast_spec],
        out_shape=[jax.ShapeDtypeStruct((n, d), F32), cast_shape],
        compiler_params=_cparams(1),
        name="ffn0",
    )(x2, g.reshape(1, d), w1, w3, w2, cast_w)


def _glu_columns(hn_ref, w1_ref, b1_ref, u_ref, cols, *, halo):
    tt, d = hn_ref.shape
    gate_cols = slice(d + cols.start, d + cols.stop)
    a = _dot(hn_ref[...], w1_ref[:, cols]) + b1_ref[:, cols]
    b = _dot(hn_ref[...], w1_ref[:, gate_cols]) + b1_ref[:, gate_cols]
    u_ref[halo:halo + tt, cols] = a * jax.nn.sigmoid(b)


def _conformer_tile(x_ref, xn_ref, g_ref, w1_ref, b1_ref, wdw_ref, bdw_ref, lng_ref, lnb_ref,
                    w2_ref, b2_ref, o_ref, ubuf_ref, unext_ref, conv_ref, hn_ref, *, halo):
    tt, d = x_ref.shape
    kw = wdw_ref.shape[0]

    x = x_ref[...]
    hn_ref[...] = _rms(xn_ref[...], g_ref[...]).astype(BF16)
    todo = iter([slice(c0, c0 + V7X_MXU_DIM) for c0 in range(0, d, V7X_MXU_DIM)])

    def fill():
        cols = next(todo, None)
        if cols is not None:
            _glu_columns(hn_ref, w1_ref, b1_ref, unext_ref, cols, halo=halo)

    off0 = halo - (kw - 1)
    seg = min(tt, 128)
    for c in range(d // V7X_LANES):
        lanes = slice(c * V7X_LANES, (c + 1) * V7X_LANES)
        wcol = wdw_ref[:, lanes]
        if c % 2 == 0:
            fill()
        for base in range(0, tt, seg):
            acc = None
            for r in range(SUB):
                part = None
                for a in range((off0 + kw - 1) // SUB + 1):
                    k = SUB * a + r - off0
                    if 0 <= k < kw:
                        win = ubuf_ref[base + SUB * a:base + SUB * a + seg + SUB, lanes]
                        term = wcol[k:k + 1, :] * win
                        part = term if part is None else part + term
                if part is not None:
                    shifted = part[r:r + seg, :]
                    acc = shifted if acc is None else acc + shifted
            conv_ref[base:base + seg, lanes] = acc + bdw_ref[:, lanes]
    for cols in todo:
        _glu_columns(hn_ref, w1_ref, b1_ref, unext_ref, cols, halo=halo)
    acc = conv_ref[...]
    unext_ref[0:halo, :] = ubuf_ref[tt:tt + halo, :]
    mu = jnp.mean(acc, axis=-1, keepdims=True)
    xc = acc - mu
    var = jnp.mean(xc * xc, axis=-1, keepdims=True)
    u = _silu(xc * lax.rsqrt(var + EPS) * lng_ref[...] + lnb_ref[...])
    o_ref[...] = x + _dot(u.astype(BF16), w2_ref[...]) + b2_ref[...]


def _conformer_kernel(x_ref, xn_ref, g_ref, w1_ref, b1_ref, wdw_ref, bdw_ref, lng_ref, lnb_ref,
                      w2_ref, b2_ref, cast_ref, o_ref, cast_o_ref, ua_ref, ub_ref, conv_ref,
                      hn_ref, *, halo, tiles_per_seq):
    tt, d = x_ref.shape
    i = pl.program_id(0)

    @pl.when(i == 0)
    def _():
        for u_ref in (ua_ref, ub_ref):
            u_ref[halo + tt:halo + tt + SUB, :] = jnp.zeros((SUB, d), F32)
        hn_ref[...] = _rms(x_ref[...], g_ref[...]).astype(BF16)
        for c0 in range(0, d, V7X_MXU_DIM):
            _glu_columns(hn_ref, w1_ref, b1_ref, ua_ref, slice(c0, c0 + V7X_MXU_DIM), halo=halo)

    for parity, (cur, nxt) in enumerate(((ua_ref, ub_ref), (ub_ref, ua_ref))):
        @pl.when(i % 2 == parity)
        def _():
            @pl.when(i % tiles_per_seq == 0)
            def _():
                cur[0:halo, :] = jnp.zeros((halo, d), F32)

            _conformer_tile(x_ref, xn_ref, g_ref, w1_ref, b1_ref, wdw_ref, bdw_ref, lng_ref,
                            lnb_ref, w2_ref, b2_ref, o_ref, cur, nxt, conv_ref, hn_ref, halo=halo)

    cast_o_ref[...] = cast_ref[...].astype(BF16)


def _conformer(x, g, w_pw1, b_pw1, w_dw, b_dw, ln_g, ln_b, w_pw2, b_pw2, cast_w, tt=256):
    bsz, seq, d = x.shape
    tt = min(tt, seq)
    kw = w_dw.shape[0]
    halo = -(-(kw - 1) // SUB) * SUB
    r = lambda a: a.reshape(1, -1)
    n_t = seq // tt
    n_tiles = bsz * n_t
    x2 = x.reshape(bsz * seq, d)
    cast_spec, cast_shape = _cast_job(cast_w, n_tiles, lambda i: i)
    out, cast_out = pl.pallas_call(
        functools.partial(_conformer_kernel, halo=halo, tiles_per_seq=n_t),
        grid=(n_tiles,),
        in_specs=[
            pl.BlockSpec((tt, d), lambda i: (i, 0)),
            pl.BlockSpec((tt, d), lambda i: (jnp.minimum(i + 1, n_tiles - 1), 0)),
            _const_spec((1, d)),
            _const_spec(w_pw1.shape),
            _const_spec((1, 2 * d)),
            _const_spec(w_dw.shape),
            _const_spec((1, d)),
            _const_spec((1, d)),
            _const_spec((1, d)),
            _const_spec(w_pw2.shape),
            _const_spec((1, d)),
            cast_spec,
        ],
        out_specs=[pl.BlockSpec((tt, d), lambda i: (i, 0)), cast_spec],
        out_shape=[jax.ShapeDtypeStruct(x2.shape, F32), cast_shape],
        scratch_shapes=[
            pltpu.VMEM((halo + tt + SUB, d), F32),
            pltpu.VMEM((halo + tt + SUB, d), F32),
            pltpu.VMEM((tt, d), F32),
            pltpu.VMEM((tt, d), BF16),
        ],
        compiler_params=_cparams(1),
        name="conformer",
    )(x2, x2, r(g), w_pw1, r(b_pw1), w_dw, r(b_dw), r(ln_g), r(ln_b), w_pw2, r(b_pw2), cast_w)
    return out.reshape(x.shape), cast_out


def _router_kernel(x_ref, g_ref, wr_ref, hn_ref, meta_ref, gate_ref, cnt_ref, run_ref,
                   *, n_experts):
    tm = x_ref.shape[0]

    @pl.when(pl.program_id(0) == 0)
    def _():
        run_ref[...] = jnp.zeros_like(run_ref)

    hn = _rms(x_ref[...], g_ref[...])
    _store_row_tiles(hn_ref, hn)
    wr = wr_ref[...]
    hn_hi, wr_hi = hn.astype(BF16), wr.astype(BF16)
    hn_lo = (hn - hn_hi.astype(F32)).astype(BF16)
    wr_lo = (wr - wr_hi.astype(F32)).astype(BF16)
    logits = _dot(hn_hi, wr_hi) + (_dot(hn_lo, wr_hi) + _dot(hn_hi, wr_lo))
    lane = lax.broadcasted_iota(I32, logits.shape, 1)
    neg = jnp.float32(-jnp.inf)
    big = jnp.int32(V7X_LANES)
    logits = jnp.where(lane < n_experts, logits, neg)
    m1 = jnp.max(logits, axis=-1, keepdims=True)
    i1 = jnp.min(jnp.where(logits == m1, lane, big), axis=-1, keepdims=True)
    rest = jnp.where(lane == i1, neg, logits)
    m2 = jnp.max(rest, axis=-1, keepdims=True)
    i2 = jnp.min(jnp.where(rest == m2, lane, big), axis=-1, keepdims=True)
    e2 = jnp.exp(m2 - m1)
    g1 = 1.0 / (1.0 + e2)
    g2 = e2 / (1.0 + e2)

    sel1 = lane == i1
    sel2 = lane == i2
    onehot = jnp.where(sel1 | sel2, 1.0, 0.0)
    ti = lax.broadcasted_iota(I32, (tm, tm), 0)
    si = lax.broadcasted_iota(I32, (tm, tm), 1)
    before = _dot((si < ti).astype(BF16), onehot.astype(BF16)) + run_ref[...]
    r1 = jnp.sum(jnp.where(sel1, before, 0.0), axis=-1, keepdims=True)
    r2 = jnp.sum(jnp.where(sel2, before, 0.0), axis=-1, keepdims=True)
    run_ref[...] = run_ref[...] + jnp.sum(onehot, axis=0, keepdims=True)

    meta = jnp.where(lane == 0, i1.astype(F32), jnp.where(lane == 1, i2.astype(F32),
                     jnp.where(lane == 2, r1, jnp.where(lane == 3, r2, 0.0))))
    meta_ref[...] = meta.T[0:SUB, :].astype(I32)
    col = lax.broadcasted_iota(I32, (tm, TOP_K), 1)
    gate_ref[...] = jnp.where(col == 0, g1, g2)
    cnt_ref[...] = run_ref[...].astype(I32)


def _router(x2, g, w_router, tm=512):
    n, d = x2.shape
    assert d == SUB * V7X_LANES
    tm = min(tm, n)
    n_experts = w_router.shape[1]
    wr = jnp.zeros((d, V7X_LANES), F32).at[:, :n_experts].set(w_router)
    return pl.pallas_call(
        functools.partial(_router_kernel, n_experts=n_experts),
        grid=(n // tm,),
        in_specs=[
            pl.BlockSpec((tm, d), lambda i: (i, 0)),
            _const_spec((1, d)),
            _const_spec((d, V7X_LANES)),
        ],
        out_specs=[
            pl.BlockSpec((tm * SUB, V7X_LANES), lambda i: (i, 0)),
            pl.BlockSpec((SUB, tm), lambda i: (0, i)),
            pl.BlockSpec((tm, TOP_K), lambda i: (i, 0)),
            pl.BlockSpec((1, V7X_LANES), lambda i: (0, 0)),
        ],
        out_shape=[
            jax.ShapeDtypeStruct((n * SUB, V7X_LANES), F32),
            jax.ShapeDtypeStruct((SUB, n), I32),
            jax.ShapeDtypeStruct((n, TOP_K), F32),
            jax.ShapeDtypeStruct((1, V7X_LANES), I32),
        ],
        scratch_shapes=[pltpu.VMEM((1, V7X_LANES), F32)],
        compiler_params=_cparams(1),
        name="router",
    )(x2, g.reshape(1, d), wr)


def _dispatch_kernel(slot0_ref, slot1_ref, pstart_ref, plen_ref, used_ref, hn_ref, xs_ref,
                     zero_ref, sem, zsem, *, n_experts, tile):
    tm = hn_ref.shape[0] // SUB
    n_tiles = xs_ref.shape[0] // (tile * SUB)
    slot_refs = (slot0_ref, slot1_ref)

    def row_copy(r, k):
        return pltpu.make_async_copy(
            _row_tile(hn_ref, r), _row_tile(xs_ref, slot_refs[k][r]), sem)

    _start_row_copies(tm, row_copy)
    for k in range(TOP_K):
        pltpu.make_async_copy(hn_ref, xs_ref.at[pl.ds(0, tm * SUB)], sem).wait()

    def rows(ref, start, count):
        return ref.at[pl.ds(pl.multiple_of(start * SUB, SUB), count * SUB)]

    def for_each_pad_copy(act):
        for e in range(n_experts):
            cur = pstart_ref[e]
            length = plen_ref[e]
            p = 1
            while p < tile:
                take = (length & p) != 0

                @pl.when(take)
                def _():
                    act(pltpu.make_async_copy(rows(zero_ref, 0, p), rows(xs_ref, cur, p), zsem))
                cur = cur + jnp.where(take, p, 0)
                p *= 2
        for i in range(n_experts):
            t = used_ref[0] + i

            @pl.when(t < n_tiles)
            def _():
                act(pltpu.make_async_copy(zero_ref, rows(xs_ref, t * tile, tile), zsem))

    @pl.when(pl.program_id(0) == pl.num_programs(0) - 1)
    def _():
        zero_ref[...] = jnp.zeros_like(zero_ref)
        for_each_pad_copy(lambda cp: cp.start())
        for_each_pad_copy(lambda cp: cp.wait())


def _dispatch(hn, slots, pad_start, pad_len, used_tiles, n_tiles, tile, tm=1024):
    n = hn.shape[0] // SUB
    tm = min(tm, n)
    n_experts = pad_start.shape[0]
    smem = pl.BlockSpec(memory_space=pltpu.SMEM)
    slot_spec = pl.BlockSpec((tm,), lambda i: (i,), memory_space=pltpu.SMEM)
    return pl.pallas_call(
        functools.partial(_dispatch_kernel, n_experts=n_experts, tile=tile),
        grid=(n // tm,),
        in_specs=[
            slot_spec, slot_spec,
            smem, smem, smem,
            pl.BlockSpec((tm * SUB, V7X_LANES), lambda i: (i, 0)),
        ],
        out_specs=pl.BlockSpec(memory_space=pl.ANY),
        out_shape=jax.ShapeDtypeStruct((n_tiles * tile * SUB, V7X_LANES), F32),
        scratch_shapes=[
            pltpu.VMEM((tile * SUB, V7X_LANES), F32),
            pltpu.SemaphoreType.DMA(()),
            pltpu.SemaphoreType.DMA(()),
        ],
        compiler_params=_cparams(1),
        name="dispatch",
    )(slots[0], slots[1], pad_start, pad_len, used_tiles, hn)


def _grouped_kernel(exp_ref, used_ref, nsub_ref, x_ref, w1_ref, w3_ref, w2_ref, o_ref,
                    xb_ref, acc_ref, *, sub):
    i = pl.program_id(0)
    j = pl.program_id(1)
    last = pl.num_programs(1) - 1
    valid = i < used_ref[0]
    tile = acc_ref.shape[0]

    @pl.when(valid & (j == 0))
    def _():
        for c in range(SUB):
            xb_ref[:, c * V7X_LANES:(c + 1) * V7X_LANES] = _load_row_tiles(x_ref, tile, c).astype(BF16)
        acc_ref[...] = jnp.zeros_like(acc_ref)

    @pl.when(valid)
    def _():
        def swiglu_rows(n_rows):
            rows = slice(0, n_rows)
            xb = xb_ref[rows, :]
            h = _silu(_dot(xb, w1_ref[...])) * _dot(xb, w3_ref[...])
            acc_ref[rows, :] += _dot(h.astype(BF16), w2_ref[...])

        for n_sub in range(1, tile // sub + 1):
            @pl.when(nsub_ref[i] == n_sub)
            def _():
                swiglu_rows(n_sub * sub)

    @pl.when(valid & (j == last))
    def _():
        _store_row_tiles(o_ref, acc_ref[...])

    @pl.when(jnp.logical_not(valid) & (j == last))
    def _():
        o_ref[...] = jnp.zeros_like(o_ref)


def _grouped(xs, w1, w3, w2, tile_exp, used_tiles, tile_nsub, tile, sub, tf=512):
    d = w1.shape[1]
    n_tiles = xs.shape[0] // (tile * SUB)
    nj = w1.shape[2] // tf

    def jsel(i, j, used):
        return jnp.where(i < used[0], j, nj - 1)

    grid_spec = pltpu.PrefetchScalarGridSpec(
        num_scalar_prefetch=3,
        grid=(n_tiles, nj),
        in_specs=[
            pl.BlockSpec((tile * SUB, V7X_LANES), lambda i, j, ex, us, ns: (i, 0)),
            pl.BlockSpec((None, d, tf), lambda i, j, ex, us, ns: (ex[i], 0, jsel(i, j, us))),
            pl.BlockSpec((None, d, tf), lambda i, j, ex, us, ns: (ex[i], 0, jsel(i, j, us))),
            pl.BlockSpec((None, tf, d), lambda i, j, ex, us, ns: (ex[i], jsel(i, j, us), 0)),
        ],
        out_specs=pl.BlockSpec((tile * SUB, V7X_LANES), lambda i, j, ex, us, ns: (i, 0)),
        scratch_shapes=[pltpu.VMEM((tile, d), BF16), pltpu.VMEM((tile, d), F32)],
    )
    return pl.pallas_call(
        functools.partial(_grouped_kernel, sub=sub),
        grid_spec=grid_spec,
        out_shape=jax.ShapeDtypeStruct(xs.shape, F32),
        compiler_params=_cparams(2),
        name="grouped_swiglu",
    )(tile_exp, used_tiles, tile_nsub, xs, w1, w3, w2)


def _combine_kernel(slot0_ref, slot1_ref, h_ref, gate_ref, g_ref, ys_ref, o_ref, buf_ref, sem):
    tm, d = h_ref.shape
    slot_refs = (slot0_ref, slot1_ref)

    def row_copy(r, k):
        return pltpu.make_async_copy(
            _row_tile(ys_ref, slot_refs[k][r]), _row_tile(buf_ref.at[k], r), sem)

    _start_row_copies(tm, row_copy)
    for k in range(TOP_K):
        pltpu.make_async_copy(ys_ref.at[pl.ds(0, tm * SUB)], buf_ref.at[k], sem).wait()

    gate = gate_ref[...]
    ss = jnp.zeros((tm, 1), F32)
    for c in range(SUB):
        lanes = slice(c * V7X_LANES, (c + 1) * V7X_LANES)
        piece = (h_ref[:, lanes] + gate[:, 0:1] * _load_row_tiles(buf_ref.at[0], tm, c)
                 + gate[:, 1:2] * _load_row_tiles(buf_ref.at[1], tm, c))
        o_ref[:, lanes] = piece
        ss = ss + jnp.sum(piece * piece, axis=-1, keepdims=True)
    o_ref[...] = o_ref[...] * lax.rsqrt(ss / d + EPS) * g_ref[...]


def _combine(h2, gates, slots, ys, g, tm=1024):
    n, d = h2.shape
    tm = min(tm, n)
    slot_spec = pl.BlockSpec((tm,), lambda i: (i,), memory_space=pltpu.SMEM)
    return pl.pallas_call(
        _combine_kernel,
        grid=(n // tm,),
        in_specs=[
            slot_spec, slot_spec,
            pl.BlockSpec((tm, d), lambda i: (i, 0)),
            pl.BlockSpec((tm, TOP_K), lambda i: (i, 0)),
            _const_spec((1, d)),
            pl.BlockSpec(memory_space=pl.ANY),
        ],
        out_specs=pl.BlockSpec((tm, d), lambda i: (i, 0)),
        out_shape=jax.ShapeDtypeStruct((n, d), F32),
        scratch_shapes=[pltpu.VMEM((TOP_K, tm * SUB, V7X_LANES), F32),
                        pltpu.SemaphoreType.DMA(())],
        compiler_params=_cparams(1),
        name="combine",
    )(slots[0], slots[1], h2, gates, g.reshape(1, d), ys)


def _moe(h2, g_ffn, w_router, w1, w3, w2, g_final, tile=1024, sub=256):
    n, d = h2.shape
    n_experts = w_router.shape[1]
    tile = min(tile, n)
    n_tiles = TOP_K * n // tile + n_experts
    hn, meta, gates, counts = _router(h2, g_ffn, w_router)
    expert, rank = meta[0:TOP_K], meta[TOP_K:2 * TOP_K]
    counts = counts[0, :n_experts]

    tiles_per = (counts + tile - 1) // tile
    ends = jnp.cumsum(tiles_per)
    starts = ends - tiles_per
    used = ends[-1:]
    slots = rank
    for e in range(n_experts):
        slots = slots + jnp.where(expert == e, starts[e] * tile, 0)
    slots = slots.astype(I32)
    pad_start = (starts * tile + counts).astype(I32)
    pad_len = (tiles_per * tile - counts).astype(I32)
    tidx = jnp.minimum(jnp.arange(n_tiles, dtype=I32), used - 1)
    tile_exp = jnp.sum((tidx[:, None] >= ends[None, :]).astype(I32), axis=1)
    sub = min(sub, tile)
    tile_rows = jnp.clip(counts[tile_exp] - (tidx - starts[tile_exp]) * tile, 0, tile)
    tile_nsub = (tile_rows + sub - 1) // sub

    xs = _dispatch(hn, slots, pad_start, pad_len, used.astype(I32), n_tiles, tile)
    ys = _grouped(xs, w1, w3, w2, tile_exp.astype(I32), used.astype(I32), tile_nsub.astype(I32),
                  tile, sub)
    return _combine(h2, gates, slots, ys, g_final)


def kernel(x, ev_norm_mix, ev_w_in, ev_conv_w, hgrn_lower_bounds, ev_gnorm, ev_w_out,
           ev_norm_ffn, ev_ffn_w1, ev_ffn_w3, ev_ffn_w2,
           od_norm_mix, od_w_pw1, od_b_pw1, od_w_dw, od_b_dw, od_ln_g, od_ln_b,
           od_w_pw2, od_b_pw2, od_norm_ffn, od_router, od_moe_w1, od_moe_w3, od_moe_w2,
           final_norm):
    bsz, seq, d = x.shape
    assert ev_w_in.shape[0] == 1 and od_w_pw1.shape[0] == 1
    n_experts, _, n_ff = od_moe_w1.shape[1:]
    flat = lambda w: w.reshape(-1, w.shape[-1])
    h, (w1b, f1b, f3b, f2b, pw1b, pw2b) = _mixer0(
        x, ev_norm_mix[0], ev_w_in[0], ev_conv_w[0], hgrn_lower_bounds, ev_gnorm[0], ev_w_out[0],
        layer=0, cast_ws=[flat(od_moe_w1[0]), ev_ffn_w1[0], ev_ffn_w3[0], ev_ffn_w2[0],
                          od_w_pw1[0], od_w_pw2[0]])
    h, w3b = _ffn(h.reshape(bsz * seq, d), ev_norm_ffn[0], f1b, f3b, f2b,
                  cast_w=flat(od_moe_w3[0]))
    h, w2b = _conformer(h.reshape(bsz, seq, d), od_norm_mix[0], pw1b, od_b_pw1[0],
                        od_w_dw[0], od_b_dw[0], od_ln_g[0], od_ln_b[0], pw2b, od_b_pw2[0],
                        cast_w=flat(od_moe_w2[0]))
    out = _moe(h.reshape(bsz * seq, d), od_norm_ffn[0], od_router[0],
               w1b.reshape(n_experts, d, n_ff), w3b.reshape(n_experts, d, n_ff),
               w2b.reshape(n_experts, n_ff, d), final_norm)
    return out.reshape(bsz, seq, d)
```

```python
import functools

import jax
import jax.numpy as jnp
from jax import lax
from jax.experimental import pallas as pl
from jax.experimental.pallas import tpu as pltpu

F32 = jnp.float32
BF16 = jnp.bfloat16
I32 = jnp.int32

EPS = 1e-6
CHUNK = 64
CHUNK_SLOTS = 2
PROJ_PIECE = 256
SUB = 8
HEAD_DIM = 128
TOP_K = 2

V7X_LANES = 128
V7X_MXU_DIM = 256
V7X_VMEM_BYTES = 64 * 1024 * 1024
VMEM_LIMIT = V7X_VMEM_BYTES - 8 * 1024 * 1024


def _cparams(n_axes):
    return pltpu.CompilerParams(
        dimension_semantics=("arbitrary",) * n_axes, vmem_limit_bytes=VMEM_LIMIT)


def _const_spec(shape):
    nd = len(shape)
    return pl.BlockSpec(shape, lambda *_: (0,) * nd, pipeline_mode=pl.Buffered(1))


def _rms(x, g):
    return x * lax.rsqrt(jnp.mean(x * x, axis=-1, keepdims=True) + EPS) * g


def _silu(x):
    return x * jax.nn.sigmoid(x)


def _dot(a, b):
    return jnp.dot(a, b, preferred_element_type=F32)


def _dot_nt(a, b):
    return lax.dot_general(a, b, (((1,), (1,)), ((), ())), preferred_element_type=F32)


def _store_row_tiles(ref, val):
    m = val.shape[0]
    for c in range(SUB):
        ref[pl.ds(c, m, stride=SUB), :] = val[:, c * V7X_LANES:(c + 1) * V7X_LANES]


def _load_row_tiles(ref, m, c):
    return ref[pl.ds(c, m, stride=SUB), :]


def _row_tile(ref, row):
    return ref.at[pl.ds(pl.multiple_of(row * SUB, SUB), SUB)]


ROW_COPY_UNROLL = 8


def _start_row_copies(n_rows, row_copy):
    def body(i, carry):
        for u in range(ROW_COPY_UNROLL):
            for k in range(TOP_K):
                row_copy(i * ROW_COPY_UNROLL + u, k).start(priority=(u * TOP_K + k) % 2)
        return carry

    lax.fori_loop(0, n_rows // ROW_COPY_UNROLL, body, 0)


def _mixer0_tile(x_ref, xn_ref, gmix_ref, win_ref, convw_ref, lbp_ref, gn_ref, wout_ref, o_ref,
                 proj_ref, next_ref, zbuf_ref, st_ref, y_ref, w_ref, s_ref, hn_ref, *, layer,
                 conv_dim, width):
    tt = x_ref.shape[0]
    heads = width // HEAD_DIM
    c3 = 3 * conv_dim

    x = x_ref[...]
    hn_ref[...] = _rms(xn_ref[...], gmix_ref[...]).astype(BF16)
    n_in = win_ref.shape[1]
    pieces = [slice(c0, min(c0 + PROJ_PIECE, n_in)) for c0 in range(0, n_in, PROJ_PIECE)]

    def project_next(cols):
        next_ref[:, cols] = _dot(hn_ref[...], win_ref[:, cols])

    z = proj_ref[:, 2 * conv_dim:c3] * proj_ref[:, 0:conv_dim]
    zbuf_ref[SUB:SUB + tt, :] = z
    cw = convw_ref[...]
    conv = (cw[2:3, :] * z + cw[1:2, :] * zbuf_ref[SUB - 1:SUB - 1 + tt, :]
            + cw[0:1, :] * zbuf_ref[SUB - 2:SUB - 2 + tt, :])
    y_ref[:, 0:conv_dim] = (proj_ref[:, conv_dim:2 * conv_dim] * conv).astype(BF16)
    zbuf_ref[0:SUB, :] = zbuf_ref[tt:tt + SUB, :]

    lbp = lbp_ref[...]
    lbe = jnp.exp(lbp - jnp.max(lbp, axis=0, keepdims=True))
    lb = jnp.sum(lbe[0:layer + 1, :], axis=0, keepdims=True) / jnp.sum(lbe, axis=0, keepdims=True)

    row = lax.broadcasted_iota(I32, (CHUNK, width), 0)
    row8 = lax.broadcasted_iota(I32, (SUB, width), 0)
    ti = lax.broadcasted_iota(I32, (CHUNK, CHUNK), 0)
    si = lax.broadcasted_iota(I32, (CHUNK, CHUNK), 1)
    tri = (si <= ti).astype(BF16)
    bdr = lax.broadcasted_iota(I32, (V7X_MXU_DIM, V7X_MXU_DIM), 0) // HEAD_DIM
    bdc = lax.broadcasted_iota(I32, (V7X_MXU_DIM, V7X_MXU_DIM), 1) // HEAD_DIM
    head_ones = (bdr == bdc).astype(BF16)
    gn = gn_ref[...]

    def bcast_row(val, r, n):
        return jnp.broadcast_to(val[r:r + 1, :], (n, val.shape[1]))

    def chunk(r0, slot, fill):
        rows = slice(r0, r0 + CHUNK)
        ws_ref = w_ref.at[slot]
        q = proj_ref[rows, c3:c3 + width]
        f = proj_ref[rows, c3 + width:c3 + 2 * width]
        v = proj_ref[rows, c3 + 2 * width:c3 + 3 * width]
        og = proj_ref[rows, c3 + 3 * width:c3 + 4 * width]
        fg = lb + (1.0 - lb) * jax.nn.sigmoid(f)
        k = 1.0 - fg
        g_rest = jnp.log(fg)
        b = jnp.zeros((CHUNK, width), F32)
        for _ in range(3):
            g_part = g_rest.astype(BF16)
            b = b + _dot(tri, g_part)
            g_rest = g_rest - g_part.astype(F32)
        b_last = b[CHUNK - 1:CHUNK, :]
        v16 = v.astype(BF16)
        fill()

        q_s = (q * jnp.exp(b)).astype(BF16)
        k_s = (k * jnp.exp(b_last - b)).astype(BF16)
        e_last = jnp.exp(b_last)

        p_acc = [jnp.zeros((CHUNK, CHUNK), F32) for _ in range(heads)]
        half = CHUNK // 2
        while half >= SUB:
            blk = 2 * half
            anc = jnp.concatenate(
                [bcast_row(b, i * blk + half, blk) for i in range(CHUNK // blk)], axis=0)
            upper = (row % blk) >= half
            q_l = (q * jnp.where(upper, jnp.exp(b - anc), 0.0)).astype(BF16)
            k_l = (k * jnp.where(upper, 0.0, jnp.exp(anc - b))).astype(BF16)
            same = (ti // blk) == (si // blk)
            for h in range(heads):
                hs = slice(h * HEAD_DIM, (h + 1) * HEAD_DIM)
                p_acc[h] = p_acc[h] + jnp.where(same, _dot_nt(q_l[:, hs], k_l[:, hs]), 0.0)
            half //= 2
        fill()

        for rb in range(CHUNK // SUB):
            if rb == CHUNK // SUB // 2:
                fill()
            rs = slice(rb * SUB, (rb + 1) * SUB)
            b8, q8, k8 = b[rs, :], q[rs, :], k[rs, :]
            for s in range(0, SUB, 2):
                pair = []
                for s1 in (s, s + 1):
                    dec = jnp.where(row8 >= s1, jnp.exp(b8 - bcast_row(b8, s1, SUB)), 0.0)
                    pair.append(q8 * dec * bcast_row(k8, s1, SUB))
                ws_ref[(rb * SUB + s) * SUB:(rb * SUB + s + 2) * SUB, :] = (
                    jnp.concatenate(pair, axis=0).astype(BF16))
        sc_ref = s_ref.at[slot]
        for j in range(width // V7X_MXU_DIM):
            lanes = slice(j * V7X_MXU_DIM, (j + 1) * V7X_MXU_DIM)
            sc_ref[:, lanes] = _dot(ws_ref[:, lanes], head_ones)
        o_diag = []
        for rb in range(CHUNK // SUB):
            sc = sc_ref[rb * SUB * SUB:(rb + 1) * SUB * SUB, :]
            v8 = v[rb * SUB:(rb + 1) * SUB, :]
            acc = None
            for s in range(SUB):
                term = sc[s * SUB:(s + 1) * SUB, :] * bcast_row(v8, s, SUB)
                acc = term if acc is None else acc + term
            o_diag.append(acc)
        o_diag = jnp.concatenate(o_diag, axis=0)
        fill()

        outs = []
        for h in range(heads):
            hs = slice(h * HEAD_DIM, (h + 1) * HEAD_DIM)
            st = st_ref[h]
            o_h = (_dot_nt(q_s[:, hs], st.astype(BF16)) + _dot(p_acc[h].astype(BF16), v16[:, hs])
                   + o_diag[:, hs])
            st_ref[h] = e_last[:, hs] * st + _dot(v16[:, hs].T, k_s[:, hs])
            o_n = o_h * lax.rsqrt(jnp.mean(o_h * o_h, axis=-1, keepdims=True) + EPS)
            outs.append(o_n)
        o = jnp.concatenate(outs, axis=1) * gn * _silu(og)
        y_ref[rows, conv_dim:conv_dim + width] = o.astype(BF16)

    todo = iter(pieces)

    def fill():
        cols = next(todo, None)
        if cols is not None:
            project_next(cols)

    for c in range(tt // CHUNK):
        chunk(c * CHUNK, c % CHUNK_SLOTS, fill)
    for cols in todo:
        project_next(cols)
    o_ref[...] = x + _dot(y_ref[...], wout_ref[...])


def _mixer0_kernel(*refs, n_cast, tiles_per_seq, **static):
    x_ref, xn_ref, gmix_ref, win_ref, convw_ref, lbp_ref, gn_ref, wout_ref = refs[:8]
    cast_refs = refs[8:8 + n_cast]
    o_ref = refs[8 + n_cast]
    cast_o_refs = refs[9 + n_cast:9 + 2 * n_cast]
    proj_a_ref, proj_b_ref, zbuf_ref, st_ref, y_ref, w_ref, s_ref, hn_ref = refs[9 + 2 * n_cast:]
    i = pl.program_id(0)

    @pl.when(i == 0)
    def _():
        proj_a_ref[...] = _dot(_rms(x_ref[...], gmix_ref[...]).astype(BF16), win_ref[...])

    @pl.when(i % tiles_per_seq == 0)
    def _():
        st_ref[...] = jnp.zeros_like(st_ref)
        zbuf_ref[0:SUB, :] = jnp.zeros((SUB, zbuf_ref.shape[1]), F32)

    for parity, (cur, nxt) in enumerate(((proj_a_ref, proj_b_ref), (proj_b_ref, proj_a_ref))):
        @pl.when(i % 2 == parity)
        def _():
            _mixer0_tile(x_ref, xn_ref, gmix_ref, win_ref, convw_ref, lbp_ref, gn_ref, wout_ref,
                         o_ref, cur, nxt, zbuf_ref, st_ref, y_ref, w_ref, s_ref, hn_ref, **static)

    for src, dst in zip(cast_refs, cast_o_refs):
        dst[...] = src[...].astype(BF16)


BF16_ROWS = 16


def _cast_job(w2d, n_steps, step_of):
    rows, cols = w2d.shape
    n_blocks = max(n for n in range(1, n_steps + 1)
                   if rows % n == 0 and (rows // n) % BF16_ROWS == 0)
    spec = pl.BlockSpec((rows // n_blocks, cols),
                        lambda *ids: (jnp.minimum(step_of(*ids), n_blocks - 1), 0))
    return spec, jax.ShapeDtypeStruct(w2d.shape, BF16)


def _mixer0(x, gmix, w_in, conv_w, lbp, gnorm, w_out, layer, cast_ws, tt=256):
    bsz, seq, d = x.shape
    conv_dim = conv_w.shape[1]
    width = lbp.shape[1]
    n_in = w_in.shape[1]
    tt = min(tt, seq)
    heads = width // HEAD_DIM
    gn = jnp.tile(gnorm.reshape(1, HEAD_DIM), (1, heads))
    n_t = seq // tt
    n_tiles = bsz * n_t
    x2 = x.reshape(bsz * seq, d)
    cast_specs, cast_shapes = zip(*[_cast_job(w, n_tiles, lambda i: i) for w in cast_ws])
    kern = functools.partial(_mixer0_kernel, n_cast=len(cast_ws), tiles_per_seq=n_t, layer=layer,
                             conv_dim=conv_dim, width=width)
    out, *cast_out = pl.pallas_call(
        kern,
        grid=(n_tiles,),
        in_specs=[
            pl.BlockSpec((tt, d), lambda i: (i, 0)),
            pl.BlockSpec((tt, d), lambda i: (jnp.minimum(i + 1, n_tiles - 1), 0)),
            _const_spec((1, d)),
            _const_spec((d, n_in)),
            _const_spec(conv_w.shape),
            _const_spec(lbp.shape),
            _const_spec((1, width)),
            _const_spec(w_out.shape),
            *cast_specs,
        ],
        out_specs=[pl.BlockSpec((tt, d), lambda i: (i, 0)), *cast_specs],
        out_shape=[jax.ShapeDtypeStruct(x2.shape, F32), *cast_shapes],
        scratch_shapes=[
            pltpu.VMEM((tt, n_in), F32),
            pltpu.VMEM((tt, n_in), F32),
            pltpu.VMEM((tt + SUB, conv_dim), F32),
            pltpu.VMEM((heads, HEAD_DIM, HEAD_DIM), F32),
            pltpu.VMEM((tt, conv_dim + width), BF16),
            pltpu.VMEM((CHUNK_SLOTS, CHUNK * SUB, width), BF16),
            pltpu.VMEM((CHUNK_SLOTS, CHUNK * SUB, width), F32),
            pltpu.VMEM((tt, d), BF16),
        ],
        compiler_params=_cparams(1),
        name="mixer0",
    )(x2, x2, gmix.reshape(1, d), w_in.astype(BF16), conv_w, lbp, gn, w_out.astype(BF16), *cast_ws)
    return out.reshape(x.shape), cast_out


def _ffn_kernel(x_ref, g_ref, w1_ref, w3_ref, w2_ref, cast_ref, o_ref, cast_o_ref, *, tf):
    x = x_ref[...]
    hn = _rms(x, g_ref[...]).astype(BF16)
    acc = x
    for j in range(w1_ref.shape[1] // tf):
        cols = slice(j * tf, (j + 1) * tf)
        h = _silu(_dot(hn, w1_ref[:, cols])) * _dot(hn, w3_ref[:, cols])
        acc = acc + _dot(h.astype(BF16), w2_ref[cols, :])
    o_ref[...] = acc
    cast_o_ref[...] = cast_ref[...].astype(BF16)


def _ffn(x2, g, w1, w3, w2, cast_w, tm=512, tf=256):
    n, d = x2.shape
    tm = min(tm, n)
    cast_spec, cast_shape = _cast_job(cast_w, n // tm, lambda i: i)
    return pl.pallas_call(
        functools.partial(_ffn_kernel, tf=tf),
        grid=(n // tm,),
        in_specs=[
            pl.BlockSpec((tm, d), lambda i: (i, 0)),
            _const_spec((1, d)),
            _const_spec(w1.shape),
            _const_spec(w3.shape),
            _const_spec(w2.shape),
            cast_spec,
        ],
        out_specs=[pl.BlockSpec((tm, d), lambda i: (i, 0)), cast_spec],
        out_shape=[jax.ShapeDtypeStruct((n, d), F32), cast_shape],
        compiler_params=_cparams(1),
        name="ffn0",
    )(x2, g.reshape(1, d), w1, w3, w2, cast_w)


def _conformer_kernel(x_ref, g_ref, w1_ref, b1_ref, wdw_ref, bdw_ref, lng_ref, lnb_ref,
                      w2_ref, b2_ref, cast_ref, o_ref, cast_o_ref, ubuf_ref, conv_ref, *, halo):
    tt = x_ref.shape[1]
    d = x_ref.shape[2]
    kw = wdw_ref.shape[0]

    @pl.when(pl.program_id(1) == 0)
    def _():
        ubuf_ref[0:halo, :] = jnp.zeros((halo, d), F32)
        ubuf_ref[halo + tt:halo + tt + SUB, :] = jnp.zeros((SUB, d), F32)

    x = x_ref[0]
    hn = _rms(x, g_ref[...]).astype(BF16)
    p = _dot(hn, w1_ref[...]) + b1_ref[...]
    ubuf_ref[halo:halo + tt, :] = p[:, 0:d] * jax.nn.sigmoid(p[:, d:2 * d])
    off0 = halo - (kw - 1)
    seg = min(tt, 128)
    for c in range(d // V7X_LANES):
        lanes = slice(c * V7X_LANES, (c + 1) * V7X_LANES)
        wcol = wdw_ref[:, lanes]
        for base in range(0, tt, seg):
            acc = None
            for r in range(SUB):
                part = None
                for a in range((off0 + kw - 1) // SUB + 1):
                    k = SUB * a + r - off0
                    if 0 <= k < kw:
                        win = ubuf_ref[base + SUB * a:base + SUB * a + seg + SUB, lanes]
                        term = wcol[k:k + 1, :] * win
                        part = term if part is None else part + term
                if part is not None:
                    shifted = part[r:r + seg, :]
                    acc = shifted if acc is None else acc + shifted
            conv_ref[base:base + seg, lanes] = acc + bdw_ref[:, lanes]
    acc = conv_ref[...]
    ubuf_ref[0:halo, :] = ubuf_ref[tt:tt + halo, :]
    mu = jnp.mean(acc, axis=-1, keepdims=True)
    xc = acc - mu
    var = jnp.mean(xc * xc, axis=-1, keepdims=True)
    u = _silu(xc * lax.rsqrt(var + EPS) * lng_ref[...] + lnb_ref[...])
    o_ref[0] = x + _dot(u.astype(BF16), w2_ref[...]) + b2_ref[...]
    cast_o_ref[...] = cast_ref[...].astype(BF16)


def _conformer(x, g, w_pw1, b_pw1, w_dw, b_dw, ln_g, ln_b, w_pw2, b_pw2, cast_w, tt=256):
    bsz, seq, d = x.shape
    tt = min(tt, seq)
    kw = w_dw.shape[0]
    halo = -(-(kw - 1) // SUB) * SUB
    r = lambda a: a.reshape(1, -1)
    n_t = seq // tt
    cast_spec, cast_shape = _cast_job(cast_w, bsz * n_t, lambda b, t: b * n_t + t)
    return pl.pallas_call(
        functools.partial(_conformer_kernel, halo=halo),
        grid=(bsz, seq // tt),
        in_specs=[
            pl.BlockSpec((1, tt, d), lambda b, t: (b, t, 0)),
            _const_spec((1, d)),
            _const_spec(w_pw1.shape),
            _const_spec((1, 2 * d)),
            _const_spec(w_dw.shape),
            _const_spec((1, d)),
            _const_spec((1, d)),
            _const_spec((1, d)),
            _const_spec(w_pw2.shape),
            _const_spec((1, d)),
            cast_spec,
        ],
        out_specs=[pl.BlockSpec((1, tt, d), lambda b, t: (b, t, 0)), cast_spec],
        out_shape=[jax.ShapeDtypeStruct(x.shape, F32), cast_shape],
        scratch_shapes=[pltpu.VMEM((halo + tt + SUB, d), F32), pltpu.VMEM((tt, d), F32)],
        compiler_params=_cparams(2),
        name="conformer",
    )(x, r(g), w_pw1, r(b_pw1), w_dw, r(b_dw), r(ln_g), r(ln_b), w_pw2, r(b_pw2), cast_w)


def _router_kernel(x_ref, g_ref, wr_ref, hn_ref, meta_ref, gate_ref, cnt_ref, run_ref,
                   *, n_experts):
    tm = x_ref.shape[0]

    @pl.when(pl.program_id(0) == 0)
    def _():
        run_ref[...] = jnp.zeros_like(run_ref)

    hn = _rms(x_ref[...], g_ref[...])
    _store_row_tiles(hn_ref, hn)
    wr = wr_ref[...]
    hn_hi, wr_hi = hn.astype(BF16), wr.astype(BF16)
    hn_lo = (hn - hn_hi.astype(F32)).astype(BF16)
    wr_lo = (wr - wr_hi.astype(F32)).astype(BF16)
    logits = _dot(hn_hi, wr_hi) + (_dot(hn_lo, wr_hi) + _dot(hn_hi, wr_lo))
    lane = lax.broadcasted_iota(I32, logits.shape, 1)
    neg = jnp.float32(-jnp.inf)
    big = jnp.int32(V7X_LANES)
    logits = jnp.where(lane < n_experts, logits, neg)
    m1 = jnp.max(logits, axis=-1, keepdims=True)
    i1 = jnp.min(jnp.where(logits == m1, lane, big), axis=-1, keepdims=True)
    rest = jnp.where(lane == i1, neg, logits)
    m2 = jnp.max(rest, axis=-1, keepdims=True)
    i2 = jnp.min(jnp.where(rest == m2, lane, big), axis=-1, keepdims=True)
    e2 = jnp.exp(m2 - m1)
    g1 = 1.0 / (1.0 + e2)
    g2 = e2 / (1.0 + e2)

    sel1 = lane == i1
    sel2 = lane == i2
    onehot = jnp.where(sel1 | sel2, 1.0, 0.0)
    ti = lax.broadcasted_iota(I32, (tm, tm), 0)
    si = lax.broadcasted_iota(I32, (tm, tm), 1)
    before = _dot((si < ti).astype(BF16), onehot.astype(BF16)) + run_ref[...]
    r1 = jnp.sum(jnp.where(sel1, before, 0.0), axis=-1, keepdims=True)
    r2 = jnp.sum(jnp.where(sel2, before, 0.0), axis=-1, keepdims=True)
    run_ref[...] = run_ref[...] + jnp.sum(onehot, axis=0, keepdims=True)

    meta = jnp.where(lane == 0, i1.astype(F32), jnp.where(lane == 1, i2.astype(F32),
                     jnp.where(lane == 2, r1, jnp.where(lane == 3, r2, 0.0))))
    meta_ref[...] = meta.T[0:SUB, :].astype(I32)
    col = lax.broadcasted_iota(I32, (tm, TOP_K), 1)
    gate_ref[...] = jnp.where(col == 0, g1, g2)
    cnt_ref[...] = run_ref[...].astype(I32)


def _router(x2, g, w_router, tm=512):
    n, d = x2.shape
    assert d == SUB * V7X_LANES
    tm = min(tm, n)
    n_experts = w_router.shape[1]
    wr = jnp.zeros((d, V7X_LANES), F32).at[:, :n_experts].set(w_router)
    return pl.pallas_call(
        functools.partial(_router_kernel, n_experts=n_experts),
        grid=(n // tm,),
        in_specs=[
            pl.BlockSpec((tm, d), lambda i: (i, 0)),
            _const_spec((1, d)),
            _const_spec((d, V7X_LANES)),
        ],
        out_specs=[
            pl.BlockSpec((tm * SUB, V7X_LANES), lambda i: (i, 0)),
            pl.BlockSpec((SUB, tm), lambda i: (0, i)),
            pl.BlockSpec((tm, TOP_K), lambda i: (i, 0)),
            pl.BlockSpec((1, V7X_LANES), lambda i: (0, 0)),
        ],
        out_shape=[
            jax.ShapeDtypeStruct((n * SUB, V7X_LANES), F32),
            jax.ShapeDtypeStruct((SUB, n), I32),
            jax.ShapeDtypeStruct((n, TOP_K), F32),
            jax.ShapeDtypeStruct((1, V7X_LANES), I32),
        ],
        scratch_shapes=[pltpu.VMEM((1, V7X_LANES), F32)],
        compiler_params=_cparams(1),
        name="router",
    )(x2, g.reshape(1, d), wr)


def _dispatch_kernel(slot0_ref, slot1_ref, pstart_ref, plen_ref, used_ref, hn_ref, xs_ref,
                     zero_ref, sem, zsem, *, n_experts, tile):
    tm = hn_ref.shape[0] // SUB
    n_tiles = xs_ref.shape[0] // (tile * SUB)
    slot_refs = (slot0_ref, slot1_ref)

    def row_copy(r, k):
        return pltpu.make_async_copy(
            _row_tile(hn_ref, r), _row_tile(xs_ref, slot_refs[k][r]), sem)

    _start_row_copies(tm, row_copy)
    for k in range(TOP_K):
        pltpu.make_async_copy(hn_ref, xs_ref.at[pl.ds(0, tm * SUB)], sem).wait()

    def rows(ref, start, count):
        return ref.at[pl.ds(pl.multiple_of(start * SUB, SUB), count * SUB)]

    def for_each_pad_copy(act):
        for e in range(n_experts):
            cur = pstart_ref[e]
            length = plen_ref[e]
            p = 1
            while p < tile:
                take = (length & p) != 0

                @pl.when(take)
                def _():
                    act(pltpu.make_async_copy(rows(zero_ref, 0, p), rows(xs_ref, cur, p), zsem))
                cur = cur + jnp.where(take, p, 0)
                p *= 2
        for i in range(n_experts):
            t = used_ref[0] + i

            @pl.when(t < n_tiles)
            def _():
                act(pltpu.make_async_copy(zero_ref, rows(xs_ref, t * tile, tile), zsem))

    @pl.when(pl.program_id(0) == pl.num_programs(0) - 1)
    def _():
        zero_ref[...] = jnp.zeros_like(zero_ref)
        for_each_pad_copy(lambda cp: cp.start())
        for_each_pad_copy(lambda cp: cp.wait())


def _dispatch(hn, slots, pad_start, pad_len, used_tiles, n_tiles, tile, tm=1024):
    n = hn.shape[0] // SUB
    tm = min(tm, n)
    n_experts = pad_start.shape[0]
    smem = pl.BlockSpec(memory_space=pltpu.SMEM)
    slot_spec = pl.BlockSpec((tm,), lambda i: (i,), memory_space=pltpu.SMEM)
    return pl.pallas_call(
        functools.partial(_dispatch_kernel, n_experts=n_experts, tile=tile),
        grid=(n // tm,),
        in_specs=[
            slot_spec, slot_spec,
            smem, smem, smem,
            pl.BlockSpec((tm * SUB, V7X_LANES), lambda i: (i, 0)),
        ],
        out_specs=pl.BlockSpec(memory_space=pl.ANY),
        out_shape=jax.ShapeDtypeStruct((n_tiles * tile * SUB, V7X_LANES), F32),
        scratch_shapes=[
            pltpu.VMEM((tile * SUB, V7X_LANES), F32),
            pltpu.SemaphoreType.DMA(()),
            pltpu.SemaphoreType.DMA(()),
        ],
        compiler_params=_cparams(1),
        name="dispatch",
    )(slots[0], slots[1], pad_start, pad_len, used_tiles, hn)


def _grouped_kernel(exp_ref, used_ref, nsub_ref, x_ref, w1_ref, w3_ref, w2_ref, o_ref,
                    xb_ref, acc_ref, *, sub):
    i = pl.program_id(0)
    j = pl.program_id(1)
    last = pl.num_programs(1) - 1
    valid = i < used_ref[0]
    tile = acc_ref.shape[0]

    @pl.when(valid & (j == 0))
    def _():
        for c in range(SUB):
            xb_ref[:, c * V7X_LANES:(c + 1) * V7X_LANES] = _load_row_tiles(x_ref, tile, c).astype(BF16)
        acc_ref[...] = jnp.zeros_like(acc_ref)

    @pl.when(valid)
    def _():
        def swiglu_rows(n_rows):
            rows = slice(0, n_rows)
            xb = xb_ref[rows, :]
            h = _silu(_dot(xb, w1_ref[...])) * _dot(xb, w3_ref[...])
            acc_ref[rows, :] += _dot(h.astype(BF16), w2_ref[...])

        for n_sub in range(1, tile // sub + 1):
            @pl.when(nsub_ref[i] == n_sub)
            def _():
                swiglu_rows(n_sub * sub)

    @pl.when(valid & (j == last))
    def _():
        _store_row_tiles(o_ref, acc_ref[...])

    @pl.when(jnp.logical_not(valid) & (j == last))
    def _():
        o_ref[...] = jnp.zeros_like(o_ref)


def _grouped(xs, w1, w3, w2, tile_exp, used_tiles, tile_nsub, tile, sub, tf=512):
    d = w1.shape[1]
    n_tiles = xs.shape[0] // (tile * SUB)
    nj = w1.shape[2] // tf

    def jsel(i, j, used):
        return jnp.where(i < used[0], j, nj - 1)

    grid_spec = pltpu.PrefetchScalarGridSpec(
        num_scalar_prefetch=3,
        grid=(n_tiles, nj),
        in_specs=[
            pl.BlockSpec((tile * SUB, V7X_LANES), lambda i, j, ex, us, ns: (i, 0)),
            pl.BlockSpec((None, d, tf), lambda i, j, ex, us, ns: (ex[i], 0, jsel(i, j, us))),
            pl.BlockSpec((None, d, tf), lambda i, j, ex, us, ns: (ex[i], 0, jsel(i, j, us))),
            pl.BlockSpec((None, tf, d), lambda i, j, ex, us, ns: (ex[i], jsel(i, j, us), 0)),
        ],
        out_specs=pl.BlockSpec((tile * SUB, V7X_LANES), lambda i, j, ex, us, ns: (i, 0)),
        scratch_shapes=[pltpu.VMEM((tile, d), BF16), pltpu.VMEM((tile, d), F32)],
    )
    return pl.pallas_call(
        functools.partial(_grouped_kernel, sub=sub),
        grid_spec=grid_spec,
        out_shape=jax.ShapeDtypeStruct(xs.shape, F32),
        compiler_params=_cparams(2),
        name="grouped_swiglu",
    )(tile_exp, used_tiles, tile_nsub, xs, w1, w3, w2)


def _combine_kernel(slot0_ref, slot1_ref, nslot0_ref, nslot1_ref, h_ref, gate_ref, g_ref, ys_ref,
                    o_ref, buf_a_ref, buf_b_ref, sems):
    tm, d = h_ref.shape
    i = pl.program_id(0)

    def start_gather(slot_refs, buf_ref, sem):
        def row_copy(r, k):
            return pltpu.make_async_copy(
                _row_tile(ys_ref, slot_refs[k][r]), _row_tile(buf_ref.at[k], r), sem)

        _start_row_copies(tm, row_copy)

    def finish(buf_ref, sem):
        for k in range(TOP_K):
            pltpu.make_async_copy(ys_ref.at[pl.ds(0, tm * SUB)], buf_ref.at[k], sem).wait()
        gate = gate_ref[...]
        ss = jnp.zeros((tm, 1), F32)
        for c in range(SUB):
            lanes = slice(c * V7X_LANES, (c + 1) * V7X_LANES)
            piece = (h_ref[:, lanes] + gate[:, 0:1] * _load_row_tiles(buf_ref.at[0], tm, c)
                     + gate[:, 1:2] * _load_row_tiles(buf_ref.at[1], tm, c))
            o_ref[:, lanes] = piece
            ss = ss + jnp.sum(piece * piece, axis=-1, keepdims=True)
        o_ref[...] = o_ref[...] * lax.rsqrt(ss / d + EPS) * g_ref[...]

    @pl.when(i == 0)
    def _():
        start_gather((slot0_ref, slot1_ref), buf_a_ref, sems.at[0])

    bufs = (buf_a_ref, buf_b_ref)
    for parity in range(2):
        @pl.when(i % 2 == parity)
        def _():
            @pl.when(i + 1 < pl.num_programs(0))
            def _():
                start_gather((nslot0_ref, nslot1_ref), bufs[1 - parity], sems.at[1 - parity])

            finish(bufs[parity], sems.at[parity])


def _combine(h2, gates, slots, ys, g, tm=1024):
    n, d = h2.shape
    tm = min(tm, n)
    n_steps = n // tm
    slot_spec = pl.BlockSpec((tm,), lambda i: (i,), memory_space=pltpu.SMEM)
    next_spec = pl.BlockSpec((tm,), lambda i: (jnp.minimum(i + 1, n_steps - 1),),
                             memory_space=pltpu.SMEM)
    return pl.pallas_call(
        _combine_kernel,
        grid=(n_steps,),
        in_specs=[
            slot_spec, slot_spec, next_spec, next_spec,
            pl.BlockSpec((tm, d), lambda i: (i, 0)),
            pl.BlockSpec((tm, TOP_K), lambda i: (i, 0)),
            _const_spec((1, d)),
            pl.BlockSpec(memory_space=pl.ANY),
        ],
        out_specs=pl.BlockSpec((tm, d), lambda i: (i, 0)),
        out_shape=jax.ShapeDtypeStruct((n, d), F32),
        scratch_shapes=[pltpu.VMEM((TOP_K, tm * SUB, V7X_LANES), F32),
                        pltpu.VMEM((TOP_K, tm * SUB, V7X_LANES), F32),
                        pltpu.SemaphoreType.DMA((2,))],
        compiler_params=_cparams(1),
        name="combine",
    )(slots[0], slots[1], slots[0], slots[1], h2, gates, g.reshape(1, d), ys)


def _moe(h2, g_ffn, w_router, w1, w3, w2, g_final, tile=1024, sub=256):
    n, d = h2.shape
    n_experts = w_router.shape[1]
    tile = min(tile, n)
    n_tiles = TOP_K * n // tile + n_experts
    hn, meta, gates, counts = _router(h2, g_ffn, w_router)
    expert, rank = meta[0:TOP_K], meta[TOP_K:2 * TOP_K]
    counts = counts[0, :n_experts]

    tiles_per = (counts + tile - 1) // tile
    ends = jnp.cumsum(tiles_per)
    starts = ends - tiles_per
    used = ends[-1:]
    expert, slots = expert.reshape(-1, V7X_LANES), rank.reshape(-1, V7X_LANES)
    for e in range(n_experts):
        slots = slots + jnp.where(expert == e, starts[e] * tile, 0)
    slots = slots.astype(I32).reshape(TOP_K, n)
    pad_start = (starts * tile + counts).astype(I32)
    pad_len = (tiles_per * tile - counts).astype(I32)
    tidx = jnp.minimum(jnp.arange(n_tiles, dtype=I32), used - 1)
    tile_exp = jnp.sum((tidx[:, None] >= ends[None, :]).astype(I32), axis=1)
    sub = min(sub, tile)
    tile_rows = jnp.clip(counts[tile_exp] - (tidx - starts[tile_exp]) * tile, 0, tile)
    tile_nsub = (tile_rows + sub - 1) // sub

    xs = _dispatch(hn, slots, pad_start, pad_len, used.astype(I32), n_tiles, tile)
    ys = _grouped(xs, w1, w3, w2, tile_exp.astype(I32), used.astype(I32), tile_nsub.astype(I32),
                  tile, sub)
    return _combine(h2, gates, slots, ys, g_final)


def kernel(x, ev_norm_mix, ev_w_in, ev_conv_w, hgrn_lower_bounds, ev_gnorm, ev_w_out,
           ev_norm_ffn, ev_ffn_w1, ev_ffn_w3, ev_ffn_w2,
           od_norm_mix, od_w_pw1, od_b_pw1, od_w_dw, od_b_dw, od_ln_g, od_ln_b,
           od_w_pw2, od_b_pw2, od_norm_ffn, od_router, od_moe_w1, od_moe_w3, od_moe_w2,
           final_norm):
    bsz, seq, d = x.shape
    assert ev_w_in.shape[0] == 1 and od_w_pw1.shape[0] == 1
    n_experts, _, n_ff = od_moe_w1.shape[1:]
    flat = lambda w: w.reshape(-1, w.shape[-1])
    h, (w1b, f1b, f3b, f2b, pw1b, pw2b) = _mixer0(
        x, ev_norm_mix[0], ev_w_in[0], ev_conv_w[0], hgrn_lower_bounds, ev_gnorm[0], ev_w_out[0],
        layer=0, cast_ws=[flat(od_moe_w1[0]), ev_ffn_w1[0], ev_ffn_w3[0], ev_ffn_w2[0],
                          od_w_pw1[0], od_w_pw2[0]])
    h, w3b = _ffn(h.reshape(bsz * seq, d), ev_norm_ffn[0], f1b, f3b, f2b,
                  cast_w=flat(od_moe_w3[0]))
    h, w2b = _conformer(h.reshape(bsz, seq, d), od_norm_mix[0], pw1b, od_b_pw1[0],
                        od_w_dw[0], od_b_dw[0], od_ln_g[0], od_ln_b[0], pw2b, od_b_pw2[0],
                        cast_w=flat(od_moe_w2[0]))
    out = _moe(h.reshape(bsz * seq, d), od_norm_ffn[0], od_router[0],
               w1b.reshape(n_experts, d, n_ff), w3b.reshape(n_experts, d, n_ff),
               w2b.reshape(n_experts, n_ff, d), final_norm)
    return out.reshape(bsz, seq, d)
```

```python
import functools

import jax
import jax.numpy as jnp
from jax import lax
from jax.experimental import pallas as pl
from jax.experimental.pallas import tpu as pltpu

F32 = jnp.float32
BF16 = jnp.bfloat16
I32 = jnp.int32

EPS = 1e-6
CHUNK = 64
CHUNK_SLOTS = 2
PROJ_PIECE = 256
SUB = 8
HEAD_DIM = 128
TOP_K = 2

V7X_LANES = 128
V7X_MXU_DIM = 256
V7X_VMEM_BYTES = 64 * 1024 * 1024
VMEM_LIMIT = V7X_VMEM_BYTES - 8 * 1024 * 1024


def _cparams(n_axes):
    return pltpu.CompilerParams(
        dimension_semantics=("arbitrary",) * n_axes, vmem_limit_bytes=VMEM_LIMIT)


def _const_spec(shape):
    nd = len(shape)
    return pl.BlockSpec(shape, lambda *_: (0,) * nd, pipeline_mode=pl.Buffered(1))


def _rms(x, g):
    return x * lax.rsqrt(jnp.mean(x * x, axis=-1, keepdims=True) + EPS) * g


def _silu(x):
    return x * jax.nn.sigmoid(x)


def _dot(a, b):
    return jnp.dot(a, b, preferred_element_type=F32)


def _dot_nt(a, b):
    return lax.dot_general(a, b, (((1,), (1,)), ((), ())), preferred_element_type=F32)


def _store_row_tiles(ref, val):
    m = val.shape[0]
    for c in range(SUB):
        ref[pl.ds(c, m, stride=SUB), :] = val[:, c * V7X_LANES:(c + 1) * V7X_LANES]


def _load_row_tiles(ref, m, c):
    return ref[pl.ds(c, m, stride=SUB), :]


def _row_tile(ref, row):
    return ref.at[pl.ds(pl.multiple_of(row * SUB, SUB), SUB)]


ROW_COPY_UNROLL = 8


def _start_row_copies(n_rows, row_copy):
    def body(i, carry):
        for u in range(ROW_COPY_UNROLL):
            for k in range(TOP_K):
                row_copy(i * ROW_COPY_UNROLL + u, k).start(priority=(u * TOP_K + k) % 2)
        return carry

    lax.fori_loop(0, n_rows // ROW_COPY_UNROLL, body, 0)


def _mixer0_tile(x_ref, xn_ref, gmix_ref, win_ref, convw_ref, lbp_ref, gn_ref, wout_ref, o_ref,
                 proj_ref, next_ref, zbuf_ref, st_ref, y_ref, w_ref, s_ref, hn_ref, *, layer,
                 conv_dim, width):
    tt = x_ref.shape[0]
    heads = width // HEAD_DIM
    c3 = 3 * conv_dim

    x = x_ref[...]
    hn_ref[...] = _rms(xn_ref[...], gmix_ref[...]).astype(BF16)
    n_in = win_ref.shape[1]
    pieces = [slice(c0, min(c0 + PROJ_PIECE, n_in)) for c0 in range(0, n_in, PROJ_PIECE)]

    def project_next(cols):
        next_ref[:, cols] = _dot(hn_ref[...], win_ref[:, cols])

    z = proj_ref[:, 2 * conv_dim:c3] * proj_ref[:, 0:conv_dim]
    zbuf_ref[SUB:SUB + tt, :] = z
    cw = convw_ref[...]
    conv = (cw[2:3, :] * z + cw[1:2, :] * zbuf_ref[SUB - 1:SUB - 1 + tt, :]
            + cw[0:1, :] * zbuf_ref[SUB - 2:SUB - 2 + tt, :])
    y_ref[:, 0:conv_dim] = (proj_ref[:, conv_dim:2 * conv_dim] * conv).astype(BF16)
    zbuf_ref[0:SUB, :] = zbuf_ref[tt:tt + SUB, :]

    lbp = lbp_ref[...]
    lbe = jnp.exp(lbp - jnp.max(lbp, axis=0, keepdims=True))
    lb = jnp.sum(lbe[0:layer + 1, :], axis=0, keepdims=True) / jnp.sum(lbe, axis=0, keepdims=True)

    row = lax.broadcasted_iota(I32, (CHUNK, width), 0)
    row8 = lax.broadcasted_iota(I32, (SUB, width), 0)
    ti = lax.broadcasted_iota(I32, (CHUNK, CHUNK), 0)
    si = lax.broadcasted_iota(I32, (CHUNK, CHUNK), 1)
    tri = (si <= ti).astype(BF16)
    bdr = lax.broadcasted_iota(I32, (V7X_MXU_DIM, V7X_MXU_DIM), 0) // HEAD_DIM
    bdc = lax.broadcasted_iota(I32, (V7X_MXU_DIM, V7X_MXU_DIM), 1) // HEAD_DIM
    head_ones = (bdr == bdc).astype(BF16)
    gn = gn_ref[...]

    def bcast_row(val, r, n):
        return jnp.broadcast_to(val[r:r + 1, :], (n, val.shape[1]))

    def chunk(r0, slot, fill):
        rows = slice(r0, r0 + CHUNK)
        ws_ref = w_ref.at[slot]
        q = proj_ref[rows, c3:c3 + width]
        f = proj_ref[rows, c3 + width:c3 + 2 * width]
        v = proj_ref[rows, c3 + 2 * width:c3 + 3 * width]
        og = proj_ref[rows, c3 + 3 * width:c3 + 4 * width]
        fg = lb + (1.0 - lb) * jax.nn.sigmoid(f)
        k = 1.0 - fg
        g_rest = jnp.log(fg)
        b = jnp.zeros((CHUNK, width), F32)
        for _ in range(3):
            g_part = g_rest.astype(BF16)
            b = b + _dot(tri, g_part)
            g_rest = g_rest - g_part.astype(F32)
        b_last = b[CHUNK - 1:CHUNK, :]
        v16 = v.astype(BF16)
        fill()

        q_s = (q * jnp.exp(b)).astype(BF16)
        k_s = (k * jnp.exp(b_last - b)).astype(BF16)
        e_last = jnp.exp(b_last)

        p_acc = [jnp.zeros((CHUNK, CHUNK), F32) for _ in range(heads)]
        half = CHUNK // 2
        while half >= SUB:
            blk = 2 * half
            anc = jnp.concatenate(
                [bcast_row(b, i * blk + half, blk) for i in range(CHUNK // blk)], axis=0)
            upper = (row % blk) >= half
            q_l = (q * jnp.where(upper, jnp.exp(b - anc), 0.0)).astype(BF16)
            k_l = (k * jnp.where(upper, 0.0, jnp.exp(anc - b))).astype(BF16)
            same = (ti // blk) == (si // blk)
            for h in range(heads):
                hs = slice(h * HEAD_DIM, (h + 1) * HEAD_DIM)
                p_acc[h] = p_acc[h] + jnp.where(same, _dot_nt(q_l[:, hs], k_l[:, hs]), 0.0)
            half //= 2
        fill()

        for rb in range(CHUNK // SUB):
            if rb == CHUNK // SUB // 2:
                fill()
            rs = slice(rb * SUB, (rb + 1) * SUB)
            b8, q8, k8 = b[rs, :], q[rs, :], k[rs, :]
            for s in range(0, SUB, 2):
                pair = []
                for s1 in (s, s + 1):
                    dec = jnp.where(row8 >= s1, jnp.exp(b8 - bcast_row(b8, s1, SUB)), 0.0)
                    pair.append(q8 * dec * bcast_row(k8, s1, SUB))
                ws_ref[(rb * SUB + s) * SUB:(rb * SUB + s + 2) * SUB, :] = (
                    jnp.concatenate(pair, axis=0).astype(BF16))
        sc_ref = s_ref.at[slot]
        for j in range(width // V7X_MXU_DIM):
            lanes = slice(j * V7X_MXU_DIM, (j + 1) * V7X_MXU_DIM)
            sc_ref[:, lanes] = _dot(ws_ref[:, lanes], head_ones)
        o_diag = []
        for rb in range(CHUNK // SUB):
            sc = sc_ref[rb * SUB * SUB:(rb + 1) * SUB * SUB, :]
            v8 = v[rb * SUB:(rb + 1) * SUB, :]
            acc = None
            for s in range(SUB):
                term = sc[s * SUB:(s + 1) * SUB, :] * bcast_row(v8, s, SUB)
                acc = term if acc is None else acc + term
            o_diag.append(acc)
        o_diag = jnp.concatenate(o_diag, axis=0)
        fill()

        outs = []
        for h in range(heads):
            hs = slice(h * HEAD_DIM, (h + 1) * HEAD_DIM)
            st = st_ref[h]
            o_h = (_dot_nt(q_s[:, hs], st.astype(BF16)) + _dot(p_acc[h].astype(BF16), v16[:, hs])
                   + o_diag[:, hs])
            st_ref[h] = e_last[:, hs] * st + _dot(v16[:, hs].T, k_s[:, hs])
            o_n = o_h * lax.rsqrt(jnp.mean(o_h * o_h, axis=-1, keepdims=True) + EPS)
            outs.append(o_n)
        o = jnp.concatenate(outs, axis=1) * gn * _silu(og)
        y_ref[rows, conv_dim:conv_dim + width] = o.astype(BF16)

    todo = iter(pieces)

    def fill():
        cols = next(todo, None)
        if cols is not None:
            project_next(cols)

    for c in range(tt // CHUNK):
        chunk(c * CHUNK, c % CHUNK_SLOTS, fill)
    for cols in todo:
        project_next(cols)
    o_ref[...] = x + _dot(y_ref[...], wout_ref[...])


def _mixer0_kernel(*refs, n_cast, tiles_per_seq, **static):
    x_ref, xn_ref, gmix_ref, win_ref, convw_ref, lbp_ref, gn_ref, wout_ref = refs[:8]
    cast_refs = refs[8:8 + n_cast]
    o_ref = refs[8 + n_cast]
    cast_o_refs = refs[9 + n_cast:9 + 2 * n_cast]
    proj_a_ref, proj_b_ref, zbuf_ref, st_ref, y_ref, w_ref, s_ref, hn_ref = refs[9 + 2 * n_cast:]
    i = pl.program_id(0)

    @pl.when(i == 0)
    def _():
        proj_a_ref[...] = _dot(_rms(x_ref[...], gmix_ref[...]).astype(BF16), win_ref[...])

    @pl.when(i % tiles_per_seq == 0)
    def _():
        st_ref[...] = jnp.zeros_like(st_ref)
        zbuf_ref[0:SUB, :] = jnp.zeros((SUB, zbuf_ref.shape[1]), F32)

    for parity, (cur, nxt) in enumerate(((proj_a_ref, proj_b_ref), (proj_b_ref, proj_a_ref))):
        @pl.when(i % 2 == parity)
        def _():
            _mixer0_tile(x_ref, xn_ref, gmix_ref, win_ref, convw_ref, lbp_ref, gn_ref, wout_ref,
                         o_ref, cur, nxt, zbuf_ref, st_ref, y_ref, w_ref, s_ref, hn_ref, **static)

    for src, dst in zip(cast_refs, cast_o_refs):
        dst[...] = src[...].astype(BF16)


BF16_ROWS = 16


def _cast_job(w2d, n_steps, step_of):
    rows, cols = w2d.shape
    n_blocks = max(n for n in range(1, n_steps + 1)
                   if rows % n == 0 and (rows // n) % BF16_ROWS == 0)
    spec = pl.BlockSpec((rows // n_blocks, cols),
                        lambda *ids: (jnp.minimum(step_of(*ids), n_blocks - 1), 0))
    return spec, jax.ShapeDtypeStruct(w2d.shape, BF16)


def _mixer0(x, gmix, w_in, conv_w, lbp, gnorm, w_out, layer, cast_ws, tt=256):
    bsz, seq, d = x.shape
    conv_dim = conv_w.shape[1]
    width = lbp.shape[1]
    n_in = w_in.shape[1]
    tt = min(tt, seq)
    heads = width // HEAD_DIM
    gn = jnp.tile(gnorm.reshape(1, HEAD_DIM), (1, heads))
    n_t = seq // tt
    n_tiles = bsz * n_t
    x2 = x.reshape(bsz * seq, d)
    cast_specs, cast_shapes = zip(*[_cast_job(w, n_tiles, lambda i: i) for w in cast_ws])
    kern = functools.partial(_mixer0_kernel, n_cast=len(cast_ws), tiles_per_seq=n_t, layer=layer,
                             conv_dim=conv_dim, width=width)
    out, *cast_out = pl.pallas_call(
        kern,
        grid=(n_tiles,),
        in_specs=[
            pl.BlockSpec((tt, d), lambda i: (i, 0)),
            pl.BlockSpec((tt, d), lambda i: (jnp.minimum(i + 1, n_tiles - 1), 0)),
            _const_spec((1, d)),
            _const_spec((d, n_in)),
            _const_spec(conv_w.shape),
            _const_spec(lbp.shape),
            _const_spec((1, width)),
            _const_spec(w_out.shape),
            *cast_specs,
        ],
        out_specs=[pl.BlockSpec((tt, d), lambda i: (i, 0)), *cast_specs],
        out_shape=[jax.ShapeDtypeStruct(x2.shape, F32), *cast_shapes],
        scratch_shapes=[
            pltpu.VMEM((tt, n_in), F32),
            pltpu.VMEM((tt, n_in), F32),
            pltpu.VMEM((tt + SUB, conv_dim), F32),
            pltpu.VMEM((heads, HEAD_DIM, HEAD_DIM), F32),
            pltpu.VMEM((tt, conv_dim + width), BF16),
            pltpu.VMEM((CHUNK_SLOTS, CHUNK * SUB, width), BF16),
            pltpu.VMEM((CHUNK_SLOTS, CHUNK * SUB, width), F32),
            pltpu.VMEM((tt, d), BF16),
        ],
        compiler_params=_cparams(1),
        name="mixer0",
    )(x2, x2, gmix.reshape(1, d), w_in.astype(BF16), conv_w, lbp, gn, w_out.astype(BF16), *cast_ws)
    return out.reshape(x.shape), cast_out


def _ffn_kernel(x_ref, g_ref, w1_ref, w3_ref, w2_ref, cast_ref, o_ref, cast_o_ref, *, tf):
    x = x_ref[...]
    hn = _rms(x, g_ref[...]).astype(BF16)
    acc = x
    for j in range(w1_ref.shape[1] // tf):
        cols = slice(j * tf, (j + 1) * tf)
        h = _silu(_dot(hn, w1_ref[:, cols])) * _dot(hn, w3_ref[:, cols])
        acc = acc + _dot(h.astype(BF16), w2_ref[cols, :])
    o_ref[...] = acc
    cast_o_ref[...] = cast_ref[...].astype(BF16)


def _ffn(x2, g, w1, w3, w2, cast_w, tm=512, tf=256):
    n, d = x2.shape
    tm = min(tm, n)
    cast_spec, cast_shape = _cast_job(cast_w, n // tm, lambda i: i)
    return pl.pallas_call(
        functools.partial(_ffn_kernel, tf=tf),
        grid=(n // tm,),
        in_specs=[
            pl.BlockSpec((tm, d), lambda i: (i, 0)),
            _const_spec((1, d)),
            _const_spec(w1.shape),
            _const_spec(w3.shape),
            _const_spec(w2.shape),
            cast_spec,
        ],
        out_specs=[pl.BlockSpec((tm, d), lambda i: (i, 0)), cast_spec],
        out_shape=[jax.ShapeDtypeStruct((n, d), F32), cast_shape],
        compiler_params=_cparams(1),
        name="ffn0",
    )(x2, g.reshape(1, d), w1, w3, w2, cast_w)


def _conformer_kernel(x_ref, g_ref, w1_ref, b1_ref, wdw_ref, bdw_ref, lng_ref, lnb_ref,
                      w2_ref, b2_ref, cast_ref, o_ref, cast_o_ref, ubuf_ref, conv_ref, *, halo):
    tt = x_ref.shape[1]
    d = x_ref.shape[2]
    kw = wdw_ref.shape[0]

    @pl.when(pl.program_id(1) == 0)
    def _():
        ubuf_ref[0:halo, :] = jnp.zeros((halo, d), F32)
        ubuf_ref[halo + tt:halo + tt + SUB, :] = jnp.zeros((SUB, d), F32)

    x = x_ref[0]
    hn = _rms(x, g_ref[...]).astype(BF16)
    p = _dot(hn, w1_ref[...]) + b1_ref[...]
    ubuf_ref[halo:halo + tt, :] = p[:, 0:d] * jax.nn.sigmoid(p[:, d:2 * d])
    off0 = halo - (kw - 1)
    seg = min(tt, 128)
    for c in range(d // V7X_LANES):
        lanes = slice(c * V7X_LANES, (c + 1) * V7X_LANES)
        wcol = wdw_ref[:, lanes]
        for base in range(0, tt, seg):
            acc = None
            for r in range(SUB):
                part = None
                for a in range((off0 + kw - 1) // SUB + 1):
                    k = SUB * a + r - off0
                    if 0 <= k < kw:
                        win = ubuf_ref[base + SUB * a:base + SUB * a + seg + SUB, lanes]
                        term = wcol[k:k + 1, :] * win
                        part = term if part is None else part + term
                if part is not None:
                    shifted = part[r:r + seg, :]
                    acc = shifted if acc is None else acc + shifted
            conv_ref[base:base + seg, lanes] = acc + bdw_ref[:, lanes]
    acc = conv_ref[...]
    ubuf_ref[0:halo, :] = ubuf_ref[tt:tt + halo, :]
    mu = jnp.mean(acc, axis=-1, keepdims=True)
    xc = acc - mu
    var = jnp.mean(xc * xc, axis=-1, keepdims=True)
    u = _silu(xc * lax.rsqrt(var + EPS) * lng_ref[...] + lnb_ref[...])
    o_ref[0] = x + _dot(u.astype(BF16), w2_ref[...]) + b2_ref[...]
    cast_o_ref[...] = cast_ref[...].astype(BF16)


def _conformer(x, g, w_pw1, b_pw1, w_dw, b_dw, ln_g, ln_b, w_pw2, b_pw2, cast_w, tt=256):
    bsz, seq, d = x.shape
    tt = min(tt, seq)
    kw = w_dw.shape[0]
    halo = -(-(kw - 1) // SUB) * SUB
    r = lambda a: a.reshape(1, -1)
    n_t = seq // tt
    cast_spec, cast_shape = _cast_job(cast_w, bsz * n_t, lambda b, t: b * n_t + t)
    return pl.pallas_call(
        functools.partial(_conformer_kernel, halo=halo),
        grid=(bsz, seq // tt),
        in_specs=[
            pl.BlockSpec((1, tt, d), lambda b, t: (b, t, 0)),
            _const_spec((1, d)),
            _const_spec(w_pw1.shape),
            _const_spec((1, 2 * d)),
            _const_spec(w_dw.shape),
            _const_spec((1, d)),
            _const_spec((1, d)),
            _const_spec((1, d)),
            _const_spec(w_pw2.shape),
            _const_spec((1, d)),
            cast_spec,
        ],
        out_specs=[pl.BlockSpec((1, tt, d), lambda b, t: (b, t, 0)), cast_spec],
        out_shape=[jax.ShapeDtypeStruct(x.shape, F32), cast_shape],
        scratch_shapes=[pltpu.VMEM((halo + tt + SUB, d), F32), pltpu.VMEM((tt, d), F32)],
        compiler_params=_cparams(2),
        name="conformer",
    )(x, r(g), w_pw1, r(b_pw1), w_dw, r(b_dw), r(ln_g), r(ln_b), w_pw2, r(b_pw2), cast_w)


def _router_kernel(x_ref, g_ref, wr_ref, hn_ref, meta_ref, gate_ref, cnt_ref, run_ref,
                   *, n_experts):
    tm = x_ref.shape[0]

    @pl.when(pl.program_id(0) == 0)
    def _():
        run_ref[...] = jnp.zeros_like(run_ref)

    hn = _rms(x_ref[...], g_ref[...])
    _store_row_tiles(hn_ref, hn)
    wr = wr_ref[...]
    hn_hi, wr_hi = hn.astype(BF16), wr.astype(BF16)
    hn_lo = (hn - hn_hi.astype(F32)).astype(BF16)
    wr_lo = (wr - wr_hi.astype(F32)).astype(BF16)
    logits = _dot(hn_hi, wr_hi) + (_dot(hn_lo, wr_hi) + _dot(hn_hi, wr_lo))
    lane = lax.broadcasted_iota(I32, logits.shape, 1)
    neg = jnp.float32(-jnp.inf)
    big = jnp.int32(V7X_LANES)
    logits = jnp.where(lane < n_experts, logits, neg)
    m1 = jnp.max(logits, axis=-1, keepdims=True)
    i1 = jnp.min(jnp.where(logits == m1, lane, big), axis=-1, keepdims=True)
    rest = jnp.where(lane == i1, neg, logits)
    m2 = jnp.max(rest, axis=-1, keepdims=True)
    i2 = jnp.min(jnp.where(rest == m2, lane, big), axis=-1, keepdims=True)
    e2 = jnp.exp(m2 - m1)
    g1 = 1.0 / (1.0 + e2)
    g2 = e2 / (1.0 + e2)

    sel1 = lane == i1
    sel2 = lane == i2
    onehot = jnp.where(sel1 | sel2, 1.0, 0.0)
    ti = lax.broadcasted_iota(I32, (tm, tm), 0)
    si = lax.broadcasted_iota(I32, (tm, tm), 1)
    before = _dot((si < ti).astype(BF16), onehot.astype(BF16)) + run_ref[...]
    r1 = jnp.sum(jnp.where(sel1, before, 0.0), axis=-1, keepdims=True)
    r2 = jnp.sum(jnp.where(sel2, before, 0.0), axis=-1, keepdims=True)
    run_ref[...] = run_ref[...] + jnp.sum(onehot, axis=0, keepdims=True)

    meta = jnp.where(lane == 0, i1.astype(F32), jnp.where(lane == 1, i2.astype(F32),
                     jnp.where(lane == 2, r1, jnp.where(lane == 3, r2, 0.0))))
    meta_ref[...] = meta.T[0:SUB, :].astype(I32)
    col = lax.broadcasted_iota(I32, (tm, TOP_K), 1)
    gate_ref[...] = jnp.where(col == 0, g1, g2)
    cnt_ref[...] = run_ref[...].astype(I32)


def _router(x2, g, w_router, tm=512):
    n, d = x2.shape
    assert d == SUB * V7X_LANES
    tm = min(tm, n)
    n_experts = w_router.shape[1]
    wr = jnp.pad(w_router, ((0, 0), (0, V7X_LANES - n_experts)))
    return pl.pallas_call(
        functools.partial(_router_kernel, n_experts=n_experts),
        grid=(n // tm,),
        in_specs=[
            pl.BlockSpec((tm, d), lambda i: (i, 0)),
            _const_spec((1, d)),
            _const_spec((d, V7X_LANES)),
        ],
        out_specs=[
            pl.BlockSpec((tm * SUB, V7X_LANES), lambda i: (i, 0)),
            pl.BlockSpec((SUB, tm), lambda i: (0, i)),
            pl.BlockSpec((tm, TOP_K), lambda i: (i, 0)),
            pl.BlockSpec((1, V7X_LANES), lambda i: (0, 0)),
        ],
        out_shape=[
            jax.ShapeDtypeStruct((n * SUB, V7X_LANES), F32),
            jax.ShapeDtypeStruct((SUB, n), I32),
            jax.ShapeDtypeStruct((n, TOP_K), F32),
            jax.ShapeDtypeStruct((1, V7X_LANES), I32),
        ],
        scratch_shapes=[pltpu.VMEM((1, V7X_LANES), F32)],
        compiler_params=_cparams(1),
        name="router",
    )(x2, g.reshape(1, d), wr)


def _dispatch_kernel(slot0_ref, slot1_ref, pstart_ref, plen_ref, used_ref, hn_ref, xs_ref,
                     zero_ref, sem, zsem, *, n_experts, tile):
    tm = hn_ref.shape[0] // SUB
    n_tiles = xs_ref.shape[0] // (tile * SUB)
    slot_refs = (slot0_ref, slot1_ref)

    def row_copy(r, k):
        return pltpu.make_async_copy(
            _row_tile(hn_ref, r), _row_tile(xs_ref, slot_refs[k][r]), sem)

    _start_row_copies(tm, row_copy)
    for k in range(TOP_K):
        pltpu.make_async_copy(hn_ref, xs_ref.at[pl.ds(0, tm * SUB)], sem).wait()

    def rows(ref, start, count):
        return ref.at[pl.ds(pl.multiple_of(start * SUB, SUB), count * SUB)]

    def for_each_pad_copy(act):
        for e in range(n_experts):
            cur = pstart_ref[e]
            length = plen_ref[e]
            p = 1
            while p < tile:
                take = (length & p) != 0

                @pl.when(take)
                def _():
                    act(pltpu.make_async_copy(rows(zero_ref, 0, p), rows(xs_ref, cur, p), zsem))
                cur = cur + jnp.where(take, p, 0)
                p *= 2
        for i in range(n_experts):
            t = used_ref[0] + i

            @pl.when(t < n_tiles)
            def _():
                act(pltpu.make_async_copy(zero_ref, rows(xs_ref, t * tile, tile), zsem))

    @pl.when(pl.program_id(0) == pl.num_programs(0) - 1)
    def _():
        zero_ref[...] = jnp.zeros_like(zero_ref)
        for_each_pad_copy(lambda cp: cp.start())
        for_each_pad_copy(lambda cp: cp.wait())


def _dispatch(hn, slots, pad_start, pad_len, used_tiles, n_tiles, tile, tm=1024):
    n = hn.shape[0] // SUB
    tm = min(tm, n)
    n_experts = pad_start.shape[0]
    smem = pl.BlockSpec(memory_space=pltpu.SMEM)
    slot_spec = pl.BlockSpec((tm,), lambda i: (i,), memory_space=pltpu.SMEM)
    return pl.pallas_call(
        functools.partial(_dispatch_kernel, n_experts=n_experts, tile=tile),
        grid=(n // tm,),
        in_specs=[
            slot_spec, slot_spec,
            smem, smem, smem,
            pl.BlockSpec((tm * SUB, V7X_LANES), lambda i: (i, 0)),
        ],
        out_specs=pl.BlockSpec(memory_space=pl.ANY),
        out_shape=jax.ShapeDtypeStruct((n_tiles * tile * SUB, V7X_LANES), F32),
        scratch_shapes=[
            pltpu.VMEM((tile * SUB, V7X_LANES), F32),
            pltpu.SemaphoreType.DMA(()),
            pltpu.SemaphoreType.DMA(()),
        ],
        compiler_params=_cparams(1),
        name="dispatch",
    )(slots[0], slots[1], pad_start, pad_len, used_tiles, hn)


def _grouped_kernel(exp_ref, used_ref, nsub_ref, x_ref, w1_ref, w3_ref, w2_ref, o_ref,
                    xb_ref, acc_ref, *, sub):
    i = pl.program_id(0)
    j = pl.program_id(1)
    last = pl.num_programs(1) - 1
    valid = i < used_ref[0]
    tile = acc_ref.shape[0]

    @pl.when(valid & (j == 0))
    def _():
        for c in range(SUB):
            xb_ref[:, c * V7X_LANES:(c + 1) * V7X_LANES] = _load_row_tiles(x_ref, tile, c).astype(BF16)
        acc_ref[...] = jnp.zeros_like(acc_ref)

    @pl.when(valid)
    def _():
        def swiglu_rows(n_rows):
            rows = slice(0, n_rows)
            xb = xb_ref[rows, :]
            h = _silu(_dot(xb, w1_ref[...])) * _dot(xb, w3_ref[...])
            acc_ref[rows, :] += _dot(h.astype(BF16), w2_ref[...])

        for n_sub in range(1, tile // sub + 1):
            @pl.when(nsub_ref[i] == n_sub)
            def _():
                swiglu_rows(n_sub * sub)

    @pl.when(valid & (j == last))
    def _():
        _store_row_tiles(o_ref, acc_ref[...])

    @pl.when(jnp.logical_not(valid) & (j == last))
    def _():
        o_ref[...] = jnp.zeros_like(o_ref)


def _grouped(xs, w1, w3, w2, tile_exp, used_tiles, tile_nsub, tile, sub, tf=512):
    d = w1.shape[1]
    n_tiles = xs.shape[0] // (tile * SUB)
    nj = w1.shape[2] // tf

    def jsel(i, j, used):
        return jnp.where(i < used[0], j, nj - 1)

    grid_spec = pltpu.PrefetchScalarGridSpec(
        num_scalar_prefetch=3,
        grid=(n_tiles, nj),
        in_specs=[
            pl.BlockSpec((tile * SUB, V7X_LANES), lambda i, j, ex, us, ns: (i, 0)),
            pl.BlockSpec((None, d, tf), lambda i, j, ex, us, ns: (ex[i], 0, jsel(i, j, us))),
            pl.BlockSpec((None, d, tf), lambda i, j, ex, us, ns: (ex[i], 0, jsel(i, j, us))),
            pl.BlockSpec((None, tf, d), lambda i, j, ex, us, ns: (ex[i], jsel(i, j, us), 0)),
        ],
        out_specs=pl.BlockSpec((tile * SUB, V7X_LANES), lambda i, j, ex, us, ns: (i, 0)),
        scratch_shapes=[pltpu.VMEM((tile, d), BF16), pltpu.VMEM((tile, d), F32)],
    )
    return pl.pallas_call(
        functools.partial(_grouped_kernel, sub=sub),
        grid_spec=grid_spec,
        out_shape=jax.ShapeDtypeStruct(xs.shape, F32),
        compiler_params=_cparams(2),
        name="grouped_swiglu",
    )(tile_exp, used_tiles, tile_nsub, xs, w1, w3, w2)


def _combine_kernel(slot0_ref, slot1_ref, nslot0_ref, nslot1_ref, h_ref, gate_ref, g_ref, ys_ref,
                    o_ref, buf_a_ref, buf_b_ref, sems):
    tm, d = h_ref.shape
    i = pl.program_id(0)

    def start_gather(slot_refs, buf_ref, sem):
        def row_copy(r, k):
            return pltpu.make_async_copy(
                _row_tile(ys_ref, slot_refs[k][r]), _row_tile(buf_ref.at[k], r), sem)

        _start_row_copies(tm, row_copy)

    def finish(buf_ref, sem):
        for k in range(TOP_K):
            pltpu.make_async_copy(ys_ref.at[pl.ds(0, tm * SUB)], buf_ref.at[k], sem).wait()
        gate = gate_ref[...]
        ss = jnp.zeros((tm, 1), F32)
        for c in range(SUB):
            lanes = slice(c * V7X_LANES, (c + 1) * V7X_LANES)
            piece = (h_ref[:, lanes] + gate[:, 0:1] * _load_row_tiles(buf_ref.at[0], tm, c)
                     + gate[:, 1:2] * _load_row_tiles(buf_ref.at[1], tm, c))
            o_ref[:, lanes] = piece
            ss = ss + jnp.sum(piece * piece, axis=-1, keepdims=True)
        o_ref[...] = o_ref[...] * lax.rsqrt(ss / d + EPS) * g_ref[...]

    @pl.when(i == 0)
    def _():
        start_gather((slot0_ref, slot1_ref), buf_a_ref, sems.at[0])

    bufs = (buf_a_ref, buf_b_ref)
    for parity in range(2):
        @pl.when(i % 2 == parity)
        def _():
            @pl.when(i + 1 < pl.num_programs(0))
            def _():
                start_gather((nslot0_ref, nslot1_ref), bufs[1 - parity], sems.at[1 - parity])

            finish(bufs[parity], sems.at[parity])


def _combine(h2, gates, slots, ys, g, tm=1024):
    n, d = h2.shape
    tm = min(tm, n)
    n_steps = n // tm
    slot_spec = pl.BlockSpec((tm,), lambda i: (i,), memory_space=pltpu.SMEM)
    next_spec = pl.BlockSpec((tm,), lambda i: (jnp.minimum(i + 1, n_steps - 1),),
                             memory_space=pltpu.SMEM)
    return pl.pallas_call(
        _combine_kernel,
        grid=(n_steps,),
        in_specs=[
            slot_spec, slot_spec, next_spec, next_spec,
            pl.BlockSpec((tm, d), lambda i: (i, 0)),
            pl.BlockSpec((tm, TOP_K), lambda i: (i, 0)),
            _const_spec((1, d)),
            pl.BlockSpec(memory_space=pl.ANY),
        ],
        out_specs=pl.BlockSpec((tm, d), lambda i: (i, 0)),
        out_shape=jax.ShapeDtypeStruct((n, d), F32),
        scratch_shapes=[pltpu.VMEM((TOP_K, tm * SUB, V7X_LANES), F32),
                        pltpu.VMEM((TOP_K, tm * SUB, V7X_LANES), F32),
                        pltpu.SemaphoreType.DMA((2,))],
        compiler_params=_cparams(1),
        name="combine",
    )(slots[0], slots[1], slots[0], slots[1], h2, gates, g.reshape(1, d), ys)


def _moe(h2, g_ffn, w_router, w1, w3, w2, g_final, tile=1024, sub=256):
    n, d = h2.shape
    n_experts = w_router.shape[1]
    tile = min(tile, n)
    n_tiles = TOP_K * n // tile + n_experts
    hn, meta, gates, counts = _router(h2, g_ffn, w_router)
    expert, rank = meta[0:TOP_K], meta[TOP_K:2 * TOP_K]
    counts = counts[0, :n_experts]

    tiles_per = (counts + tile - 1) // tile
    ends = jnp.cumsum(tiles_per)
    starts = ends - tiles_per
    used = ends[-1:]
    expert, slots = expert.reshape(-1, V7X_LANES), rank.reshape(-1, V7X_LANES)
    for e in range(n_experts):
        slots = slots + jnp.where(expert == e, starts[e] * tile, 0)
    slots = slots.astype(I32).reshape(TOP_K, n)
    pad_start = (starts * tile + counts).astype(I32)
    pad_len = (tiles_per * tile - counts).astype(I32)
    tidx = jnp.minimum(jnp.arange(n_tiles, dtype=I32), used - 1)
    tile_exp = jnp.sum((tidx[:, None] >= ends[None, :]).astype(I32), axis=1)
    sub = min(sub, tile)
    tile_rows = jnp.clip(counts[tile_exp] - (tidx - starts[tile_exp]) * tile, 0, tile)
    tile_nsub = (tile_rows + sub - 1) // sub

    xs = _dispatch(hn, slots, pad_start, pad_len, used.astype(I32), n_tiles, tile)
    ys = _grouped(xs, w1, w3, w2, tile_exp.astype(I32), used.astype(I32), tile_nsub.astype(I32),
                  tile, sub)
    return _combine(h2, gates, slots, ys, g_final)


def kernel(x, ev_norm_mix, ev_w_in, ev_conv_w, hgrn_lower_bounds, ev_gnorm, ev_w_out,
           ev_norm_ffn, ev_ffn_w1, ev_ffn_w3, ev_ffn_w2,
           od_norm_mix, od_w_pw1, od_b_pw1, od_w_dw, od_b_dw, od_ln_g, od_ln_b,
           od_w_pw2, od_b_pw2, od_norm_ffn, od_router, od_moe_w1, od_moe_w3, od_moe_w2,
           final_norm):
    bsz, seq, d = x.shape
    assert ev_w_in.shape[0] == 1 and od_w_pw1.shape[0] == 1
    n_experts, _, n_ff = od_moe_w1.shape[1:]
    flat = lambda w: w.reshape(-1, w.shape[-1])
    h, (w1b, f1b, f3b, f2b, pw1b, pw2b) = _mixer0(
        x, ev_norm_mix[0], ev_w_in[0], ev_conv_w[0], hgrn_lower_bounds, ev_gnorm[0], ev_w_out[0],
        layer=0, cast_ws=[flat(od_moe_w1[0]), ev_ffn_w1[0], ev_ffn_w3[0], ev_ffn_w2[0],
                          od_w_pw1[0], od_w_pw2[0]])
    h, w3b = _ffn(h.reshape(bsz * seq, d), ev_norm_ffn[0], f1b, f3b, f2b,
                  cast_w=flat(od_moe_w3[0]))
    h, w2b = _conformer(h.reshape(bsz, seq, d), od_norm_mix[0], pw1b, od_b_pw1[0],
                        od_w_dw[0], od_b_dw[0], od_ln_g[0], od_ln_b[0], pw2b, od_b_pw2[0],
                        cast_w=flat(od_moe_w2[0]))
    out = _moe(h.reshape(bsz * seq, d), od_norm_ffn[0], od_router[0],
               w1b.reshape(n_experts, d, n_ff), w3b.reshape(n_experts, d, n_ff),
               w2b.reshape(n_experts, n_ff, d), final_norm)
    return out.reshape(bsz, seq, d)
```

```python
import functools

import jax
import jax.numpy as jnp
from jax import lax
from jax.experimental import pallas as pl
from jax.experimental.pallas import tpu as pltpu

F32 = jnp.float32
BF16 = jnp.bfloat16
I32 = jnp.int32

EPS = 1e-6
CHUNK = 64
CHUNK_SLOTS = 2
PROJ_PIECE = 256
SUB = 8
HEAD_DIM = 128
TOP_K = 2

V7X_LANES = 128
V7X_MXU_DIM = 256
V7X_VMEM_BYTES = 64 * 1024 * 1024
VMEM_LIMIT = V7X_VMEM_BYTES - 8 * 1024 * 1024


def _cparams(n_axes):
    return pltpu.CompilerParams(
        dimension_semantics=("arbitrary",) * n_axes, vmem_limit_bytes=VMEM_LIMIT)


def _const_spec(shape):
    nd = len(shape)
    return pl.BlockSpec(shape, lambda *_: (0,) * nd, pipeline_mode=pl.Buffered(1))


def _rms(x, g):
    return x * lax.rsqrt(jnp.mean(x * x, axis=-1, keepdims=True) + EPS) * g


def _silu(x):
    return x * jax.nn.sigmoid(x)


def _dot(a, b):
    return jnp.dot(a, b, preferred_element_type=F32)


def _dot_nt(a, b):
    return lax.dot_general(a, b, (((1,), (1,)), ((), ())), preferred_element_type=F32)


def _store_row_tiles(ref, val):
    m = val.shape[0]
    for c in range(SUB):
        ref[pl.ds(c, m, stride=SUB), :] = val[:, c * V7X_LANES:(c + 1) * V7X_LANES]


def _load_row_tiles(ref, m, c):
    return ref[pl.ds(c, m, stride=SUB), :]


def _row_tile(ref, row):
    return ref.at[pl.ds(pl.multiple_of(row * SUB, SUB), SUB)]


ROW_COPY_UNROLL = 8
COMBINE_ROWS = 128


def _start_row_copies(n_rows, row_copy):
    def body(i, carry):
        for u in range(ROW_COPY_UNROLL):
            for k in range(TOP_K):
                row_copy(i * ROW_COPY_UNROLL + u, k).start(priority=(u * TOP_K + k) % 2)
        return carry

    lax.fori_loop(0, n_rows // ROW_COPY_UNROLL, body, 0)


def _mixer0_tile(x_ref, xn_ref, gmix_ref, win_ref, convw_ref, lbp_ref, gn_ref, wout_ref, o_ref,
                 proj_ref, next_ref, zbuf_ref, st_ref, y_ref, w_ref, s_ref, hn_ref, *, layer,
                 conv_dim, width):
    tt = x_ref.shape[0]
    heads = width // HEAD_DIM
    c3 = 3 * conv_dim

    x = x_ref[...]
    hn_ref[...] = _rms(xn_ref[...], gmix_ref[...]).astype(BF16)
    n_in = win_ref.shape[1]
    pieces = [slice(c0, min(c0 + PROJ_PIECE, n_in)) for c0 in range(0, n_in, PROJ_PIECE)]

    def project_next(cols):
        next_ref[:, cols] = _dot(hn_ref[...], win_ref[:, cols])

    z = proj_ref[:, 2 * conv_dim:c3] * proj_ref[:, 0:conv_dim]
    zbuf_ref[SUB:SUB + tt, :] = z
    cw = convw_ref[...]
    conv = (cw[2:3, :] * z + cw[1:2, :] * zbuf_ref[SUB - 1:SUB - 1 + tt, :]
            + cw[0:1, :] * zbuf_ref[SUB - 2:SUB - 2 + tt, :])
    y_ref[:, 0:conv_dim] = (proj_ref[:, conv_dim:2 * conv_dim] * conv).astype(BF16)
    zbuf_ref[0:SUB, :] = zbuf_ref[tt:tt + SUB, :]

    lbp = lbp_ref[...]
    lbe = jnp.exp(lbp - jnp.max(lbp, axis=0, keepdims=True))
    lb = jnp.sum(lbe[0:layer + 1, :], axis=0, keepdims=True) / jnp.sum(lbe, axis=0, keepdims=True)

    row = lax.broadcasted_iota(I32, (CHUNK, width), 0)
    row8 = lax.broadcasted_iota(I32, (SUB, width), 0)
    ti = lax.broadcasted_iota(I32, (CHUNK, CHUNK), 0)
    si = lax.broadcasted_iota(I32, (CHUNK, CHUNK), 1)
    tri = (si <= ti).astype(BF16)
    bdr = lax.broadcasted_iota(I32, (V7X_MXU_DIM, V7X_MXU_DIM), 0) // HEAD_DIM
    bdc = lax.broadcasted_iota(I32, (V7X_MXU_DIM, V7X_MXU_DIM), 1) // HEAD_DIM
    head_ones = (bdr == bdc).astype(BF16)
    gn = gn_ref[...]

    def bcast_row(val, r, n):
        return jnp.broadcast_to(val[r:r + 1, :], (n, val.shape[1]))

    def chunk(r0, slot, fill):
        rows = slice(r0, r0 + CHUNK)
        ws_ref = w_ref.at[slot]
        q = proj_ref[rows, c3:c3 + width]
        f = proj_ref[rows, c3 + width:c3 + 2 * width]
        v = proj_ref[rows, c3 + 2 * width:c3 + 3 * width]
        og = proj_ref[rows, c3 + 3 * width:c3 + 4 * width]
        fg = lb + (1.0 - lb) * jax.nn.sigmoid(f)
        k = 1.0 - fg
        g_rest = jnp.log(fg)
        b = jnp.zeros((CHUNK, width), F32)
        for _ in range(3):
            g_part = g_rest.astype(BF16)
            b = b + _dot(tri, g_part)
            g_rest = g_rest - g_part.astype(F32)
        b_last = b[CHUNK - 1:CHUNK, :]
        v16 = v.astype(BF16)
        fill()

        q_s = (q * jnp.exp(b)).astype(BF16)
        k_s = (k * jnp.exp(b_last - b)).astype(BF16)
        e_last = jnp.exp(b_last)

        p_acc = [jnp.zeros((CHUNK, CHUNK), F32) for _ in range(heads)]
        half = CHUNK // 2
        while half >= SUB:
            blk = 2 * half
            anc = jnp.concatenate(
                [bcast_row(b, i * blk + half, blk) for i in range(CHUNK // blk)], axis=0)
            upper = (row % blk) >= half
            q_l = (q * jnp.where(upper, jnp.exp(b - anc), 0.0)).astype(BF16)
            k_l = (k * jnp.where(upper, 0.0, jnp.exp(anc - b))).astype(BF16)
            same = (ti // blk) == (si // blk)
            for h in range(heads):
                hs = slice(h * HEAD_DIM, (h + 1) * HEAD_DIM)
                p_acc[h] = p_acc[h] + jnp.where(same, _dot_nt(q_l[:, hs], k_l[:, hs]), 0.0)
            half //= 2
        fill()

        for rb in range(CHUNK // SUB):
            if rb == CHUNK // SUB // 2:
                fill()
            rs = slice(rb * SUB, (rb + 1) * SUB)
            b8, q8, k8 = b[rs, :], q[rs, :], k[rs, :]
            for s in range(0, SUB, 2):
                pair = []
                for s1 in (s, s + 1):
                    dec = jnp.where(row8 >= s1, jnp.exp(b8 - bcast_row(b8, s1, SUB)), 0.0)
                    pair.append(q8 * dec * bcast_row(k8, s1, SUB))
                ws_ref[(rb * SUB + s) * SUB:(rb * SUB + s + 2) * SUB, :] = (
                    jnp.concatenate(pair, axis=0).astype(BF16))
        sc_ref = s_ref.at[slot]
        for j in range(width // V7X_MXU_DIM):
            lanes = slice(j * V7X_MXU_DIM, (j + 1) * V7X_MXU_DIM)
            sc_ref[:, lanes] = _dot(ws_ref[:, lanes], head_ones)
        o_diag = []
        for rb in range(CHUNK // SUB):
            sc = sc_ref[rb * SUB * SUB:(rb + 1) * SUB * SUB, :]
            v8 = v[rb * SUB:(rb + 1) * SUB, :]
            acc = None
            for s in range(SUB):
                term = sc[s * SUB:(s + 1) * SUB, :] * bcast_row(v8, s, SUB)
                acc = term if acc is None else acc + term
            o_diag.append(acc)
        o_diag = jnp.concatenate(o_diag, axis=0)
        fill()

        outs = []
        for h in range(heads):
            hs = slice(h * HEAD_DIM, (h + 1) * HEAD_DIM)
            st = st_ref[h]
            o_h = (_dot_nt(q_s[:, hs], st.astype(BF16)) + _dot(p_acc[h].astype(BF16), v16[:, hs])
                   + o_diag[:, hs])
            st_ref[h] = e_last[:, hs] * st + _dot(v16[:, hs].T, k_s[:, hs])
            o_n = o_h * lax.rsqrt(jnp.mean(o_h * o_h, axis=-1, keepdims=True) + EPS)
            outs.append(o_n)
        o = jnp.concatenate(outs, axis=1) * gn * _silu(og)
        y_ref[rows, conv_dim:conv_dim + width] = o.astype(BF16)

    todo = iter(pieces)

    def fill():
        cols = next(todo, None)
        if cols is not None:
            project_next(cols)

    for c in range(tt // CHUNK):
        chunk(c * CHUNK, c % CHUNK_SLOTS, fill)
    for cols in todo:
        project_next(cols)
    o_ref[...] = x + _dot(y_ref[...], wout_ref[...])


def _mixer0_kernel(*refs, n_cast, tiles_per_seq, **static):
    x_ref, xn_ref, gmix_ref, win_ref, convw_ref, lbp_ref, gn_ref, wout_ref = refs[:8]
    cast_refs = refs[8:8 + n_cast]
    o_ref = refs[8 + n_cast]
    cast_o_refs = refs[9 + n_cast:9 + 2 * n_cast]
    proj_a_ref, proj_b_ref, zbuf_ref, st_ref, y_ref, w_ref, s_ref, hn_ref = refs[9 + 2 * n_cast:]
    i = pl.program_id(0)

    @pl.when(i == 0)
    def _():
        proj_a_ref[...] = _dot(_rms(x_ref[...], gmix_ref[...]).astype(BF16), win_ref[...])

    @pl.when(i % tiles_per_seq == 0)
    def _():
        st_ref[...] = jnp.zeros_like(st_ref)
        zbuf_ref[0:SUB, :] = jnp.zeros((SUB, zbuf_ref.shape[1]), F32)

    for parity, (cur, nxt) in enumerate(((proj_a_ref, proj_b_ref), (proj_b_ref, proj_a_ref))):
        @pl.when(i % 2 == parity)
        def _():
            _mixer0_tile(x_ref, xn_ref, gmix_ref, win_ref, convw_ref, lbp_ref, gn_ref, wout_ref,
                         o_ref, cur, nxt, zbuf_ref, st_ref, y_ref, w_ref, s_ref, hn_ref, **static)

    for src, dst in zip(cast_refs, cast_o_refs):
        dst[...] = src[...].astype(BF16)


BF16_ROWS = 16


def _cast_job(w2d, n_steps, step_of):
    rows, cols = w2d.shape
    n_blocks = max(n for n in range(1, n_steps + 1)
                   if rows % n == 0 and (rows // n) % BF16_ROWS == 0)
    spec = pl.BlockSpec((rows // n_blocks, cols),
                        lambda *ids: (jnp.minimum(step_of(*ids), n_blocks - 1), 0))
    return spec, jax.ShapeDtypeStruct(w2d.shape, BF16)


def _mixer0(x, gmix, w_in, conv_w, lbp, gnorm, w_out, layer, cast_ws, tt=256):
    bsz, seq, d = x.shape
    conv_dim = conv_w.shape[1]
    width = lbp.shape[1]
    n_in = w_in.shape[1]
    tt = min(tt, seq)
    heads = width // HEAD_DIM
    gn = jnp.tile(gnorm.reshape(1, HEAD_DIM), (1, heads))
    n_t = seq // tt
    n_tiles = bsz * n_t
    x2 = x.reshape(bsz * seq, d)
    cast_specs, cast_shapes = zip(*[_cast_job(w, n_tiles, lambda i: i) for w in cast_ws])
    kern = functools.partial(_mixer0_kernel, n_cast=len(cast_ws), tiles_per_seq=n_t, layer=layer,
                             conv_dim=conv_dim, width=width)
    out, *cast_out = pl.pallas_call(
        kern,
        grid=(n_tiles,),
        in_specs=[
            pl.BlockSpec((tt, d), lambda i: (i, 0)),
            pl.BlockSpec((tt, d), lambda i: (jnp.minimum(i + 1, n_tiles - 1), 0)),
            _const_spec((1, d)),
            _const_spec((d, n_in)),
            _const_spec(conv_w.shape),
            _const_spec(lbp.shape),
            _const_spec((1, width)),
            _const_spec(w_out.shape),
            *cast_specs,
        ],
        out_specs=[pl.BlockSpec((tt, d), lambda i: (i, 0)), *cast_specs],
        out_shape=[jax.ShapeDtypeStruct(x2.shape, F32), *cast_shapes],
        scratch_shapes=[
            pltpu.VMEM((tt, n_in), F32),
            pltpu.VMEM((tt, n_in), F32),
            pltpu.VMEM((tt + SUB, conv_dim), F32),
            pltpu.VMEM((heads, HEAD_DIM, HEAD_DIM), F32),
            pltpu.VMEM((tt, conv_dim + width), BF16),
            pltpu.VMEM((CHUNK_SLOTS, CHUNK * SUB, width), BF16),
            pltpu.VMEM((CHUNK_SLOTS, CHUNK * SUB, width), F32),
            pltpu.VMEM((tt, d), BF16),
        ],
        compiler_params=_cparams(1),
        name="mixer0",
    )(x2, x2, gmix.reshape(1, d), w_in.astype(BF16), conv_w, lbp, gn, w_out.astype(BF16), *cast_ws)
    return out.reshape(x.shape), cast_out


def _ffn_kernel(x_ref, g_ref, w1_ref, w3_ref, w2_ref, cast_ref, o_ref, cast_o_ref, *, tf):
    x = x_ref[...]
    hn = _rms(x, g_ref[...]).astype(BF16)
    acc = x
    for j in range(w1_ref.shape[1] // tf):
        cols = slice(j * tf, (j + 1) * tf)
        h = _silu(_dot(hn, w1_ref[:, cols])) * _dot(hn, w3_ref[:, cols])
        acc = acc + _dot(h.astype(BF16), w2_ref[cols, :])
    o_ref[...] = acc
    cast_o_ref[...] = cast_ref[...].astype(BF16)


def _ffn(x2, g, w1, w3, w2, cast_w, tm=512, tf=256):
    n, d = x2.shape
    tm = min(tm, n)
    cast_spec, cast_shape = _cast_job(cast_w, n // tm, lambda i: i)
    return pl.pallas_call(
        functools.partial(_ffn_kernel, tf=tf),
        grid=(n // tm,),
        in_specs=[
            pl.BlockSpec((tm, d), lambda i: (i, 0)),
            _const_spec((1, d)),
            _const_spec(w1.shape),
            _const_spec(w3.shape),
            _const_spec(w2.shape),
            cast_spec,
        ],
        out_specs=[pl.BlockSpec((tm, d), lambda i: (i, 0)), cast_spec],
        out_shape=[jax.ShapeDtypeStruct((n, d), F32), cast_shape],
        compiler_params=_cparams(1),
        name="ffn0",
    )(x2, g.reshape(1, d), w1, w3, w2, cast_w)


def _conformer_kernel(x_ref, g_ref, w1_ref, b1_ref, wdw_ref, bdw_ref, lng_ref, lnb_ref,
                      w2_ref, b2_ref, cast_ref, o_ref, cast_o_ref, ubuf_ref, conv_ref, *, halo):
    tt = x_ref.shape[1]
    d = x_ref.shape[2]
    kw = wdw_ref.shape[0]

    @pl.when(pl.program_id(1) == 0)
    def _():
        ubuf_ref[0:halo, :] = jnp.zeros((halo, d), F32)
        ubuf_ref[halo + tt:halo + tt + SUB, :] = jnp.zeros((SUB, d), F32)

    x = x_ref[0]
    hn = _rms(x, g_ref[...]).astype(BF16)
    p = _dot(hn, w1_ref[...]) + b1_ref[...]
    ubuf_ref[halo:halo + tt, :] = p[:, 0:d] * jax.nn.sigmoid(p[:, d:2 * d])
    off0 = halo - (kw - 1)
    seg = min(tt, 128)
    for c in range(d // V7X_LANES):
        lanes = slice(c * V7X_LANES, (c + 1) * V7X_LANES)
        wcol = wdw_ref[:, lanes]
        for base in range(0, tt, seg):
            acc = None
            for r in range(SUB):
                part = None
                for a in range((off0 + kw - 1) // SUB + 1):
                    k = SUB * a + r - off0
                    if 0 <= k < kw:
                        win = ubuf_ref[base + SUB * a:base + SUB * a + seg + SUB, lanes]
                        term = wcol[k:k + 1, :] * win
                        part = term if part is None else part + term
                if part is not None:
                    shifted = part[r:r + seg, :]
                    acc = shifted if acc is None else acc + shifted
            conv_ref[base:base + seg, lanes] = acc + bdw_ref[:, lanes]
    acc = conv_ref[...]
    ubuf_ref[0:halo, :] = ubuf_ref[tt:tt + halo, :]
    mu = jnp.mean(acc, axis=-1, keepdims=True)
    xc = acc - mu
    var = jnp.mean(xc * xc, axis=-1, keepdims=True)
    u = _silu(xc * lax.rsqrt(var + EPS) * lng_ref[...] + lnb_ref[...])
    o_ref[0] = x + _dot(u.astype(BF16), w2_ref[...]) + b2_ref[...]
    cast_o_ref[...] = cast_ref[...].astype(BF16)


def _conformer(x, g, w_pw1, b_pw1, w_dw, b_dw, ln_g, ln_b, w_pw2, b_pw2, cast_w, tt=256):
    bsz, seq, d = x.shape
    tt = min(tt, seq)
    kw = w_dw.shape[0]
    halo = -(-(kw - 1) // SUB) * SUB
    r = lambda a: a.reshape(1, -1)
    n_t = seq // tt
    cast_spec, cast_shape = _cast_job(cast_w, bsz * n_t, lambda b, t: b * n_t + t)
    return pl.pallas_call(
        functools.partial(_conformer_kernel, halo=halo),
        grid=(bsz, seq // tt),
        in_specs=[
            pl.BlockSpec((1, tt, d), lambda b, t: (b, t, 0)),
            _const_spec((1, d)),
            _const_spec(w_pw1.shape),
            _const_spec((1, 2 * d)),
            _const_spec(w_dw.shape),
            _const_spec((1, d)),
            _const_spec((1, d)),
            _const_spec((1, d)),
            _const_spec(w_pw2.shape),
            _const_spec((1, d)),
            cast_spec,
        ],
        out_specs=[pl.BlockSpec((1, tt, d), lambda b, t: (b, t, 0)), cast_spec],
        out_shape=[jax.ShapeDtypeStruct(x.shape, F32), cast_shape],
        scratch_shapes=[pltpu.VMEM((halo + tt + SUB, d), F32), pltpu.VMEM((tt, d), F32)],
        compiler_params=_cparams(2),
        name="conformer",
    )(x, r(g), w_pw1, r(b_pw1), w_dw, r(b_dw), r(ln_g), r(ln_b), w_pw2, r(b_pw2), cast_w)


def _router_kernel(x_ref, g_ref, wr_ref, hn_ref, meta_ref, gate_ref, cnt_ref, run_ref,
                   *, n_experts):
    tm = x_ref.shape[0]

    @pl.when(pl.program_id(0) == 0)
    def _():
        run_ref[...] = jnp.zeros_like(run_ref)

    hn = _rms(x_ref[...], g_ref[...])
    _store_row_tiles(hn_ref, hn)
    wr = wr_ref[...]
    hn_hi, wr_hi = hn.astype(BF16), wr.astype(BF16)
    hn_lo = (hn - hn_hi.astype(F32)).astype(BF16)
    wr_lo = (wr - wr_hi.astype(F32)).astype(BF16)
    logits = _dot(hn_hi, wr_hi) + (_dot(hn_lo, wr_hi) + _dot(hn_hi, wr_lo))
    lane = lax.broadcasted_iota(I32, logits.shape, 1)
    neg = jnp.float32(-jnp.inf)
    big = jnp.int32(V7X_LANES)
    logits = jnp.where(lane < n_experts, logits, neg)
    m1 = jnp.max(logits, axis=-1, keepdims=True)
    i1 = jnp.min(jnp.where(logits == m1, lane, big), axis=-1, keepdims=True)
    rest = jnp.where(lane == i1, neg, logits)
    m2 = jnp.max(rest, axis=-1, keepdims=True)
    i2 = jnp.min(jnp.where(rest == m2, lane, big), axis=-1, keepdims=True)
    e2 = jnp.exp(m2 - m1)
    g1 = 1.0 / (1.0 + e2)
    g2 = e2 / (1.0 + e2)

    sel1 = lane == i1
    sel2 = lane == i2
    onehot = jnp.where(sel1 | sel2, 1.0, 0.0)
    ti = lax.broadcasted_iota(I32, (tm, tm), 0)
    si = lax.broadcasted_iota(I32, (tm, tm), 1)
    before = _dot((si < ti).astype(BF16), onehot.astype(BF16)) + run_ref[...]
    r1 = jnp.sum(jnp.where(sel1, before, 0.0), axis=-1, keepdims=True)
    r2 = jnp.sum(jnp.where(sel2, before, 0.0), axis=-1, keepdims=True)
    run_ref[...] = run_ref[...] + jnp.sum(onehot, axis=0, keepdims=True)

    meta = jnp.where(lane == 0, i1.astype(F32), jnp.where(lane == 1, i2.astype(F32),
                     jnp.where(lane == 2, r1, jnp.where(lane == 3, r2, 0.0))))
    meta_ref[...] = meta.T[0:SUB, :].astype(I32)
    col = lax.broadcasted_iota(I32, (tm, TOP_K), 1)
    gate_ref[...] = jnp.where(col == 0, g1, g2)
    cnt_ref[...] = run_ref[...].astype(I32)


def _router(x2, g, w_router, tm=512):
    n, d = x2.shape
    assert d == SUB * V7X_LANES
    tm = min(tm, n)
    n_experts = w_router.shape[1]
    wr = jnp.pad(w_router, ((0, 0), (0, V7X_LANES - n_experts)))
    return pl.pallas_call(
        functools.partial(_router_kernel, n_experts=n_experts),
        grid=(n // tm,),
        in_specs=[
            pl.BlockSpec((tm, d), lambda i: (i, 0)),
            _const_spec((1, d)),
            _const_spec((d, V7X_LANES)),
        ],
        out_specs=[
            pl.BlockSpec((tm * SUB, V7X_LANES), lambda i: (i, 0)),
            pl.BlockSpec((SUB, tm), lambda i: (0, i)),
            pl.BlockSpec((tm, TOP_K), lambda i: (i, 0)),
            pl.BlockSpec((1, V7X_LANES), lambda i: (0, 0)),
        ],
        out_shape=[
            jax.ShapeDtypeStruct((n * SUB, V7X_LANES), F32),
            jax.ShapeDtypeStruct((SUB, n), I32),
            jax.ShapeDtypeStruct((n, TOP_K), F32),
            jax.ShapeDtypeStruct((1, V7X_LANES), I32),
        ],
        scratch_shapes=[pltpu.VMEM((1, V7X_LANES), F32)],
        compiler_params=_cparams(1),
        name="router",
    )(x2, g.reshape(1, d), wr)


def _dispatch_kernel(slot0_ref, slot1_ref, pstart_ref, plen_ref, used_ref, hn_ref, xs_ref,
                     zero_ref, sem, zsem, *, n_experts, tile):
    tm = hn_ref.shape[0] // SUB
    n_tiles = xs_ref.shape[0] // (tile * SUB)
    slot_refs = (slot0_ref, slot1_ref)

    def row_copy(r, k):
        return pltpu.make_async_copy(
            _row_tile(hn_ref, r), _row_tile(xs_ref, slot_refs[k][r]), sem)

    _start_row_copies(tm, row_copy)
    for k in range(TOP_K):
        pltpu.make_async_copy(hn_ref, xs_ref.at[pl.ds(0, tm * SUB)], sem).wait()

    def rows(ref, start, count):
        return ref.at[pl.ds(pl.multiple_of(start * SUB, SUB), count * SUB)]

    def for_each_pad_copy(act):
        for e in range(n_experts):
            cur = pstart_ref[e]
            length = plen_ref[e]
            p = 1
            while p < tile:
                take = (length & p) != 0

                @pl.when(take)
                def _():
                    act(pltpu.make_async_copy(rows(zero_ref, 0, p), rows(xs_ref, cur, p), zsem))
                cur = cur + jnp.where(take, p, 0)
                p *= 2
        for i in range(n_experts):
            t = used_ref[0] + i

            @pl.when(t < n_tiles)
            def _():
                act(pltpu.make_async_copy(zero_ref, rows(xs_ref, t * tile, tile), zsem))

    @pl.when(pl.program_id(0) == pl.num_programs(0) - 1)
    def _():
        zero_ref[...] = jnp.zeros_like(zero_ref)
        for_each_pad_copy(lambda cp: cp.start())
        for_each_pad_copy(lambda cp: cp.wait())


def _dispatch(hn, slots, pad_start, pad_len, used_tiles, n_tiles, tile, tm=1024):
    n = hn.shape[0] // SUB
    tm = min(tm, n)
    n_experts = pad_start.shape[0]
    smem = pl.BlockSpec(memory_space=pltpu.SMEM)
    slot_spec = pl.BlockSpec((tm,), lambda i: (i,), memory_space=pltpu.SMEM)
    return pl.pallas_call(
        functools.partial(_dispatch_kernel, n_experts=n_experts, tile=tile),
        grid=(n // tm,),
        in_specs=[
            slot_spec, slot_spec,
            smem, smem, smem,
            pl.BlockSpec((tm * SUB, V7X_LANES), lambda i: (i, 0)),
        ],
        out_specs=pl.BlockSpec(memory_space=pl.ANY),
        out_shape=jax.ShapeDtypeStruct((n_tiles * tile * SUB, V7X_LANES), F32),
        scratch_shapes=[
            pltpu.VMEM((tile * SUB, V7X_LANES), F32),
            pltpu.SemaphoreType.DMA(()),
            pltpu.SemaphoreType.DMA(()),
        ],
        compiler_params=_cparams(1),
        name="dispatch",
    )(slots[0], slots[1], pad_start, pad_len, used_tiles, hn)


def _grouped_kernel(exp_ref, used_ref, nsub_ref, x_ref, w1_ref, w3_ref, w2_ref, o_ref,
                    xb_ref, acc_ref, *, sub):
    i = pl.program_id(0)
    j = pl.program_id(1)
    last = pl.num_programs(1) - 1
    valid = i < used_ref[0]
    tile = acc_ref.shape[0]

    @pl.when(valid & (j == 0))
    def _():
        for c in range(SUB):
            xb_ref[:, c * V7X_LANES:(c + 1) * V7X_LANES] = _load_row_tiles(x_ref, tile, c).astype(BF16)
        acc_ref[...] = jnp.zeros_like(acc_ref)

    @pl.when(valid)
    def _():
        def swiglu_rows(n_rows):
            rows = slice(0, n_rows)
            xb = xb_ref[rows, :]
            h = _silu(_dot(xb, w1_ref[...])) * _dot(xb, w3_ref[...])
            acc_ref[rows, :] += _dot(h.astype(BF16), w2_ref[...])

        for n_sub in range(1, tile // sub + 1):
            @pl.when(nsub_ref[i] == n_sub)
            def _():
                swiglu_rows(n_sub * sub)

    @pl.when(valid & (j == last))
    def _():
        _store_row_tiles(o_ref, acc_ref[...])

    @pl.when(jnp.logical_not(valid) & (j == last))
    def _():
        o_ref[...] = jnp.zeros_like(o_ref)


def _grouped(xs, w1, w3, w2, tile_exp, used_tiles, tile_nsub, tile, sub, tf=512):
    d = w1.shape[1]
    n_tiles = xs.shape[0] // (tile * SUB)
    nj = w1.shape[2] // tf

    def jsel(i, j, used):
        return jnp.where(i < used[0], j, nj - 1)

    grid_spec = pltpu.PrefetchScalarGridSpec(
        num_scalar_prefetch=3,
        grid=(n_tiles, nj),
        in_specs=[
            pl.BlockSpec((tile * SUB, V7X_LANES), lambda i, j, ex, us, ns: (i, 0)),
            pl.BlockSpec((None, d, tf), lambda i, j, ex, us, ns: (ex[i], 0, jsel(i, j, us))),
            pl.BlockSpec((None, d, tf), lambda i, j, ex, us, ns: (ex[i], 0, jsel(i, j, us))),
            pl.BlockSpec((None, tf, d), lambda i, j, ex, us, ns: (ex[i], jsel(i, j, us), 0)),
        ],
        out_specs=pl.BlockSpec((tile * SUB, V7X_LANES), lambda i, j, ex, us, ns: (i, 0)),
        scratch_shapes=[pltpu.VMEM((tile, d), BF16), pltpu.VMEM((tile, d), F32)],
    )
    return pl.pallas_call(
        functools.partial(_grouped_kernel, sub=sub),
        grid_spec=grid_spec,
        out_shape=jax.ShapeDtypeStruct(xs.shape, F32),
        compiler_params=_cparams(2),
        name="grouped_swiglu",
    )(tile_exp, used_tiles, tile_nsub, xs, w1, w3, w2)


def _combine_kernel(slot0_ref, slot1_ref, nslot0_ref, nslot1_ref, h_ref, gate_ref, g_ref, ys_ref,
                    o_ref, buf_a_ref, buf_b_ref, sems):
    tm, d = h_ref.shape
    i = pl.program_id(0)

    def start_gather(slot_refs, buf_ref, sem):
        def row_copy(r, k):
            return pltpu.make_async_copy(
                _row_tile(ys_ref, slot_refs[k][r]), _row_tile(buf_ref.at[k], r), sem)

        _start_row_copies(tm, row_copy)

    def finish(buf_ref, sem):
        for k in range(TOP_K):
            pltpu.make_async_copy(ys_ref.at[pl.ds(0, tm * SUB)], buf_ref.at[k], sem).wait()
        for r0 in range(0, tm, COMBINE_ROWS):
            rows = slice(r0, r0 + COMBINE_ROWS)
            gate = gate_ref[rows, :]
            ss = jnp.zeros((COMBINE_ROWS, 1), F32)
            for c in range(SUB):
                lanes = slice(c * V7X_LANES, (c + 1) * V7X_LANES)
                tiles = pl.ds(r0 * SUB + c, COMBINE_ROWS, stride=SUB)
                piece = (h_ref[rows, lanes] + gate[:, 0:1] * buf_ref[0, tiles, :]
                         + gate[:, 1:2] * buf_ref[1, tiles, :])
                o_ref[rows, lanes] = piece
                ss = ss + jnp.sum(piece * piece, axis=-1, keepdims=True)
            o_ref[rows, :] = o_ref[rows, :] * lax.rsqrt(ss / d + EPS) * g_ref[...]

    @pl.when(i == 0)
    def _():
        start_gather((slot0_ref, slot1_ref), buf_a_ref, sems.at[0])

    bufs = (buf_a_ref, buf_b_ref)
    for parity in range(2):
        @pl.when(i % 2 == parity)
        def _():
            @pl.when(i + 1 < pl.num_programs(0))
            def _():
                start_gather((nslot0_ref, nslot1_ref), bufs[1 - parity], sems.at[1 - parity])

            finish(bufs[parity], sems.at[parity])


def _combine(h2, gates, slots, ys, g, tm=1024):
    n, d = h2.shape
    tm = min(tm, n)
    n_steps = n // tm
    slot_spec = pl.BlockSpec((tm,), lambda i: (i,), memory_space=pltpu.SMEM)
    next_spec = pl.BlockSpec((tm,), lambda i: (jnp.minimum(i + 1, n_steps - 1),),
                             memory_space=pltpu.SMEM)
    return pl.pallas_call(
        _combine_kernel,
        grid=(n_steps,),
        in_specs=[
            slot_spec, slot_spec, next_spec, next_spec,
            pl.BlockSpec((tm, d), lambda i: (i, 0)),
            pl.BlockSpec((tm, TOP_K), lambda i: (i, 0)),
            _const_spec((1, d)),
            pl.BlockSpec(memory_space=pl.ANY),
        ],
        out_specs=pl.BlockSpec((tm, d), lambda i: (i, 0)),
        out_shape=jax.ShapeDtypeStruct((n, d), F32),
        scratch_shapes=[pltpu.VMEM((TOP_K, tm * SUB, V7X_LANES), F32),
                        pltpu.VMEM((TOP_K, tm * SUB, V7X_LANES), F32),
                        pltpu.SemaphoreType.DMA((2,))],
        compiler_params=_cparams(1),
        name="combine",
    )(slots[0], slots[1], slots[0], slots[1], h2, gates, g.reshape(1, d), ys)


def _moe(h2, g_ffn, w_router, w1, w3, w2, g_final, tile=1024, sub=256):
    n, d = h2.shape
    n_experts = w_router.shape[1]
    tile = min(tile, n)
    n_tiles = TOP_K * n // tile + n_experts
    hn, meta, gates, counts = _router(h2, g_ffn, w_router)
    expert, rank = meta[0:TOP_K], meta[TOP_K:2 * TOP_K]
    counts = counts[0, :n_experts]

    tiles_per = (counts + tile - 1) // tile
    ends = jnp.cumsum(tiles_per)
    starts = ends - tiles_per
    used = ends[-1:]
    expert, slots = expert.reshape(-1, V7X_LANES), rank.reshape(-1, V7X_LANES)
    for e in range(n_experts):
        slots = slots + jnp.where(expert == e, starts[e] * tile, 0)
    slots = slots.astype(I32).reshape(TOP_K, n)
    pad_start = (starts * tile + counts).astype(I32)
    pad_len = (tiles_per * tile - counts).astype(I32)
    tidx = jnp.minimum(jnp.arange(n_tiles, dtype=I32), used - 1)
    tile_exp = jnp.sum((tidx[:, None] >= ends[None, :]).astype(I32), axis=1)
    sub = min(sub, tile)
    tile_rows = jnp.clip(counts[tile_exp] - (tidx - starts[tile_exp]) * tile, 0, tile)
    tile_nsub = (tile_rows + sub - 1) // sub

    xs = _dispatch(hn, slots, pad_start, pad_len, used.astype(I32), n_tiles, tile)
    ys = _grouped(xs, w1, w3, w2, tile_exp.astype(I32), used.astype(I32), tile_nsub.astype(I32),
                  tile, sub)
    return _combine(h2, gates, slots, ys, g_final)


def kernel(x, ev_norm_mix, ev_w_in, ev_conv_w, hgrn_lower_bounds, ev_gnorm, ev_w_out,
           ev_norm_ffn, ev_ffn_w1, ev_ffn_w3, ev_ffn_w2,
           od_norm_mix, od_w_pw1, od_b_pw1, od_w_dw, od_b_dw, od_ln_g, od_ln_b,
           od_w_pw2, od_b_pw2, od_norm_ffn, od_router, od_moe_w1, od_moe_w3, od_moe_w2,
           final_norm):
    bsz, seq, d = x.shape
    assert ev_w_in.shape[0] == 1 and od_w_pw1.shape[0] == 1
    n_experts, _, n_ff = od_moe_w1.shape[1:]
    flat = lambda w: w.reshape(-1, w.shape[-1])
    h, (w1b, f1b, f3b, f2b, pw1b, pw2b) = _mixer0(
        x, ev_norm_mix[0], ev_w_in[0], ev_conv_w[0], hgrn_lower_bounds, ev_gnorm[0], ev_w_out[0],
        layer=0, cast_ws=[flat(od_moe_w1[0]), ev_ffn_w1[0], ev_ffn_w3[0], ev_ffn_w2[0],
                          od_w_pw1[0], od_w_pw2[0]])
    h, w3b = _ffn(h.reshape(bsz * seq, d), ev_norm_ffn[0], f1b, f3b, f2b,
                  cast_w=flat(od_moe_w3[0]))
    h, w2b = _conformer(h.reshape(bsz, seq, d), od_norm_mix[0], pw1b, od_b_pw1[0],
                        od_w_dw[0], od_b_dw[0], od_ln_g[0], od_ln_b[0], pw2b, od_b_pw2[0],
                        cast_w=flat(od_moe_w2[0]))
    out = _moe(h.reshape(bsz * seq, d), od_norm_ffn[0], od_router[0],
               w1b.reshape(n_experts, d, n_ff), w3b.reshape(n_experts, d, n_ff),
               w2b.reshape(n_experts, n_ff, d), final_norm)
    return out.reshape(bsz, seq, d)
```

```python
import functools

import jax
import jax.numpy as jnp
from jax import lax
from jax.experimental import pallas as pl
from jax.experimental.pallas import tpu as pltpu

F32 = jnp.float32
BF16 = jnp.bfloat16
I32 = jnp.int32

EPS = 1e-6
CHUNK = 64
CHUNK_SLOTS = 2
PROJ_PIECE = 256
SUB = 8
HEAD_DIM = 128
TOP_K = 2

V7X_LANES = 128
V7X_MXU_DIM = 256
V7X_VMEM_BYTES = 64 * 1024 * 1024
VMEM_LIMIT = V7X_VMEM_BYTES - 8 * 1024 * 1024


def _cparams(n_axes):
    return pltpu.CompilerParams(
        dimension_semantics=("arbitrary",) * n_axes, vmem_limit_bytes=VMEM_LIMIT)


def _const_spec(shape):
    nd = len(shape)
    return pl.BlockSpec(shape, lambda *_: (0,) * nd, pipeline_mode=pl.Buffered(1))


def _rms(x, g):
    return x * lax.rsqrt(jnp.mean(x * x, axis=-1, keepdims=True) + EPS) * g


def _silu(x):
    return x * jax.nn.sigmoid(x)


def _dot(a, b):
    return jnp.dot(a, b, preferred_element_type=F32)


def _dot_nt(a, b):
    return lax.dot_general(a, b, (((1,), (1,)), ((), ())), preferred_element_type=F32)


def _store_row_tiles(ref, val):
    m = val.shape[0]
    for c in range(SUB):
        ref[pl.ds(c, m, stride=SUB), :] = val[:, c * V7X_LANES:(c + 1) * V7X_LANES]


def _load_row_tiles(ref, m, c):
    return ref[pl.ds(c, m, stride=SUB), :]


def _row_tile(ref, row):
    return ref.at[pl.ds(pl.multiple_of(row * SUB, SUB), SUB)]


ROW_COPY_UNROLL = 8
COMBINE_ROWS = 128


def _start_row_copies(n_rows, row_copy):
    def body(i, carry):
        for u in range(ROW_COPY_UNROLL):
            for k in range(TOP_K):
                row_copy(i * ROW_COPY_UNROLL + u, k).start(priority=(u * TOP_K + k) % 2)
        return carry

    lax.fori_loop(0, n_rows // ROW_COPY_UNROLL, body, 0)


def _mixer0_tile(x_ref, xn_ref, gmix_ref, win_ref, convw_ref, lbp_ref, gn_ref, wout_ref, o_ref,
                 proj_ref, next_ref, zbuf_ref, st_ref, y_ref, w_ref, s_ref, hn_ref, *, layer,
                 conv_dim, width):
    tt = x_ref.shape[0]
    heads = width // HEAD_DIM
    c3 = 3 * conv_dim

    x = x_ref[...]
    hn_ref[...] = _rms(xn_ref[...], gmix_ref[...]).astype(BF16)
    n_in = win_ref.shape[1]
    pieces = [slice(c0, min(c0 + PROJ_PIECE, n_in)) for c0 in range(0, n_in, PROJ_PIECE)]

    def project_next(cols):
        next_ref[:, cols] = _dot(hn_ref[...], win_ref[:, cols])

    z = proj_ref[:, 2 * conv_dim:c3] * proj_ref[:, 0:conv_dim]
    zbuf_ref[SUB:SUB + tt, :] = z
    cw = convw_ref[...]
    conv = (cw[2:3, :] * z + cw[1:2, :] * zbuf_ref[SUB - 1:SUB - 1 + tt, :]
            + cw[0:1, :] * zbuf_ref[SUB - 2:SUB - 2 + tt, :])
    y_ref[:, 0:conv_dim] = (proj_ref[:, conv_dim:2 * conv_dim] * conv).astype(BF16)
    zbuf_ref[0:SUB, :] = zbuf_ref[tt:tt + SUB, :]

    lbp = lbp_ref[...]
    lbe = jnp.exp(lbp - jnp.max(lbp, axis=0, keepdims=True))
    lb = jnp.sum(lbe[0:layer + 1, :], axis=0, keepdims=True) / jnp.sum(lbe, axis=0, keepdims=True)

    row = lax.broadcasted_iota(I32, (CHUNK, width), 0)
    row8 = lax.broadcasted_iota(I32, (SUB, width), 0)
    ti = lax.broadcasted_iota(I32, (CHUNK, CHUNK), 0)
    si = lax.broadcasted_iota(I32, (CHUNK, CHUNK), 1)
    tri = (si <= ti).astype(BF16)
    bdr = lax.broadcasted_iota(I32, (V7X_MXU_DIM, V7X_MXU_DIM), 0) // HEAD_DIM
    bdc = lax.broadcasted_iota(I32, (V7X_MXU_DIM, V7X_MXU_DIM), 1) // HEAD_DIM
    head_ones = (bdr == bdc).astype(BF16)
    gn = gn_ref[...]

    def bcast_row(val, r, n):
        return jnp.broadcast_to(val[r:r + 1, :], (n, val.shape[1]))

    def chunk(r0, slot, fill):
        rows = slice(r0, r0 + CHUNK)
        ws_ref = w_ref.at[slot]
        q = proj_ref[rows, c3:c3 + width]
        f = proj_ref[rows, c3 + width:c3 + 2 * width]
        v = proj_ref[rows, c3 + 2 * width:c3 + 3 * width]
        og = proj_ref[rows, c3 + 3 * width:c3 + 4 * width]
        fg = lb + (1.0 - lb) * jax.nn.sigmoid(f)
        k = 1.0 - fg
        g_rest = jnp.log(fg)
        b = jnp.zeros((CHUNK, width), F32)
        for _ in range(3):
            g_part = g_rest.astype(BF16)
            b = b + _dot(tri, g_part)
            g_rest = g_rest - g_part.astype(F32)
        b_last = b[CHUNK - 1:CHUNK, :]
        v16 = v.astype(BF16)
        fill()

        q_s = (q * jnp.exp(b)).astype(BF16)
        k_s = (k * jnp.exp(b_last - b)).astype(BF16)
        e_last = jnp.exp(b_last)

        p_acc = [jnp.zeros((CHUNK, CHUNK), F32) for _ in range(heads)]
        half = CHUNK // 2
        while half >= SUB:
            blk = 2 * half
            anc = jnp.concatenate(
                [bcast_row(b, i * blk + half, blk) for i in range(CHUNK // blk)], axis=0)
            upper = (row % blk) >= half
            q_l = (q * jnp.where(upper, jnp.exp(b - anc), 0.0)).astype(BF16)
            k_l = (k * jnp.where(upper, 0.0, jnp.exp(anc - b))).astype(BF16)
            same = (ti // blk) == (si // blk)
            for h in range(heads):
                hs = slice(h * HEAD_DIM, (h + 1) * HEAD_DIM)
                p_acc[h] = p_acc[h] + jnp.where(same, _dot_nt(q_l[:, hs], k_l[:, hs]), 0.0)
            half //= 2
        fill()

        for rb in range(CHUNK // SUB):
            if rb == CHUNK // SUB // 2:
                fill()
            rs = slice(rb * SUB, (rb + 1) * SUB)
            b8, q8, k8 = b[rs, :], q[rs, :], k[rs, :]
            for s in range(0, SUB, 2):
                pair = []
                for s1 in (s, s + 1):
                    dec = jnp.where(row8 >= s1, jnp.exp(b8 - bcast_row(b8, s1, SUB)), 0.0)
                    pair.append(q8 * dec * bcast_row(k8, s1, SUB))
                ws_ref[(rb * SUB + s) * SUB:(rb * SUB + s + 2) * SUB, :] = (
                    jnp.concatenate(pair, axis=0).astype(BF16))
        sc_ref = s_ref.at[slot]
        for j in range(width // V7X_MXU_DIM):
            lanes = slice(j * V7X_MXU_DIM, (j + 1) * V7X_MXU_DIM)
            sc_ref[:, lanes] = _dot(ws_ref[:, lanes], head_ones)
        o_diag = []
        for rb in range(CHUNK // SUB):
            sc = sc_ref[rb * SUB * SUB:(rb + 1) * SUB * SUB, :]
            v8 = v[rb * SUB:(rb + 1) * SUB, :]
            acc = None
            for s in range(SUB):
                term = sc[s * SUB:(s + 1) * SUB, :] * bcast_row(v8, s, SUB)
                acc = term if acc is None else acc + term
            o_diag.append(acc)
        o_diag = jnp.concatenate(o_diag, axis=0)
        fill()

        outs = []
        for h in range(heads):
            hs = slice(h * HEAD_DIM, (h + 1) * HEAD_DIM)
            st = st_ref[h]
            o_h = (_dot_nt(q_s[:, hs], st.astype(BF16)) + _dot(p_acc[h].astype(BF16), v16[:, hs])
                   + o_diag[:, hs])
            st_ref[h] = e_last[:, hs] * st + _dot(v16[:, hs].T, k_s[:, hs])
            o_n = o_h * lax.rsqrt(jnp.mean(o_h * o_h, axis=-1, keepdims=True) + EPS)
            outs.append(o_n)
        o = jnp.concatenate(outs, axis=1) * gn * _silu(og)
        y_ref[rows, conv_dim:conv_dim + width] = o.astype(BF16)

    todo = iter(pieces)

    def fill():
        cols = next(todo, None)
        if cols is not None:
            project_next(cols)

    for c in range(tt // CHUNK):
        chunk(c * CHUNK, c % CHUNK_SLOTS, fill)
    for cols in todo:
        project_next(cols)
    o_ref[...] = x + _dot(y_ref[...], wout_ref[...])


def _mixer0_kernel(*refs, n_cast, tiles_per_seq, **static):
    x_ref, xn_ref, gmix_ref, win_ref, convw_ref, lbp_ref, gn_ref, wout_ref = refs[:8]
    cast_refs = refs[8:8 + n_cast]
    o_ref = refs[8 + n_cast]
    cast_o_refs = refs[9 + n_cast:9 + 2 * n_cast]
    proj_a_ref, proj_b_ref, zbuf_ref, st_ref, y_ref, w_ref, s_ref, hn_ref = refs[9 + 2 * n_cast:]
    i = pl.program_id(0)

    @pl.when(i == 0)
    def _():
        proj_a_ref[...] = _dot(_rms(x_ref[...], gmix_ref[...]).astype(BF16), win_ref[...])

    @pl.when(i % tiles_per_seq == 0)
    def _():
        st_ref[...] = jnp.zeros_like(st_ref)
        zbuf_ref[0:SUB, :] = jnp.zeros((SUB, zbuf_ref.shape[1]), F32)

    for parity, (cur, nxt) in enumerate(((proj_a_ref, proj_b_ref), (proj_b_ref, proj_a_ref))):
        @pl.when(i % 2 == parity)
        def _():
            _mixer0_tile(x_ref, xn_ref, gmix_ref, win_ref, convw_ref, lbp_ref, gn_ref, wout_ref,
                         o_ref, cur, nxt, zbuf_ref, st_ref, y_ref, w_ref, s_ref, hn_ref, **static)

    for src, dst in zip(cast_refs, cast_o_refs):
        dst[...] = src[...].astype(BF16)


BF16_ROWS = 16


def _cast_job(w2d, n_steps, step_of):
    rows, cols = w2d.shape
    n_blocks = max(n for n in range(1, n_steps + 1)
                   if rows % n == 0 and (rows // n) % BF16_ROWS == 0)
    spec = pl.BlockSpec((rows // n_blocks, cols),
                        lambda *ids: (jnp.minimum(step_of(*ids), n_blocks - 1), 0))
    return spec, jax.ShapeDtypeStruct(w2d.shape, BF16)


def _mixer0(x, gmix, w_in, conv_w, lbp, gnorm, w_out, layer, cast_ws, tt=256):
    bsz, seq, d = x.shape
    conv_dim = conv_w.shape[1]
    width = lbp.shape[1]
    n_in = w_in.shape[1]
    tt = min(tt, seq)
    heads = width // HEAD_DIM
    gn = jnp.tile(gnorm.reshape(1, HEAD_DIM), (1, heads))
    n_t = seq // tt
    n_tiles = bsz * n_t
    x2 = x.reshape(bsz * seq, d)
    cast_specs, cast_shapes = zip(*[_cast_job(w, n_tiles, lambda i: i) for w in cast_ws])
    kern = functools.partial(_mixer0_kernel, n_cast=len(cast_ws), tiles_per_seq=n_t, layer=layer,
                             conv_dim=conv_dim, width=width)
    out, *cast_out = pl.pallas_call(
        kern,
        grid=(n_tiles,),
        in_specs=[
            pl.BlockSpec((tt, d), lambda i: (i, 0)),
            pl.BlockSpec((tt, d), lambda i: (jnp.minimum(i + 1, n_tiles - 1), 0)),
            _const_spec((1, d)),
            _const_spec((d, n_in)),
            _const_spec(conv_w.shape),
            _const_spec(lbp.shape),
            _const_spec((1, width)),
            _const_spec(w_out.shape),
            *cast_specs,
        ],
        out_specs=[pl.BlockSpec((tt, d), lambda i: (i, 0)), *cast_specs],
        out_shape=[jax.ShapeDtypeStruct(x2.shape, F32), *cast_shapes],
        scratch_shapes=[
            pltpu.VMEM((tt, n_in), F32),
            pltpu.VMEM((tt, n_in), F32),
            pltpu.VMEM((tt + SUB, conv_dim), F32),
            pltpu.VMEM((heads, HEAD_DIM, HEAD_DIM), F32),
            pltpu.VMEM((tt, conv_dim + width), BF16),
            pltpu.VMEM((CHUNK_SLOTS, CHUNK * SUB, width), BF16),
            pltpu.VMEM((CHUNK_SLOTS, CHUNK * SUB, width), F32),
            pltpu.VMEM((tt, d), BF16),
        ],
        compiler_params=_cparams(1),
        name="mixer0",
    )(x2, x2, gmix.reshape(1, d), w_in.astype(BF16), conv_w, lbp, gn, w_out.astype(BF16), *cast_ws)
    return out.reshape(x.shape), cast_out


def _ffn_kernel(x_ref, g_ref, w1_ref, w3_ref, w2_ref, cast_ref, o_ref, cast_o_ref, *, tf):
    x = x_ref[...]
    hn = _rms(x, g_ref[...]).astype(BF16)
    acc = x
    for j in range(w1_ref.shape[1] // tf):
        cols = slice(j * tf, (j + 1) * tf)
        h = _silu(_dot(hn, w1_ref[:, cols])) * _dot(hn, w3_ref[:, cols])
        acc = acc + _dot(h.astype(BF16), w2_ref[cols, :])
    o_ref[...] = acc
    cast_o_ref[...] = cast_ref[...].astype(BF16)


def _ffn(x2, g, w1, w3, w2, cast_w, tm=512, tf=256):
    n, d = x2.shape
    tm = min(tm, n)
    cast_spec, cast_shape = _cast_job(cast_w, n // tm, lambda i: i)
    return pl.pallas_call(
        functools.partial(_ffn_kernel, tf=tf),
        grid=(n // tm,),
        in_specs=[
            pl.BlockSpec((tm, d), lambda i: (i, 0)),
            _const_spec((1, d)),
            _const_spec(w1.shape),
            _const_spec(w3.shape),
            _const_spec(w2.shape),
            cast_spec,
        ],
        out_specs=[pl.BlockSpec((tm, d), lambda i: (i, 0)), cast_spec],
        out_shape=[jax.ShapeDtypeStruct((n, d), F32), cast_shape],
        compiler_params=_cparams(1),
        name="ffn0",
    )(x2, g.reshape(1, d), w1, w3, w2, cast_w)


def _conformer_kernel(x_ref, g_ref, w1_ref, b1_ref, wdw_ref, bdw_ref, lng_ref, lnb_ref,
                      w2_ref, b2_ref, cast_ref, o_ref, cast_o_ref, ubuf_ref, conv_ref, *, halo):
    tt = x_ref.shape[1]
    d = x_ref.shape[2]
    kw = wdw_ref.shape[0]

    @pl.when(pl.program_id(1) == 0)
    def _():
        ubuf_ref[0:halo, :] = jnp.zeros((halo, d), F32)
        ubuf_ref[halo + tt:halo + tt + SUB, :] = jnp.zeros((SUB, d), F32)

    x = x_ref[0]
    hn = _rms(x, g_ref[...]).astype(BF16)
    p = _dot(hn, w1_ref[...]) + b1_ref[...]
    ubuf_ref[halo:halo + tt, :] = p[:, 0:d] * jax.nn.sigmoid(p[:, d:2 * d])
    off0 = halo - (kw - 1)
    seg = min(tt, 128)
    for c in range(d // V7X_LANES):
        lanes = slice(c * V7X_LANES, (c + 1) * V7X_LANES)
        wcol = wdw_ref[:, lanes]
        for base in range(0, tt, seg):
            acc = None
            for r in range(SUB):
                part = None
                for a in range((off0 + kw - 1) // SUB + 1):
                    k = SUB * a + r - off0
                    if 0 <= k < kw:
                        win = ubuf_ref[base + SUB * a:base + SUB * a + seg + SUB, lanes]
                        term = wcol[k:k + 1, :] * win
                        part = term if part is None else part + term
                if part is not None:
                    shifted = part[r:r + seg, :]
                    acc = shifted if acc is None else acc + shifted
            conv_ref[base:base + seg, lanes] = acc + bdw_ref[:, lanes]
    acc = conv_ref[...]
    ubuf_ref[0:halo, :] = ubuf_ref[tt:tt + halo, :]
    mu = jnp.mean(acc, axis=-1, keepdims=True)
    xc = acc - mu
    var = jnp.mean(xc * xc, axis=-1, keepdims=True)
    u = _silu(xc * lax.rsqrt(var + EPS) * lng_ref[...] + lnb_ref[...])
    o_ref[0] = x + _dot(u.astype(BF16), w2_ref[...]) + b2_ref[...]
    cast_o_ref[...] = cast_ref[...].astype(BF16)


def _conformer(x, g, w_pw1, b_pw1, w_dw, b_dw, ln_g, ln_b, w_pw2, b_pw2, cast_w, tt=512):
    bsz, seq, d = x.shape
    tt = min(tt, seq)
    kw = w_dw.shape[0]
    halo = -(-(kw - 1) // SUB) * SUB
    r = lambda a: a.reshape(1, -1)
    n_t = seq // tt
    cast_spec, cast_shape = _cast_job(cast_w, bsz * n_t, lambda b, t: b * n_t + t)
    return pl.pallas_call(
        functools.partial(_conformer_kernel, halo=halo),
        grid=(bsz, seq // tt),
        in_specs=[
            pl.BlockSpec((1, tt, d), lambda b, t: (b, t, 0)),
            _const_spec((1, d)),
            _const_spec(w_pw1.shape),
            _const_spec((1, 2 * d)),
            _const_spec(w_dw.shape),
            _const_spec((1, d)),
            _const_spec((1, d)),
            _const_spec((1, d)),
            _const_spec(w_pw2.shape),
            _const_spec((1, d)),
            cast_spec,
        ],
        out_specs=[pl.BlockSpec((1, tt, d), lambda b, t: (b, t, 0)), cast_spec],
        out_shape=[jax.ShapeDtypeStruct(x.shape, F32), cast_shape],
        scratch_shapes=[pltpu.VMEM((halo + tt + SUB, d), F32), pltpu.VMEM((tt, d), F32)],
        compiler_params=_cparams(2),
        name="conformer",
    )(x, r(g), w_pw1, r(b_pw1), w_dw, r(b_dw), r(ln_g), r(ln_b), w_pw2, r(b_pw2), cast_w)


def _router_kernel(x_ref, g_ref, wr_ref, hn_ref, meta_ref, gate_ref, cnt_ref, run_ref,
                   *, n_experts):
    tm = x_ref.shape[0]

    @pl.when(pl.program_id(0) == 0)
    def _():
        run_ref[...] = jnp.zeros_like(run_ref)

    hn = _rms(x_ref[...], g_ref[...])
    _store_row_tiles(hn_ref, hn)
    wr = wr_ref[...]
    hn_hi, wr_hi = hn.astype(BF16), wr.astype(BF16)
    hn_lo = (hn - hn_hi.astype(F32)).astype(BF16)
    wr_lo = (wr - wr_hi.astype(F32)).astype(BF16)
    logits = _dot(hn_hi, wr_hi) + (_dot(hn_lo, wr_hi) + _dot(hn_hi, wr_lo))
    lane = lax.broadcasted_iota(I32, logits.shape, 1)
    neg = jnp.float32(-jnp.inf)
    big = jnp.int32(V7X_LANES)
    logits = jnp.where(lane < n_experts, logits, neg)
    m1 = jnp.max(logits, axis=-1, keepdims=True)
    i1 = jnp.min(jnp.where(logits == m1, lane, big), axis=-1, keepdims=True)
    rest = jnp.where(lane == i1, neg, logits)
    m2 = jnp.max(rest, axis=-1, keepdims=True)
    i2 = jnp.min(jnp.where(rest == m2, lane, big), axis=-1, keepdims=True)
    e2 = jnp.exp(m2 - m1)
    g1 = 1.0 / (1.0 + e2)
    g2 = e2 / (1.0 + e2)

    sel1 = lane == i1
    sel2 = lane == i2
    onehot = jnp.where(sel1 | sel2, 1.0, 0.0)
    ti = lax.broadcasted_iota(I32, (tm, tm), 0)
    si = lax.broadcasted_iota(I32, (tm, tm), 1)
    before = _dot((si < ti).astype(BF16), onehot.astype(BF16)) + run_ref[...]
    r1 = jnp.sum(jnp.where(sel1, before, 0.0), axis=-1, keepdims=True)
    r2 = jnp.sum(jnp.where(sel2, before, 0.0), axis=-1, keepdims=True)
    run_ref[...] = run_ref[...] + jnp.sum(onehot, axis=0, keepdims=True)

    meta = jnp.where(lane == 0, i1.astype(F32), jnp.where(lane == 1, i2.astype(F32),
                     jnp.where(lane == 2, r1, jnp.where(lane == 3, r2, 0.0))))
    meta_ref[...] = meta.T[0:SUB, :].astype(I32)
    col = lax.broadcasted_iota(I32, (tm, TOP_K), 1)
    gate_ref[...] = jnp.where(col == 0, g1, g2)
    cnt_ref[...] = run_ref[...].astype(I32)


def _router(x2, g, w_router, tm=512):
    n, d = x2.shape
    assert d == SUB * V7X_LANES
    tm = min(tm, n)
    n_experts = w_router.shape[1]
    wr = jnp.pad(w_router, ((0, 0), (0, V7X_LANES - n_experts)))
    return pl.pallas_call(
        functools.partial(_router_kernel, n_experts=n_experts),
        grid=(n // tm,),
        in_specs=[
            pl.BlockSpec((tm, d), lambda i: (i, 0)),
            _const_spec((1, d)),
            _const_spec((d, V7X_LANES)),
        ],
        out_specs=[
            pl.BlockSpec((tm * SUB, V7X_LANES), lambda i: (i, 0)),
            pl.BlockSpec((SUB, tm), lambda i: (0, i)),
            pl.BlockSpec((tm, TOP_K), lambda i: (i, 0)),
            pl.BlockSpec((1, V7X_LANES), lambda i: (0, 0)),
        ],
        out_shape=[
            jax.ShapeDtypeStruct((n * SUB, V7X_LANES), F32),
            jax.ShapeDtypeStruct((SUB, n), I32),
            jax.ShapeDtypeStruct((n, TOP_K), F32),
            jax.ShapeDtypeStruct((1, V7X_LANES), I32),
        ],
        scratch_shapes=[pltpu.VMEM((1, V7X_LANES), F32)],
        compiler_params=_cparams(1),
        name="router",
    )(x2, g.reshape(1, d), wr)


def _dispatch_kernel(slot0_ref, slot1_ref, pstart_ref, plen_ref, used_ref, hn_ref, xs_ref,
                     zero_ref, sem, zsem, *, n_experts, tile):
    tm = hn_ref.shape[0] // SUB
    n_tiles = xs_ref.shape[0] // (tile * SUB)
    slot_refs = (slot0_ref, slot1_ref)

    def row_copy(r, k):
        return pltpu.make_async_copy(
            _row_tile(hn_ref, r), _row_tile(xs_ref, slot_refs[k][r]), sem)

    _start_row_copies(tm, row_copy)
    for k in range(TOP_K):
        pltpu.make_async_copy(hn_ref, xs_ref.at[pl.ds(0, tm * SUB)], sem).wait()

    def rows(ref, start, count):
        return ref.at[pl.ds(pl.multiple_of(start * SUB, SUB), count * SUB)]

    def for_each_pad_copy(act):
        for e in range(n_experts):
            cur = pstart_ref[e]
            length = plen_ref[e]
            p = 1
            while p < tile:
                take = (length & p) != 0

                @pl.when(take)
                def _():
                    act(pltpu.make_async_copy(rows(zero_ref, 0, p), rows(xs_ref, cur, p), zsem))
                cur = cur + jnp.where(take, p, 0)
                p *= 2
        for i in range(n_experts):
            t = used_ref[0] + i

            @pl.when(t < n_tiles)
            def _():
                act(pltpu.make_async_copy(zero_ref, rows(xs_ref, t * tile, tile), zsem))

    @pl.when(pl.program_id(0) == pl.num_programs(0) - 1)
    def _():
        zero_ref[...] = jnp.zeros_like(zero_ref)
        for_each_pad_copy(lambda cp: cp.start())
        for_each_pad_copy(lambda cp: cp.wait())


def _dispatch(hn, slots, pad_start, pad_len, used_tiles, n_tiles, tile, tm=1024):
    n = hn.shape[0] // SUB
    tm = min(tm, n)
    n_experts = pad_start.shape[0]
    smem = pl.BlockSpec(memory_space=pltpu.SMEM)
    slot_spec = pl.BlockSpec((tm,), lambda i: (i,), memory_space=pltpu.SMEM)
    return pl.pallas_call(
        functools.partial(_dispatch_kernel, n_experts=n_experts, tile=tile),
        grid=(n // tm,),
        in_specs=[
            slot_spec, slot_spec,
            smem, smem, smem,
            pl.BlockSpec((tm * SUB, V7X_LANES), lambda i: (i, 0)),
        ],
        out_specs=pl.BlockSpec(memory_space=pl.ANY),
        out_shape=jax.ShapeDtypeStruct((n_tiles * tile * SUB, V7X_LANES), F32),
        scratch_shapes=[
            pltpu.VMEM((tile * SUB, V7X_LANES), F32),
            pltpu.SemaphoreType.DMA(()),
            pltpu.SemaphoreType.DMA(()),
        ],
        compiler_params=_cparams(1),
        name="dispatch",
    )(slots[0], slots[1], pad_start, pad_len, used_tiles, hn)


def _grouped_kernel(exp_ref, used_ref, nsub_ref, x_ref, w1_ref, w3_ref, w2_ref, o_ref,
                    xb_ref, acc_ref, *, sub):
    i = pl.program_id(0)
    j = pl.program_id(1)
    last = pl.num_programs(1) - 1
    valid = i < used_ref[0]
    tile = acc_ref.shape[0]

    @pl.when(valid & (j == 0))
    def _():
        for c in range(SUB):
            xb_ref[:, c * V7X_LANES:(c + 1) * V7X_LANES] = _load_row_tiles(x_ref, tile, c).astype(BF16)
        acc_ref[...] = jnp.zeros_like(acc_ref)

    @pl.when(valid)
    def _():
        def swiglu_rows(n_rows):
            rows = slice(0, n_rows)
            xb = xb_ref[rows, :]
            h = _silu(_dot(xb, w1_ref[...])) * _dot(xb, w3_ref[...])
            acc_ref[rows, :] += _dot(h.astype(BF16), w2_ref[...])

        for n_sub in range(1, tile // sub + 1):
            @pl.when(nsub_ref[i] == n_sub)
            def _():
                swiglu_rows(n_sub * sub)

    @pl.when(valid & (j == last))
    def _():
        _store_row_tiles(o_ref, acc_ref[...])

    @pl.when(jnp.logical_not(valid) & (j == last))
    def _():
        o_ref[...] = jnp.zeros_like(o_ref)


def _grouped(xs, w1, w3, w2, tile_exp, used_tiles, tile_nsub, tile, sub, tf=512):
    d = w1.shape[1]
    n_tiles = xs.shape[0] // (tile * SUB)
    nj = w1.shape[2] // tf

    def jsel(i, j, used):
        return jnp.where(i < used[0], j, nj - 1)

    grid_spec = pltpu.PrefetchScalarGridSpec(
        num_scalar_prefetch=3,
        grid=(n_tiles, nj),
        in_specs=[
            pl.BlockSpec((tile * SUB, V7X_LANES), lambda i, j, ex, us, ns: (i, 0)),
            pl.BlockSpec((None, d, tf), lambda i, j, ex, us, ns: (ex[i], 0, jsel(i, j, us))),
            pl.BlockSpec((None, d, tf), lambda i, j, ex, us, ns: (ex[i], 0, jsel(i, j, us))),
            pl.BlockSpec((None, tf, d), lambda i, j, ex, us, ns: (ex[i], jsel(i, j, us), 0)),
        ],
        out_specs=pl.BlockSpec((tile * SUB, V7X_LANES), lambda i, j, ex, us, ns: (i, 0)),
        scratch_shapes=[pltpu.VMEM((tile, d), BF16), pltpu.VMEM((tile, d), F32)],
    )
    return pl.pallas_call(
        functools.partial(_grouped_kernel, sub=sub),
        grid_spec=grid_spec,
        out_shape=jax.ShapeDtypeStruct(xs.shape, F32),
        compiler_params=_cparams(2),
        name="grouped_swiglu",
    )(tile_exp, used_tiles, tile_nsub, xs, w1, w3, w2)


def _combine_kernel(slot0_ref, slot1_ref, nslot0_ref, nslot1_ref, h_ref, gate_ref, g_ref, ys_ref,
                    o_ref, buf_a_ref, buf_b_ref, sems):
    tm, d = h_ref.shape
    i = pl.program_id(0)

    def start_gather(slot_refs, buf_ref, sem):
        def row_copy(r, k):
            return pltpu.make_async_copy(
                _row_tile(ys_ref, slot_refs[k][r]), _row_tile(buf_ref.at[k], r), sem)

        _start_row_copies(tm, row_copy)

    def finish(buf_ref, sem):
        for k in range(TOP_K):
            pltpu.make_async_copy(ys_ref.at[pl.ds(0, tm * SUB)], buf_ref.at[k], sem).wait()
        for r0 in range(0, tm, COMBINE_ROWS):
            rows = slice(r0, r0 + COMBINE_ROWS)
            gate = gate_ref[rows, :]
            ss = jnp.zeros((COMBINE_ROWS, 1), F32)
            for c in range(SUB):
                lanes = slice(c * V7X_LANES, (c + 1) * V7X_LANES)
                tiles = pl.ds(r0 * SUB + c, COMBINE_ROWS, stride=SUB)
                piece = (h_ref[rows, lanes] + gate[:, 0:1] * buf_ref[0, tiles, :]
                         + gate[:, 1:2] * buf_ref[1, tiles, :])
                o_ref[rows, lanes] = piece
                ss = ss + jnp.sum(piece * piece, axis=-1, keepdims=True)
            o_ref[rows, :] = o_ref[rows, :] * lax.rsqrt(ss / d + EPS) * g_ref[...]

    @pl.when(i == 0)
    def _():
        start_gather((slot0_ref, slot1_ref), buf_a_ref, sems.at[0])

    bufs = (buf_a_ref, buf_b_ref)
    for parity in range(2):
        @pl.when(i % 2 == parity)
        def _():
            @pl.when(i + 1 < pl.num_programs(0))
            def _():
                start_gather((nslot0_ref, nslot1_ref), bufs[1 - parity], sems.at[1 - parity])

            finish(bufs[parity], sems.at[parity])


def _combine(h2, gates, slots, ys, g, tm=1024):
    n, d = h2.shape
    tm = min(tm, n)
    n_steps = n // tm
    slot_spec = pl.BlockSpec((tm,), lambda i: (i,), memory_space=pltpu.SMEM)
    next_spec = pl.BlockSpec((tm,), lambda i: (jnp.minimum(i + 1, n_steps - 1),),
                             memory_space=pltpu.SMEM)
    return pl.pallas_call(
        _combine_kernel,
        grid=(n_steps,),
        in_specs=[
            slot_spec, slot_spec, next_spec, next_spec,
            pl.BlockSpec((tm, d), lambda i: (i, 0)),
            pl.BlockSpec((tm, TOP_K), lambda i: (i, 0)),
            _const_spec((1, d)),
            pl.BlockSpec(memory_space=pl.ANY),
        ],
        out_specs=pl.BlockSpec((tm, d), lambda i: (i, 0)),
        out_shape=jax.ShapeDtypeStruct((n, d), F32),
        scratch_shapes=[pltpu.VMEM((TOP_K, tm * SUB, V7X_LANES), F32),
                        pltpu.VMEM((TOP_K, tm * SUB, V7X_LANES), F32),
                        pltpu.SemaphoreType.DMA((2,))],
        compiler_params=_cparams(1),
        name="combine",
    )(slots[0], slots[1], slots[0], slots[1], h2, gates, g.reshape(1, d), ys)


def _moe(h2, g_ffn, w_router, w1, w3, w2, g_final, tile=1024, sub=256):
    n, d = h2.shape
    n_experts = w_router.shape[1]
    tile = min(tile, n)
    n_tiles = TOP_K * n // tile + n_experts
    hn, meta, gates, counts = _router(h2, g_ffn, w_router)
    expert, rank = meta[0:TOP_K], meta[TOP_K:2 * TOP_K]
    counts = counts[0, :n_experts]

    tiles_per = (counts + tile - 1) // tile
    ends = jnp.cumsum(tiles_per)
    starts = ends - tiles_per
    used = ends[-1:]
    expert, slots = expert.reshape(-1, V7X_LANES), rank.reshape(-1, V7X_LANES)
    for e in range(n_experts):
        slots = slots + jnp.where(expert == e, starts[e] * tile, 0)
    slots = slots.astype(I32).reshape(TOP_K, n)
    pad_start = (starts * tile + counts).astype(I32)
    pad_len = (tiles_per * tile - counts).astype(I32)
    tidx = jnp.minimum(jnp.arange(n_tiles, dtype=I32), used - 1)
    tile_exp = jnp.sum((tidx[:, None] >= ends[None, :]).astype(I32), axis=1)
    sub = min(sub, tile)
    tile_rows = jnp.clip(counts[tile_exp] - (tidx - starts[tile_exp]) * tile, 0, tile)
    tile_nsub = (tile_rows + sub - 1) // sub

    xs = _dispatch(hn, slots, pad_start, pad_len, used.astype(I32), n_tiles, tile)
    ys = _grouped(xs, w1, w3, w2, tile_exp.astype(I32), used.astype(I32), tile_nsub.astype(I32),
                  tile, sub)
    return _combine(h2, gates, slots, ys, g_final)


def kernel(x, ev_norm_mix, ev_w_in, ev_conv_w, hgrn_lower_bounds, ev_gnorm, ev_w_out,
           ev_norm_ffn, ev_ffn_w1, ev_ffn_w3, ev_ffn_w2,
           od_norm_mix, od_w_pw1, od_b_pw1, od_w_dw, od_b_dw, od_ln_g, od_ln_b,
           od_w_pw2, od_b_pw2, od_norm_ffn, od_router, od_moe_w1, od_moe_w3, od_moe_w2,
           final_norm):
    bsz, seq, d = x.shape
    assert ev_w_in.shape[0] == 1 and od_w_pw1.shape[0] == 1
    n_experts, _, n_ff = od_moe_w1.shape[1:]
    flat = lambda w: w.reshape(-1, w.shape[-1])
    h, (w1b, f1b, f3b, f2b, pw1b, pw2b) = _mixer0(
        x, ev_norm_mix[0], ev_w_in[0], ev_conv_w[0], hgrn_lower_bounds, ev_gnorm[0], ev_w_out[0],
        layer=0, cast_ws=[flat(od_moe_w1[0]), ev_ffn_w1[0], ev_ffn_w3[0], ev_ffn_w2[0],
                          od_w_pw1[0], od_w_pw2[0]])
    h, w3b = _ffn(h.reshape(bsz * seq, d), ev_norm_ffn[0], f1b, f3b, f2b,
                  cast_w=flat(od_moe_w3[0]))
    h, w2b = _conformer(h.reshape(bsz, seq, d), od_norm_mix[0], pw1b, od_b_pw1[0],
                        od_w_dw[0], od_b_dw[0], od_ln_g[0], od_ln_b[0], pw2b, od_b_pw2[0],
                        cast_w=flat(od_moe_w2[0]))
    out = _moe(h.reshape(bsz * seq, d), od_norm_ffn[0], od_router[0],
               w1b.reshape(n_experts, d, n_ff), w3b.reshape(n_experts, d, n_ff),
               w2b.reshape(n_experts, n_ff, d), final_norm)
    return out.reshape(bsz, seq, d)
```
